```python
import math
import jax
import jax.numpy as jnp
from jax import lax
import numpy as np

D_MODEL = 2048
BATCH = 1
SEQ = 16384
DEPTH = 1
DEC_BATCH = 32
DEC_SEQ = 8
PAST_LEN = 16384
PAGE_SIZE = 128

HEAD_DIM = 128
ATTN_HEADS = (D_MODEL // 2) // HEAD_DIM
KV_HEADS = 2
GROUP = ATTN_HEADS // KV_HEADS
IDX_HEADS = 16
IDX_DIM = 64
INDEX_TOPK = 256
Q_BLOCK = 128
IDX_SCALE = (IDX_HEADS * IDX_DIM) ** -0.5
GDN_DK = 128
GDN_DV = 128
GDN_HEADS = (D_MODEL // 2) // GDN_DV
CONV_W = 4
CONV_DIM = GDN_HEADS * (2 * GDN_DK + GDN_DV)
GDN_CHUNK = 64
MIX_WIDTH = ATTN_HEADS * HEAD_DIM + GDN_HEADS * GDN_DV
IN_SIZES = (ATTN_HEADS * HEAD_DIM, KV_HEADS * HEAD_DIM, KV_HEADS * HEAD_DIM,
            IDX_HEADS * IDX_DIM, IDX_DIM, IDX_HEADS,
            CONV_DIM, GDN_HEADS, GDN_HEADS, GDN_HEADS * GDN_DV)
IN_COLS = sum(IN_SIZES)
PEER_HEADS = 8
PEER_NKEYS = 128
PEER_EXPERTS = PEER_NKEYS * PEER_NKEYS
PEER_QDIM = 256
PEER_TOPK = 16
PEER_BLOCK = 128
NORM_EPS = 1e-6

kernel_name = 'hymba_dsa_gdn_peer_step'


def _rmsnorm(x, w):
    xf = x.astype(jnp.float32)
    y = xf * lax.rsqrt(jnp.mean(xf * xf, axis=-1, keepdims=True) + NORM_EPS)
    return (y * w.astype(jnp.float32)).astype(x.dtype)


def _l2norm(x):
    return x * lax.rsqrt(jnp.sum(x * x, axis=-1, keepdims=True) + NORM_EPS)


def _in_projection(x, attn_norm_w, w_in, q_norm_w, k_norm_w, idx_k_norm_w):
    B, T, _ = x.shape
    z = _rmsnorm(x, attn_norm_w) @ w_in
    offs = []
    acc = 0
    for s in IN_SIZES[:-1]:
        acc += s
        offs.append(acc)
    qa, ka, va, qi, ki, wi, qkv, b_raw, a_raw, gate = jnp.split(z, offs, axis=-1)
    qa = _rmsnorm(qa.reshape(B, T, ATTN_HEADS, HEAD_DIM), q_norm_w)
    ka = _rmsnorm(ka.reshape(B, T, KV_HEADS, HEAD_DIM), k_norm_w)
    va = va.reshape(B, T, KV_HEADS, HEAD_DIM)
    qi = qi.reshape(B, T, IDX_HEADS, IDX_DIM)
    ki = _rmsnorm(ki, idx_k_norm_w)
    return (qa, ka, va, qi, ki, wi), (qkv, b_raw, a_raw, gate)


def _index_scores(qi, wi, ki):
    s = jax.nn.relu(jnp.einsum('bqhd,bsd->bqhs', qi, ki).astype(jnp.float32))
    return jnp.einsum('bqhs,bqh->bqs', s, wi.astype(jnp.float32)) * IDX_SCALE


def _gathered_attention(q, kg, vg, valid):
    B, Q = q.shape[:2]
    qg = q.reshape(B, Q, KV_HEADS, GROUP, HEAD_DIM)
    s = jnp.einsum('bqngd,bqknd->bqngk', qg, kg).astype(jnp.float32) * (HEAD_DIM ** -0.5)
    s = jnp.where(valid[:, :, None, None, :], s, -jnp.inf)
    p = jax.nn.softmax(s, axis=-1)
    o = jnp.einsum('bqngk,bqknd->bqngd', p.astype(vg.dtype), vg)
    return o.reshape(B, Q, ATTN_HEADS * HEAD_DIM)


def _dsa_prompt(q, k, v, qi, ki, wi):
    B, T = q.shape[:2]
    ktop = min(INDEX_TOPK, T // 4)
    nb = T // Q_BLOCK
    bidx = jnp.arange(B)[:, None, None]
    spos = jnp.arange(T)

    def blocks(a):
        return jnp.moveaxis(a.reshape((B, nb, Q_BLOCK) + a.shape[2:]), 1, 0)

    def blk(xs):
        qb, qib, wib, start = xs
        tpos = start + jnp.arange(Q_BLOCK)
        sc = _index_scores(qib, wib, ki)
        sc = jnp.where(spos[None, None, :] <= tpos[None, :, None], sc, -jnp.inf)
        _, idx = lax.top_k(sc, ktop)
        valid = idx <= tpos[None, :, None]
        return _gathered_attention(qb, k[bidx, idx], v[bidx, idx], valid)

    o = lax.map(blk, (blocks(q), blocks(qi), blocks(wi), jnp.arange(nb) * Q_BLOCK))
    return jnp.moveaxis(o, 0, 1).reshape(B, T, ATTN_HEADS * HEAD_DIM)


def _dsa_sample(q, k, v, qi, ki, wi, cache_k, cache_v, cache_idx_k, page_table):
    B, T = q.shape[:2]
    past = page_table.shape[1] * PAGE_SIZE
    L = past + T
    ktop = min(INDEX_TOPK, L // 4)
    ki_past = cache_idx_k[page_table].reshape(B, past, IDX_DIM).astype(ki.dtype)
    ki_all = jnp.concatenate([ki_past, ki], axis=1)
    tpos = past + jnp.arange(T)
    sc = _index_scores(qi, wi, ki_all)
    sc = jnp.where(jnp.arange(L)[None, None, :] <= tpos[None, :, None], sc, -jnp.inf)
    _, idx = lax.top_k(sc, ktop)
    valid = idx <= tpos[None, :, None]
    bidx = jnp.arange(B)[:, None, None]
    in_past = (idx < past)[..., None, None]
    pidx = jnp.minimum(idx, past - 1)
    phys = page_table[bidx, pidx // PAGE_SIZE]
    off = pidx % PAGE_SIZE
    nidx = jnp.clip(idx - past, 0, T - 1)
    kg = jnp.where(in_past, cache_k[phys, off].astype(k.dtype), k[bidx, nidx])
    vg = jnp.where(in_past, cache_v[phys, off].astype(v.dtype), v[bidx, nidx])
    return _gathered_attention(q, kg, vg, valid)


def _causal_conv(x, buf, conv_w):
    T = x.shape[1]
    xp = jnp.concatenate([buf.astype(x.dtype), x], axis=1)
    y = xp[:, 0:T] * conv_w[0]
    for j in range(1, CONV_W):
        y = y + xp[:, j:j + T] * conv_w[j]
    return jax.nn.silu(y), xp[:, T:]


def _chunk_gated_delta(q, k, v, g, beta, S0):
    B, T, H, DK = q.shape
    DV = v.shape[-1]
    C = min(GDN_CHUNK, T)
    n = -(-T // C)
    pad = n * C - T

    def prep(a):
        a = jnp.pad(a, [(0, 0), (0, pad)] + [(0, 0)] * (a.ndim - 2))
        a = a.reshape((B, n, C) + a.shape[2:])
        return jnp.swapaxes(jnp.moveaxis(a, 1, 0), 2, 3)

    q, k, v, g, beta = prep(q), prep(k), prep(v), prep(g), prep(beta)
    gc = jnp.cumsum(g, axis=-1)
    causal = jnp.tril(jnp.ones((C, C), bool))
    strict = jnp.tril(jnp.ones((C, C), bool), -1)
    decay = jnp.exp(jnp.where(causal, gc[..., :, None] - gc[..., None, :], -jnp.inf))
    kb = k * beta[..., None]
    lower = jnp.where(strict, jnp.einsum('...id,...jd->...ij', kb, k) * decay, 0.0)
    amat = lower + jnp.eye(C, dtype=jnp.float32)
    rhs = jnp.concatenate([v * beta[..., None], kb * jnp.exp(gc)[..., None]], axis=-1)
    sol = lax.linalg.triangular_solve(amat, rhs, left_side=True, lower=True, unit_diagonal=True)
    u, w = sol[..., :DV], sol[..., DV:]
    qk = jnp.einsum('...id,...jd->...ij', q, k) * decay
    qg = q * jnp.exp(gc)[..., None]
    kd = k * jnp.exp(gc[..., -1:] - gc)[..., None]
    g_last = jnp.exp(gc[..., -1])

    def step(S, xs):
        qg_i, kd_i, u_i, w_i, qk_i, gl_i = xs
        v_new = u_i - jnp.einsum('bhck,bhkv->bhcv', w_i, S)
        o = jnp.einsum('bhck,bhkv->bhcv', qg_i, S) + jnp.einsum('bhcj,bhjv->bhcv', qk_i, v_new)
        S = S * gl_i[..., None, None] + jnp.einsum('bhck,bhcv->bhkv', kd_i, v_new)
        return S, o

    S, o = lax.scan(step, S0, (qg, kd, u, w, qk, g_last))
    o = jnp.moveaxis(jnp.swapaxes(o, 2, 3), 0, 1).reshape(B, n * C, H, DV)[:, :T]
    return o, S


def _gated_deltanet(qkv, b_raw, a_raw, gate, conv_buf, S0, conv_w, a_log, dt_bias, gdn_norm_w):
    B, T, _ = qkv.shape
    f32 = jnp.float32
    conv, new_buf = _causal_conv(qkv, conv_buf, conv_w)
    qc, kc, vc = jnp.split(conv, [GDN_HEADS * GDN_DK, 2 * GDN_HEADS * GDN_DK], axis=-1)
    q = _l2norm(qc.reshape(B, T, GDN_HEADS, GDN_DK).astype(f32)) * (GDN_DK ** -0.5)
    k = _l2norm(kc.reshape(B, T, GDN_HEADS, GDN_DK).astype(f32))
    v = vc.reshape(B, T, GDN_HEADS, GDN_DV).astype(f32)
    beta = jax.nn.sigmoid(b_raw.astype(f32))
    g = -jnp.exp(a_log.astype(f32)) * jax.nn.softplus(a_raw.astype(f32) + dt_bias.astype(f32))
    o, S = _chunk_gated_delta(q, k, v, g, beta, S0)
    o = _rmsnorm(o, gdn_norm_w) * jax.nn.silu(gate.reshape(B, T, GDN_HEADS, GDN_DV).astype(f32))
    return o.reshape(B, T, GDN_HEADS * GDN_DV).astype(qkv.dtype), S, new_buf


def _peer(xn, peer_wq, peer_keys, peer_u, peer_v):
    shape = xn.shape
    xf = xn.reshape(-1, D_MODEL)
    N = xf.shape[0]
    nb = -(-N // PEER_BLOCK)
    xb = jnp.pad(xf, ((0, nb * PEER_BLOCK - N), (0, 0))).reshape(nb, PEER_BLOCK, D_MODEL)
    ncand = PEER_TOPK * PEER_TOPK

    def blk(x):
        qh = (x @ peer_wq).reshape(PEER_BLOCK, PEER_HEADS, 2, PEER_QDIM // 2)
        s1 = jnp.einsum('thd,hkd->thk', qh[:, :, 0], peer_keys[0]).astype(jnp.float32)
        s2 = jnp.einsum('thd,hkd->thk', qh[:, :, 1], peer_keys[1]).astype(jnp.float32)
        v1, i1 = lax.top_k(s1, PEER_TOPK)
        v2, i2 = lax.top_k(s2, PEER_TOPK)
        cand = (v1[..., :, None] + v2[..., None, :]).reshape(PEER_BLOCK, PEER_HEADS, ncand)
        cidx = (i1[..., :, None] * PEER_NKEYS + i2[..., None, :]).reshape(PEER_BLOCK, PEER_HEADS, ncand)
        sv, si = lax.top_k(cand, PEER_TOPK)
        eidx = jnp.take_along_axis(cidx, si, axis=-1)
        gsm = jax.nn.softmax(sv, axis=-1)
        act = jax.nn.gelu(jnp.einsum('thkd,td->thk', peer_u[eidx], x).astype(jnp.float32), approximate=False)
        return jnp.einsum('thk,thkd->td', (gsm * act).astype(x.dtype), peer_v[eidx])

    y = lax.map(blk, xb).reshape(nb * PEER_BLOCK, D_MODEL)[:N]
    return y.reshape(shape)


def _layer_out(x, oa, ob, w_out, ffn_norm_w, peer_wq, peer_keys, peer_u, peer_v):
    h = x + jnp.concatenate([oa, ob], axis=-1) @ w_out
    return h + _peer(_rmsnorm(h, ffn_norm_w), peer_wq, peer_keys, peer_u, peer_v)


def setup_inputs(seed: int = 0) -> dict:
    key = jax.random.key(seed)
    ks = jax.random.split(key, 24)
    f32 = jnp.float32
    n_pages = PAST_LEN // PAGE_SIZE
    n_used = DEC_BATCH * n_pages
    n_pool = n_used + n_used // 4

    def nrm(k, shape, scale):
        return jax.random.normal(k, shape, f32) * scale

    def gain(k, shape):
        return 1.0 + 0.02 * jax.random.normal(k, shape, f32)

    page_table = jax.random.permutation(ks[7], n_pool)[:n_used].reshape(DEC_BATCH, n_pages).astype(jnp.int32)
    dt = jnp.exp(jax.random.uniform(ks[14], (DEPTH, GDN_HEADS), f32, math.log(1e-3), math.log(1e-1)))
    return {
        'x_prompt': nrm(ks[0], (BATCH, SEQ, D_MODEL), 1.0),
        'x_sample': nrm(ks[1], (DEC_BATCH, DEC_SEQ, D_MODEL), 1.0),
        'cache_k': nrm(ks[2], (DEPTH, n_pool, PAGE_SIZE, KV_HEADS, HEAD_DIM), 1.0),
        'cache_v': nrm(ks[3], (DEPTH, n_pool, PAGE_SIZE, KV_HEADS, HEAD_DIM), 1.0),
        'cache_idx_k': nrm(ks[4], (DEPTH, n_pool, PAGE_SIZE, IDX_DIM), 1.0),
        'state_ssm': nrm(ks[5], (DEPTH, DEC_BATCH, GDN_HEADS, GDN_DK, GDN_DV), 0.5),
        'state_conv': nrm(ks[6], (DEPTH, DEC_BATCH, CONV_W - 1, CONV_DIM), 1.0),
        'page_table': page_table,
        'attn_norm_w': gain(ks[8], (DEPTH, D_MODEL)),
        'w_in': nrm(ks[9], (DEPTH, D_MODEL, IN_COLS), D_MODEL ** -0.5),
        'q_norm_w': gain(ks[10], (DEPTH, HEAD_DIM)),
        'k_norm_w': gain(ks[11], (DEPTH, HEAD_DIM)),
        'idx_k_norm_w': gain(ks[12], (DEPTH, IDX_DIM)),
        'conv_w': nrm(ks[13], (DEPTH, CONV_W, CONV_DIM), CONV_W ** -0.5),
        'a_log': jnp.log(jax.random.uniform(ks[15], (DEPTH, GDN_HEADS), f32, 1.0, 16.0)),
        'dt_bias': dt + jnp.log(-jnp.expm1(-dt)),
        'gdn_norm_w': gain(ks[16], (DEPTH, GDN_DV)),
        'w_out': nrm(ks[17], (DEPTH, MIX_WIDTH, D_MODEL), MIX_WIDTH ** -0.5),
        'ffn_norm_w': gain(ks[18], (DEPTH, D_MODEL)),
        'peer_wq': nrm(ks[19], (DEPTH, D_MODEL, PEER_HEADS * PEER_QDIM), D_MODEL ** -0.5),
        'peer_keys': nrm(ks[20], (DEPTH, 2, PEER_HEADS, PEER_NKEYS, PEER_QDIM // 2), (PEER_QDIM // 2) ** -0.5),
        'peer_u': nrm(ks[21], (DEPTH, PEER_EXPERTS, D_MODEL), D_MODEL ** -0.5),
        'peer_v': nrm(ks[22], (DEPTH, PEER_EXPERTS, D_MODEL), PEER_HEADS ** -0.5),
    }


def reference(x_prompt, x_sample, cache_k, cache_v, cache_idx_k, state_ssm, state_conv, page_table,
              attn_norm_w, w_in, q_norm_w, k_norm_w, idx_k_norm_w, conv_w, a_log, dt_bias, gdn_norm_w,
              w_out, ffn_norm_w, peer_wq, peer_keys, peer_u, peer_v):
    hp, hs = x_prompt, x_sample
    kp_l, vp_l, ip_l, sp_l, cp_l = [], [], [], [], []
    ks_l, vs_l, is_l, ss_l, cs_l = [], [], [], [], []
    for l in range(DEPTH):
        proj_w = (attn_norm_w[l], w_in[l], q_norm_w[l], k_norm_w[l], idx_k_norm_w[l])
        gdn_w = (conv_w[l], a_log[l], dt_bias[l], gdn_norm_w[l])
        out_w = (w_out[l], ffn_norm_w[l], peer_wq[l], peer_keys[l], peer_u[l], peer_v[l])

        (qa, ka, va, qi, ki, wi), gdn_in = _in_projection(hp, *proj_w)
        oa = _dsa_prompt(qa, ka, va, qi, ki, wi)
        Bp = hp.shape[0]
        buf0 = jnp.zeros((Bp, CONV_W - 1, CONV_DIM), hp.dtype)
        S0 = jnp.zeros((Bp, GDN_HEADS, GDN_DK, GDN_DV), jnp.float32)
        ob, S_p, buf_p = _gated_deltanet(*gdn_in, buf0, S0, *gdn_w)
        hp = _layer_out(hp, oa, ob, *out_w)
        kp_l.append(ka)
        vp_l.append(va)
        ip_l.append(ki)
        sp_l.append(S_p.astype(state_ssm.dtype))
        cp_l.append(buf_p.astype(state_conv.dtype))

        (qa, ka, va, qi, ki, wi), gdn_in = _in_projection(hs, *proj_w)
        oa = _dsa_sample(qa, ka, va, qi, ki, wi, cache_k[l], cache_v[l], cache_idx_k[l], page_table)
        ob, S_s, buf_s = _gated_deltanet(*gdn_in, state_conv[l], state_ssm[l].astype(jnp.float32), *gdn_w)
        hs = _layer_out(hs, oa, ob, *out_w)
        ks_l.append(ka)
        vs_l.append(va)
        is_l.append(ki)
        ss_l.append(S_s.astype(state_ssm.dtype))
        cs_l.append(buf_s.astype(state_conv.dtype))

    y_prompt, y_sample = hp, hs
    k_prompt, v_prompt, idx_k_prompt = jnp.stack(kp_l), jnp.stack(vp_l), jnp.stack(ip_l)
    ssm_prompt, conv_prompt = jnp.stack(sp_l), jnp.stack(cp_l)
    k_sample, v_sample, idx_k_sample = jnp.stack(ks_l), jnp.stack(vs_l), jnp.stack(is_l)
    ssm_sample, conv_sample = jnp.stack(ss_l), jnp.stack(cs_l)
    return (y_prompt, y_sample, k_prompt, v_prompt, idx_k_prompt, ssm_prompt, conv_prompt,
            k_sample, v_sample, idx_k_sample, ssm_sample, conv_sample)
```

```python
import functools
import math

import jax
import jax.numpy as jnp
from jax import lax
from jax.experimental import pallas as pl
from jax.experimental.pallas import tpu as pltpu

D_MODEL = 2048
PAGE_SIZE = 128
HEAD_DIM = 128
ATTN_HEADS = 8
KV_HEADS = 2
GROUP = ATTN_HEADS // KV_HEADS
IDX_HEADS = 16
IDX_DIM = 64
INDEX_TOPK = 256
Q_BLOCK = 128
IDX_SCALE = (IDX_HEADS * IDX_DIM) ** -0.5
GDN_DK = 128
GDN_DV = 128
GDN_HEADS = 8
CONV_W = 4
CONV_DIM = GDN_HEADS * (2 * GDN_DK + GDN_DV)
GDN_CHUNK = 64
IN_SIZES = (ATTN_HEADS * HEAD_DIM, KV_HEADS * HEAD_DIM, KV_HEADS * HEAD_DIM,
            IDX_HEADS * IDX_DIM, IDX_DIM, IDX_HEADS,
            CONV_DIM, GDN_HEADS, GDN_HEADS, GDN_HEADS * GDN_DV)
IN_COLS = sum(IN_SIZES)
PEER_HEADS = 8
PEER_NKEYS = 128
PEER_QDIM = 256
PEER_TOPK = 16
PEER_BLOCK = 128
NORM_EPS = 1e-6

LANES = 128


def _rmsnorm(x, w):
    xf = x.astype(jnp.float32)
    y = xf * lax.rsqrt(jnp.mean(xf * xf, axis=-1, keepdims=True) + NORM_EPS)
    return (y * w.astype(jnp.float32)).astype(x.dtype)


def _l2norm(x):
    return x * lax.rsqrt(jnp.sum(x * x, axis=-1, keepdims=True) + NORM_EPS)


def _norm_matmul_body(x_ref, g_ref, w_ref, o_ref, xn_ref):
    @pl.when(pl.program_id(1) == 0)
    def _():
        x = x_ref[...]
        r = lax.rsqrt(jnp.mean(x * x, axis=-1, keepdims=True) + NORM_EPS)
        xn_ref[...] = (x * r * g_ref[...]).astype(jnp.bfloat16)

    o_ref[...] = jnp.dot(xn_ref[...], w_ref[...], preferred_element_type=jnp.float32)


def _norm_matmul(x, g, w, tm, tn):
    m, k = x.shape
    n = w.shape[1]
    return pl.pallas_call(
        _norm_matmul_body,
        grid=(m // tm, n // tn),
        in_specs=[pl.BlockSpec((tm, k), lambda i, j: (i, 0)),
                  pl.BlockSpec((1, k), lambda i, j: (0, 0)),
                  pl.BlockSpec((k, tn), lambda i, j: (0, j))],
        out_specs=pl.BlockSpec((tm, tn), lambda i, j: (i, j)),
        out_shape=jax.ShapeDtypeStruct((m, n), jnp.float32),
        scratch_shapes=[pltpu.VMEM((tm, k), jnp.bfloat16)],
        compiler_params=pltpu.CompilerParams(
            dimension_semantics=("arbitrary", "arbitrary"),
            vmem_limit_bytes=48 * 1024 * 1024),
        name="norm_matmul",
    )(x, g.reshape(1, k), w)


def _in_projection(x, attn_norm_w, w_in_p, q_norm_w, k_norm_w, idx_k_norm_w):
    B, T, _ = x.shape
    m = B * T
    tm = 512 if m % 512 == 0 else m
    z = _norm_matmul(x.reshape(m, D_MODEL), attn_norm_w, w_in_p, tm, 1024)
    z = z[:, :IN_COLS].reshape(B, T, IN_COLS)
    offs = []
    acc = 0
    for s in IN_SIZES[:-1]:
        acc += s
        offs.append(acc)
    qa, ka, va, qi, ki, wi, qkv, b_raw, a_raw, gate = jnp.split(z, offs, axis=-1)
    qa = _rmsnorm(qa.reshape(B, T, ATTN_HEADS, HEAD_DIM), q_norm_w)
    ka = _rmsnorm(ka.reshape(B, T, KV_HEADS, HEAD_DIM), k_norm_w)
    va = va.reshape(B, T, KV_HEADS, HEAD_DIM)
    qi = qi.reshape(B, T, IDX_HEADS, IDX_DIM)
    ki = _rmsnorm(ki, idx_k_norm_w)
    return (qa, ka, va, qi, ki, wi), (qkv, b_raw, a_raw, gate)


def _index_scores(qi, wi, ki):
    s = jax.nn.relu(jnp.einsum('bqhd,bsd->bqhs', qi, ki).astype(jnp.float32))
    return jnp.einsum('bqhs,bqh->bqs', s, wi.astype(jnp.float32)) * IDX_SCALE


def _gathered_attention(q, kg, vg, valid):
    B, Q = q.shape[:2]
    qg = q.reshape(B, Q, KV_HEADS, GROUP, HEAD_DIM)
    s = jnp.einsum('bqngd,bqknd->bqngk', qg, kg).astype(jnp.float32) * (HEAD_DIM ** -0.5)
    s = jnp.where(valid[:, :, None, None, :], s, -jnp.inf)
    p = jax.nn.softmax(s, axis=-1)
    o = jnp.einsum('bqngk,bqknd->bqngd', p.astype(vg.dtype), vg)
    return o.reshape(B, Q, ATTN_HEADS * HEAD_DIM)


def _dsa_prompt(q, k, v, qi, ki, wi):
    B, T = q.shape[:2]
    ktop = min(INDEX_TOPK, T // 4)
    nb = T // Q_BLOCK
    bidx = jnp.arange(B)[:, None, None]
    spos = jnp.arange(T)

    def blocks(a):
        return jnp.moveaxis(a.reshape((B, nb, Q_BLOCK) + a.shape[2:]), 1, 0)

    def blk(xs):
        qb, qib, wib, start = xs
        tpos = start + jnp.arange(Q_BLOCK)
        sc = _index_scores(qib, wib, ki)
        sc = jnp.where(spos[None, None, :] <= tpos[None, :, None], sc, -jnp.inf)
        _, idx = lax.top_k(sc, ktop)
        valid = idx <= tpos[None, :, None]
        return _gathered_attention(qb, k[bidx, idx], v[bidx, idx], valid)

    o = lax.map(blk, (blocks(q), blocks(qi), blocks(wi), jnp.arange(nb) * Q_BLOCK))
    return jnp.moveaxis(o, 0, 1).reshape(B, T, ATTN_HEADS * HEAD_DIM)


def _dsa_sample(q, k, v, qi, ki, wi, cache_k, cache_v, cache_idx_k, page_table):
    B, T = q.shape[:2]
    past = page_table.shape[1] * PAGE_SIZE
    L = past + T
    ktop = min(INDEX_TOPK, L // 4)
    ki_past = cache_idx_k[page_table].reshape(B, past, IDX_DIM).astype(ki.dtype)
    ki_all = jnp.concatenate([ki_past, ki], axis=1)
    tpos = past + jnp.arange(T)
    sc = _index_scores(qi, wi, ki_all)
    sc = jnp.where(jnp.arange(L)[None, None, :] <= tpos[None, :, None], sc, -jnp.inf)
    _, idx = lax.top_k(sc, ktop)
    valid = idx <= tpos[None, :, None]
    bidx = jnp.arange(B)[:, None, None]
    in_past = (idx < past)[..., None, None]
    pidx = jnp.minimum(idx, past - 1)
    phys = page_table[bidx, pidx // PAGE_SIZE]
    off = pidx % PAGE_SIZE
    nidx = jnp.clip(idx - past, 0, T - 1)
    kg = jnp.where(in_past, cache_k[phys, off].astype(k.dtype), k[bidx, nidx])
    vg = jnp.where(in_past, cache_v[phys, off].astype(v.dtype), v[bidx, nidx])
    return _gathered_attention(q, kg, vg, valid)


def _causal_conv(x, buf, conv_w):
    T = x.shape[1]
    xp = jnp.concatenate([buf.astype(x.dtype), x], axis=1)
    y = xp[:, 0:T] * conv_w[0]
    for j in range(1, CONV_W):
        y = y + xp[:, j:j + T] * conv_w[j]
    return jax.nn.silu(y), xp[:, T:]


def _chunk_gated_delta(q, k, v, g, beta, S0):
    B, T, H, DK = q.shape
    DV = v.shape[-1]
    C = min(GDN_CHUNK, T)
    n = -(-T // C)
    pad = n * C - T

    def prep(a):
        a = jnp.pad(a, [(0, 0), (0, pad)] + [(0, 0)] * (a.ndim - 2))
        a = a.reshape((B, n, C) + a.shape[2:])
        return jnp.swapaxes(jnp.moveaxis(a, 1, 0), 2, 3)

    q, k, v, g, beta = prep(q), prep(k), prep(v), prep(g), prep(beta)
    gc = jnp.cumsum(g, axis=-1)
    causal = jnp.tril(jnp.ones((C, C), bool))
    strict = jnp.tril(jnp.ones((C, C), bool), -1)
    decay = jnp.exp(jnp.where(causal, gc[..., :, None] - gc[..., None, :], -jnp.inf))
    kb = k * beta[..., None]
    lower = jnp.where(strict, jnp.einsum('...id,...jd->...ij', kb, k) * decay, 0.0)
    amat = lower + jnp.eye(C, dtype=jnp.float32)
    rhs = jnp.concatenate([v * beta[..., None], kb * jnp.exp(gc)[..., None]], axis=-1)
    sol = lax.linalg.triangular_solve(amat, rhs, left_side=True, lower=True, unit_diagonal=True)
    u, w = sol[..., :DV], sol[..., DV:]
    qk = jnp.einsum('...id,...jd->...ij', q, k) * decay
    qg = q * jnp.exp(gc)[..., None]
    kd = k * jnp.exp(gc[..., -1:] - gc)[..., None]
    g_last = jnp.exp(gc[..., -1])

    def step(S, xs):
        qg_i, kd_i, u_i, w_i, qk_i, gl_i = xs
        v_new = u_i - jnp.einsum('bhck,bhkv->bhcv', w_i, S)
        o = jnp.einsum('bhck,bhkv->bhcv', qg_i, S) + jnp.einsum('bhcj,bhjv->bhcv', qk_i, v_new)
        S = S * gl_i[..., None, None] + jnp.einsum('bhck,bhcv->bhkv', kd_i, v_new)
        return S, o

    S, o = lax.scan(step, S0, (qg, kd, u, w, qk, g_last))
    o = jnp.moveaxis(jnp.swapaxes(o, 2, 3), 0, 1).reshape(B, n * C, H, DV)[:, :T]
    return o, S


def _gated_deltanet(qkv, b_raw, a_raw, gate, conv_buf, S0, conv_w, a_log, dt_bias, gdn_norm_w):
    B, T, _ = qkv.shape
    f32 = jnp.float32
    conv, new_buf = _causal_conv(qkv, conv_buf, conv_w)
    qc, kc, vc = jnp.split(conv, [GDN_HEADS * GDN_DK, 2 * GDN_HEADS * GDN_DK], axis=-1)
    q = _l2norm(qc.reshape(B, T, GDN_HEADS, GDN_DK).astype(f32)) * (GDN_DK ** -0.5)
    k = _l2norm(kc.reshape(B, T, GDN_HEADS, GDN_DK).astype(f32))
    v = vc.reshape(B, T, GDN_HEADS, GDN_DV).astype(f32)
    beta = jax.nn.sigmoid(b_raw.astype(f32))
    g = -jnp.exp(a_log.astype(f32)) * jax.nn.softplus(a_raw.astype(f32) + dt_bias.astype(f32))
    o, S = _chunk_gated_delta(q, k, v, g, beta, S0)
    o = _rmsnorm(o, gdn_norm_w) * jax.nn.silu(gate.reshape(B, T, GDN_HEADS, GDN_DV).astype(f32))
    return o.reshape(B, T, GDN_HEADS * GDN_DV).astype(qkv.dtype), S, new_buf


def _peer(xn, peer_wq, peer_keys, peer_u, peer_v):
    shape = xn.shape
    xf = xn.reshape(-1, D_MODEL)
    N = xf.shape[0]
    nb = -(-N // PEER_BLOCK)
    xb = jnp.pad(xf, ((0, nb * PEER_BLOCK - N), (0, 0))).reshape(nb, PEER_BLOCK, D_MODEL)
    ncand = PEER_TOPK * PEER_TOPK

    def blk(x):
        qh = (x @ peer_wq).reshape(PEER_BLOCK, PEER_HEADS, 2, PEER_QDIM // 2)
        s1 = jnp.einsum('thd,hkd->thk', qh[:, :, 0], peer_keys[0]).astype(jnp.float32)
        s2 = jnp.einsum('thd,hkd->thk', qh[:, :, 1], peer_keys[1]).astype(jnp.float32)
        v1, i1 = lax.top_k(s1, PEER_TOPK)
        v2, i2 = lax.top_k(s2, PEER_TOPK)
        cand = (v1[..., :, None] + v2[..., None, :]).reshape(PEER_BLOCK, PEER_HEADS, ncand)
        cidx = (i1[..., :, None] * PEER_NKEYS + i2[..., None, :]).reshape(PEER_BLOCK, PEER_HEADS, ncand)
        sv, si = lax.top_k(cand, PEER_TOPK)
        eidx = jnp.take_along_axis(cidx, si, axis=-1)
        gsm = jax.nn.softmax(sv, axis=-1)
        act = jax.nn.gelu(jnp.einsum('thkd,td->thk', peer_u[eidx], x).astype(jnp.float32), approximate=False)
        return jnp.einsum('thk,thkd->td', (gsm * act).astype(x.dtype), peer_v[eidx])

    y = lax.map(blk, xb).reshape(nb * PEER_BLOCK, D_MODEL)[:N]
    return y.reshape(shape)


def _layer_out(x, oa, ob, w_out, ffn_norm_w, peer_wq, peer_keys, peer_u, peer_v):
    h = x + jnp.concatenate([oa, ob], axis=-1) @ w_out
    return h + _peer(_rmsnorm(h, ffn_norm_w), peer_wq, peer_keys, peer_u, peer_v)


def kernel(x_prompt, x_sample, cache_k, cache_v, cache_idx_k, state_ssm, state_conv, page_table,
           attn_norm_w, w_in, q_norm_w, k_norm_w, idx_k_norm_w, conv_w, a_log, dt_bias, gdn_norm_w,
           w_out, ffn_norm_w, peer_wq, peer_keys, peer_u, peer_v):
    l = 0
    n_pad = -(-IN_COLS // 1024) * 1024
    w_in_p = jnp.pad(w_in[l], ((0, 0), (0, n_pad - IN_COLS))).astype(jnp.bfloat16)
    proj_w = (attn_norm_w[l], w_in_p, q_norm_w[l], k_norm_w[l], idx_k_norm_w[l])
    gdn_w = (conv_w[l], a_log[l], dt_bias[l], gdn_norm_w[l])
    out_w = (w_out[l], ffn_norm_w[l], peer_wq[l], peer_keys[l], peer_u[l], peer_v[l])

    hp, hs = x_prompt, x_sample
    (qa, ka, va, qi, ki, wi), gdn_in = _in_projection(hp, *proj_w)
    oa = _dsa_prompt(qa, ka, va, qi, ki, wi)
    Bp = hp.shape[0]
    buf0 = jnp.zeros((Bp, CONV_W - 1, CONV_DIM), hp.dtype)
    S0 = jnp.zeros((Bp, GDN_HEADS, GDN_DK, GDN_DV), jnp.float32)
    ob, S_p, buf_p = _gated_deltanet(*gdn_in, buf0, S0, *gdn_w)
    hp = _layer_out(hp, oa, ob, *out_w)
    kp, vp, ip = ka, va, ki

    (qa, ka, va, qi, ki, wi), gdn_in = _in_projection(hs, *proj_w)
    oa = _dsa_sample(qa, ka, va, qi, ki, wi, cache_k[l], cache_v[l], cache_idx_k[l], page_table)
    ob, S_s, buf_s = _gated_deltanet(*gdn_in, state_conv[l], state_ssm[l].astype(jnp.float32), *gdn_w)
    hs = _layer_out(hs, oa, ob, *out_w)

    return (hp, hs, kp[None], vp[None], ip[None], S_p[None], buf_p[None],
            ka[None], va[None], ki[None], S_s[None], buf_s[None])
```

```python
import functools
import math

import jax
import jax.numpy as jnp
from jax import lax
from jax.experimental import pallas as pl
from jax.experimental.pallas import tpu as pltpu

D_MODEL = 2048
PAGE_SIZE = 128
HEAD_DIM = 128
ATTN_HEADS = 8
KV_HEADS = 2
GROUP = ATTN_HEADS // KV_HEADS
IDX_HEADS = 16
IDX_DIM = 64
INDEX_TOPK = 256
Q_BLOCK = 128
IDX_SCALE = (IDX_HEADS * IDX_DIM) ** -0.5
GDN_DK = 128
GDN_DV = 128
GDN_HEADS = 8
CONV_W = 4
CONV_DIM = GDN_HEADS * (2 * GDN_DK + GDN_DV)
GDN_CHUNK = 64
IN_SIZES = (ATTN_HEADS * HEAD_DIM, KV_HEADS * HEAD_DIM, KV_HEADS * HEAD_DIM,
            IDX_HEADS * IDX_DIM, IDX_DIM, IDX_HEADS,
            CONV_DIM, GDN_HEADS, GDN_HEADS, GDN_HEADS * GDN_DV)
IN_COLS = sum(IN_SIZES)
PEER_HEADS = 8
PEER_NKEYS = 128
PEER_QDIM = 256
PEER_TOPK = 16
PEER_BLOCK = 128
NORM_EPS = 1e-6

LANES = 128


def _rmsnorm(x, w):
    xf = x.astype(jnp.float32)
    y = xf * lax.rsqrt(jnp.mean(xf * xf, axis=-1, keepdims=True) + NORM_EPS)
    return (y * w.astype(jnp.float32)).astype(x.dtype)


def _l2norm(x):
    return x * lax.rsqrt(jnp.sum(x * x, axis=-1, keepdims=True) + NORM_EPS)


def _norm_matmul_body(x_ref, g_ref, w_ref, o_ref, xn_ref):
    @pl.when(pl.program_id(1) == 0)
    def _():
        x = x_ref[...]
        r = lax.rsqrt(jnp.mean(x * x, axis=-1, keepdims=True) + NORM_EPS)
        xn_ref[...] = (x * r * g_ref[...]).astype(jnp.bfloat16)

    o_ref[...] = jnp.dot(xn_ref[...], w_ref[...], preferred_element_type=jnp.float32)


def _norm_matmul(x, g, w, tm, tn):
    m, k = x.shape
    n = w.shape[1]
    return pl.pallas_call(
        _norm_matmul_body,
        grid=(m // tm, n // tn),
        in_specs=[pl.BlockSpec((tm, k), lambda i, j: (i, 0)),
                  pl.BlockSpec((1, k), lambda i, j: (0, 0)),
                  pl.BlockSpec((k, tn), lambda i, j: (0, j))],
        out_specs=pl.BlockSpec((tm, tn), lambda i, j: (i, j)),
        out_shape=jax.ShapeDtypeStruct((m, n), jnp.float32),
        scratch_shapes=[pltpu.VMEM((tm, k), jnp.bfloat16)],
        compiler_params=pltpu.CompilerParams(
            dimension_semantics=("arbitrary", "arbitrary"),
            vmem_limit_bytes=48 * 1024 * 1024),
        name="norm_matmul",
    )(x, g.reshape(1, k), w)


def _in_projection(x, attn_norm_w, w_in_p, q_norm_w, k_norm_w, idx_k_norm_w):
    B, T, _ = x.shape
    m = B * T
    tm = 512 if m % 512 == 0 else m
    z = _norm_matmul(x.reshape(m, D_MODEL), attn_norm_w, w_in_p, tm, 1024)
    z = z[:, :IN_COLS].reshape(B, T, IN_COLS)
    offs = []
    acc = 0
    for s in IN_SIZES[:-1]:
        acc += s
        offs.append(acc)
    qa, ka, va, qi, ki, wi, qkv, b_raw, a_raw, gate = jnp.split(z, offs, axis=-1)
    qa = _rmsnorm(qa.reshape(B, T, ATTN_HEADS, HEAD_DIM), q_norm_w)
    ka = _rmsnorm(ka.reshape(B, T, KV_HEADS, HEAD_DIM), k_norm_w)
    va = va.reshape(B, T, KV_HEADS, HEAD_DIM)
    qi = qi.reshape(B, T, IDX_HEADS, IDX_DIM)
    ki = _rmsnorm(ki, idx_k_norm_w)
    return (qa, ka, va, qi, ki, wi), (qkv, b_raw, a_raw, gate)


def _index_scores(qi, wi, ki):
    s = jax.nn.relu(jnp.einsum('bqhd,bsd->bqhs', qi, ki).astype(jnp.float32))
    return jnp.einsum('bqhs,bqh->bqs', s, wi.astype(jnp.float32)) * IDX_SCALE


def _gathered_attention(q, kg, vg, valid):
    B, Q = q.shape[:2]
    qg = q.reshape(B, Q, KV_HEADS, GROUP, HEAD_DIM)
    s = jnp.einsum('bqngd,bqknd->bqngk', qg, kg).astype(jnp.float32) * (HEAD_DIM ** -0.5)
    s = jnp.where(valid[:, :, None, None, :], s, -jnp.inf)
    p = jax.nn.softmax(s, axis=-1)
    o = jnp.einsum('bqngk,bqknd->bqngd', p.astype(vg.dtype), vg)
    return o.reshape(B, Q, ATTN_HEADS * HEAD_DIM)


def _dsa_prompt(q, k, v, qi, ki, wi):
    B, T = q.shape[:2]
    ktop = min(INDEX_TOPK, T // 4)
    nb = T // Q_BLOCK
    bidx = jnp.arange(B)[:, None, None]
    spos = jnp.arange(T)

    def blocks(a):
        return jnp.moveaxis(a.reshape((B, nb, Q_BLOCK) + a.shape[2:]), 1, 0)

    def blk(xs):
        qb, qib, wib, start = xs
        tpos = start + jnp.arange(Q_BLOCK)
        sc = _index_scores(qib, wib, ki)
        sc = jnp.where(spos[None, None, :] <= tpos[None, :, None], sc, -jnp.inf)
        _, idx = lax.top_k(sc, ktop)
        valid = idx <= tpos[None, :, None]
        return _gathered_attention(qb, k[bidx, idx], v[bidx, idx], valid)

    o = lax.map(blk, (blocks(q), blocks(qi), blocks(wi), jnp.arange(nb) * Q_BLOCK))
    return jnp.moveaxis(o, 0, 1).reshape(B, T, ATTN_HEADS * HEAD_DIM)


_INT_MIN = -2 ** 31
_NEG_BIG = -1e30
_NT = (((1,), (1,)), ((), ()))


def _dsa_prompt_body(qi_ref, wT_ref, q_ref, ki_ref, k_ref, vT_ref, o_ref,
                     keys_ref, m_ref, l_ref, acc_ref, *, ktop, tq):
    f32 = jnp.float32
    i = pl.program_id(0)
    nkb = i + 1
    col_t = i * tq + lax.broadcasted_iota(jnp.int32, (tq, tq), 1)
    row_s = lax.broadcasted_iota(jnp.int32, (tq, tq), 0)

    def score_blk(kb, carry):
        kib = ki_ref[kb]
        acc = jnp.zeros((tq, tq), f32)
        for h in range(IDX_HEADS):
            s = lax.dot_general(kib, qi_ref[h], _NT, preferred_element_type=f32)
            acc = acc + jnp.maximum(s, 0.0) * wT_ref[h:h + 1, :]
        sc = acc * IDX_SCALE
        bits = lax.bitcast_convert_type(sc, jnp.int32)
        key = bits ^ (lax.shift_right_arithmetic(bits, 31) & 0x7FFFFFFF)
        valid = (kb * tq + row_s) <= col_t
        keys_ref[kb] = jnp.where(valid, key, _INT_MIN)
        return carry

    lax.fori_loop(0, nkb, score_blk, 0)

    def bisect(it, ans_u):
        cand_u = ans_u | lax.shift_left(jnp.int32(1), 31 - it)
        cand_s = cand_u ^ _INT_MIN

        def count_blk(kb, cnt):
            hit = jnp.where(keys_ref[kb] >= cand_s, 1.0, 0.0)
            return cnt + hit.reshape(tq // 8, 8, tq).sum(axis=0)

        cnt = lax.fori_loop(0, nkb, count_blk, jnp.zeros((8, tq), f32))
        cnt = cnt.sum(axis=0, keepdims=True)
        return jnp.where(cnt >= ktop, cand_u, ans_u)

    ans_u = lax.fori_loop(0, 32, bisect, jnp.zeros((1, tq), jnp.int32))
    thr = jnp.maximum(ans_u ^ _INT_MIN, _INT_MIN + 1)

    m_ref[...] = jnp.full(m_ref.shape, _NEG_BIG, f32)
    l_ref[...] = jnp.zeros(l_ref.shape, f32)
    acc_ref[...] = jnp.zeros(acc_ref.shape, f32)
    scale = HEAD_DIM ** -0.5

    def attn_blk(kb, carry):
        sel = keys_ref[kb] >= thr
        kblk = k_ref[kb]
        vT = vT_ref[kb]
        for h in range(ATTN_HEADS):
            n = h // GROUP
            s = lax.dot_general(kblk[:, n * HEAD_DIM:(n + 1) * HEAD_DIM], q_ref[h], _NT,
                                preferred_element_type=f32) * scale
            s = jnp.where(sel, s, _NEG_BIG)
            m_old = m_ref[h]
            m_new = jnp.maximum(m_old, s.max(axis=0, keepdims=True))
            p = jnp.where(sel, jnp.exp(s - m_new), 0.0)
            alpha = jnp.exp(m_old - m_new)
            l_ref[h] = alpha * l_ref[h] + p.sum(axis=0, keepdims=True)
            pv = jnp.dot(vT[n * HEAD_DIM:(n + 1) * HEAD_DIM, :], p.astype(jnp.bfloat16),
                         preferred_element_type=f32)
            acc_ref[h] = alpha * acc_ref[h] + pv
            m_ref[h] = m_new
        return carry

    lax.fori_loop(0, nkb, attn_blk, 0)
    for h in range(ATTN_HEADS):
        o_ref[:, h * HEAD_DIM:(h + 1) * HEAD_DIM] = (acc_ref[h] / l_ref[h]).T


def _dsa_prompt_pallas(q, k, v, qi, ki, wi, interpret=False):
    T = q.shape[0]
    tq = min(256, T)
    nb = T // tq
    ktop = min(INDEX_TOPK, T // 4)
    bf16 = jnp.bfloat16
    qh = jnp.transpose(q.astype(bf16), (1, 0, 2))
    qih = jnp.transpose(qi.astype(bf16), (1, 0, 2))
    wT = wi.astype(jnp.float32).T
    kib = ki.astype(bf16).reshape(nb, tq, IDX_DIM)
    kb = k.astype(bf16).reshape(nb, tq, KV_HEADS * HEAD_DIM)
    vT = jnp.transpose(v.astype(bf16).reshape(nb, tq, KV_HEADS * HEAD_DIM), (0, 2, 1))
    body = functools.partial(_dsa_prompt_body, ktop=ktop, tq=tq)
    return pl.pallas_call(
        body,
        grid=(nb,),
        in_specs=[pl.BlockSpec((IDX_HEADS, tq, IDX_DIM), lambda i: (0, i, 0)),
                  pl.BlockSpec((IDX_HEADS, tq), lambda i: (0, i)),
                  pl.BlockSpec((ATTN_HEADS, tq, HEAD_DIM), lambda i: (0, i, 0)),
                  pl.BlockSpec((nb, tq, IDX_DIM), lambda i: (0, 0, 0)),
                  pl.BlockSpec((nb, tq, KV_HEADS * HEAD_DIM), lambda i: (0, 0, 0)),
                  pl.BlockSpec((nb, KV_HEADS * HEAD_DIM, tq), lambda i: (0, 0, 0))],
        out_specs=pl.BlockSpec((tq, ATTN_HEADS * HEAD_DIM), lambda i: (i, 0)),
        out_shape=jax.ShapeDtypeStruct((T, ATTN_HEADS * HEAD_DIM), jnp.float32),
        scratch_shapes=[pltpu.VMEM((nb, tq, tq), jnp.int32),
                        pltpu.VMEM((ATTN_HEADS, 1, tq), jnp.float32),
                        pltpu.VMEM((ATTN_HEADS, 1, tq), jnp.float32),
                        pltpu.VMEM((ATTN_HEADS, HEAD_DIM, tq), jnp.float32)],
        compiler_params=pltpu.CompilerParams(
            dimension_semantics=("arbitrary",),
            vmem_limit_bytes=56 * 1024 * 1024),
        name="dsa_prompt",
        interpret=interpret,
    )(qih, wT, qh, kib, kb, vT)


def _dsa_sample(q, k, v, qi, ki, wi, cache_k, cache_v, cache_idx_k, page_table):
    B, T = q.shape[:2]
    past = page_table.shape[1] * PAGE_SIZE
    L = past + T
    ktop = min(INDEX_TOPK, L // 4)
    ki_past = cache_idx_k[page_table].reshape(B, past, IDX_DIM).astype(ki.dtype)
    ki_all = jnp.concatenate([ki_past, ki], axis=1)
    tpos = past + jnp.arange(T)
    sc = _index_scores(qi, wi, ki_all)
    sc = jnp.where(jnp.arange(L)[None, None, :] <= tpos[None, :, None], sc, -jnp.inf)
    _, idx = lax.top_k(sc, ktop)
    valid = idx <= tpos[None, :, None]
    bidx = jnp.arange(B)[:, None, None]
    in_past = (idx < past)[..., None, None]
    pidx = jnp.minimum(idx, past - 1)
    phys = page_table[bidx, pidx // PAGE_SIZE]
    off = pidx % PAGE_SIZE
    nidx = jnp.clip(idx - past, 0, T - 1)
    kg = jnp.where(in_past, cache_k[phys, off].astype(k.dtype), k[bidx, nidx])
    vg = jnp.where(in_past, cache_v[phys, off].astype(v.dtype), v[bidx, nidx])
    return _gathered_attention(q, kg, vg, valid)


def _causal_conv(x, buf, conv_w):
    T = x.shape[1]
    xp = jnp.concatenate([buf.astype(x.dtype), x], axis=1)
    y = xp[:, 0:T] * conv_w[0]
    for j in range(1, CONV_W):
        y = y + xp[:, j:j + T] * conv_w[j]
    return jax.nn.silu(y), xp[:, T:]


def _chunk_gated_delta(q, k, v, g, beta, S0):
    B, T, H, DK = q.shape
    DV = v.shape[-1]
    C = min(GDN_CHUNK, T)
    n = -(-T // C)
    pad = n * C - T

    def prep(a):
        a = jnp.pad(a, [(0, 0), (0, pad)] + [(0, 0)] * (a.ndim - 2))
        a = a.reshape((B, n, C) + a.shape[2:])
        return jnp.swapaxes(jnp.moveaxis(a, 1, 0), 2, 3)

    q, k, v, g, beta = prep(q), prep(k), prep(v), prep(g), prep(beta)
    gc = jnp.cumsum(g, axis=-1)
    causal = jnp.tril(jnp.ones((C, C), bool))
    strict = jnp.tril(jnp.ones((C, C), bool), -1)
    decay = jnp.exp(jnp.where(causal, gc[..., :, None] - gc[..., None, :], -jnp.inf))
    kb = k * beta[..., None]
    lower = jnp.where(strict, jnp.einsum('...id,...jd->...ij', kb, k) * decay, 0.0)
    amat = lower + jnp.eye(C, dtype=jnp.float32)
    rhs = jnp.concatenate([v * beta[..., None], kb * jnp.exp(gc)[..., None]], axis=-1)
    sol = lax.linalg.triangular_solve(amat, rhs, left_side=True, lower=True, unit_diagonal=True)
    u, w = sol[..., :DV], sol[..., DV:]
    qk = jnp.einsum('...id,...jd->...ij', q, k) * decay
    qg = q * jnp.exp(gc)[..., None]
    kd = k * jnp.exp(gc[..., -1:] - gc)[..., None]
    g_last = jnp.exp(gc[..., -1])

    def step(S, xs):
        qg_i, kd_i, u_i, w_i, qk_i, gl_i = xs
        v_new = u_i - jnp.einsum('bhck,bhkv->bhcv', w_i, S)
        o = jnp.einsum('bhck,bhkv->bhcv', qg_i, S) + jnp.einsum('bhcj,bhjv->bhcv', qk_i, v_new)
        S = S * gl_i[..., None, None] + jnp.einsum('bhck,bhcv->bhkv', kd_i, v_new)
        return S, o

    S, o = lax.scan(step, S0, (qg, kd, u, w, qk, g_last))
    o = jnp.moveaxis(jnp.swapaxes(o, 2, 3), 0, 1).reshape(B, n * C, H, DV)[:, :T]
    return o, S


def _gated_deltanet(qkv, b_raw, a_raw, gate, conv_buf, S0, conv_w, a_log, dt_bias, gdn_norm_w):
    B, T, _ = qkv.shape
    f32 = jnp.float32
    conv, new_buf = _causal_conv(qkv, conv_buf, conv_w)
    qc, kc, vc = jnp.split(conv, [GDN_HEADS * GDN_DK, 2 * GDN_HEADS * GDN_DK], axis=-1)
    q = _l2norm(qc.reshape(B, T, GDN_HEADS, GDN_DK).astype(f32)) * (GDN_DK ** -0.5)
    k = _l2norm(kc.reshape(B, T, GDN_HEADS, GDN_DK).astype(f32))
    v = vc.reshape(B, T, GDN_HEADS, GDN_DV).astype(f32)
    beta = jax.nn.sigmoid(b_raw.astype(f32))
    g = -jnp.exp(a_log.astype(f32)) * jax.nn.softplus(a_raw.astype(f32) + dt_bias.astype(f32))
    o, S = _chunk_gated_delta(q, k, v, g, beta, S0)
    o = _rmsnorm(o, gdn_norm_w) * jax.nn.silu(gate.reshape(B, T, GDN_HEADS, GDN_DV).astype(f32))
    return o.reshape(B, T, GDN_HEADS * GDN_DV).astype(qkv.dtype), S, new_buf


def _peer(xn, peer_wq, peer_keys, peer_u, peer_v):
    shape = xn.shape
    xf = xn.reshape(-1, D_MODEL)
    N = xf.shape[0]
    nb = -(-N // PEER_BLOCK)
    xb = jnp.pad(xf, ((0, nb * PEER_BLOCK - N), (0, 0))).reshape(nb, PEER_BLOCK, D_MODEL)
    ncand = PEER_TOPK * PEER_TOPK

    def blk(x):
        qh = (x @ peer_wq).reshape(PEER_BLOCK, PEER_HEADS, 2, PEER_QDIM // 2)
        s1 = jnp.einsum('thd,hkd->thk', qh[:, :, 0], peer_keys[0]).astype(jnp.float32)
        s2 = jnp.einsum('thd,hkd->thk', qh[:, :, 1], peer_keys[1]).astype(jnp.float32)
        v1, i1 = lax.top_k(s1, PEER_TOPK)
        v2, i2 = lax.top_k(s2, PEER_TOPK)
        cand = (v1[..., :, None] + v2[..., None, :]).reshape(PEER_BLOCK, PEER_HEADS, ncand)
        cidx = (i1[..., :, None] * PEER_NKEYS + i2[..., None, :]).reshape(PEER_BLOCK, PEER_HEADS, ncand)
        sv, si = lax.top_k(cand, PEER_TOPK)
        eidx = jnp.take_along_axis(cidx, si, axis=-1)
        gsm = jax.nn.softmax(sv, axis=-1)
        act = jax.nn.gelu(jnp.einsum('thkd,td->thk', peer_u[eidx], x).astype(jnp.float32), approximate=False)
        return jnp.einsum('thk,thkd->td', (gsm * act).astype(x.dtype), peer_v[eidx])

    y = lax.map(blk, xb).reshape(nb * PEER_BLOCK, D_MODEL)[:N]
    return y.reshape(shape)


def _layer_out(x, oa, ob, w_out, ffn_norm_w, peer_wq, peer_keys, peer_u, peer_v):
    h = x + jnp.concatenate([oa, ob], axis=-1) @ w_out
    return h + _peer(_rmsnorm(h, ffn_norm_w), peer_wq, peer_keys, peer_u, peer_v)


def kernel(x_prompt, x_sample, cache_k, cache_v, cache_idx_k, state_ssm, state_conv, page_table,
           attn_norm_w, w_in, q_norm_w, k_norm_w, idx_k_norm_w, conv_w, a_log, dt_bias, gdn_norm_w,
           w_out, ffn_norm_w, peer_wq, peer_keys, peer_u, peer_v):
    l = 0
    n_pad = -(-IN_COLS // 1024) * 1024
    w_in_p = jnp.pad(w_in[l], ((0, 0), (0, n_pad - IN_COLS))).astype(jnp.bfloat16)
    proj_w = (attn_norm_w[l], w_in_p, q_norm_w[l], k_norm_w[l], idx_k_norm_w[l])
    gdn_w = (conv_w[l], a_log[l], dt_bias[l], gdn_norm_w[l])
    out_w = (w_out[l], ffn_norm_w[l], peer_wq[l], peer_keys[l], peer_u[l], peer_v[l])

    hp, hs = x_prompt, x_sample
    (qa, ka, va, qi, ki, wi), gdn_in = _in_projection(hp, *proj_w)
    oa = _dsa_prompt_pallas(qa[0], ka[0], va[0], qi[0], ki[0], wi[0])[None]
    Bp = hp.shape[0]
    buf0 = jnp.zeros((Bp, CONV_W - 1, CONV_DIM), hp.dtype)
    S0 = jnp.zeros((Bp, GDN_HEADS, GDN_DK, GDN_DV), jnp.float32)
    ob, S_p, buf_p = _gated_deltanet(*gdn_in, buf0, S0, *gdn_w)
    hp = _layer_out(hp, oa, ob, *out_w)
    kp, vp, ip = ka, va, ki

    (qa, ka, va, qi, ki, wi), gdn_in = _in_projection(hs, *proj_w)
    oa = _dsa_sample(qa, ka, va, qi, ki, wi, cache_k[l], cache_v[l], cache_idx_k[l], page_table)
    ob, S_s, buf_s = _gated_deltanet(*gdn_in, state_conv[l], state_ssm[l].astype(jnp.float32), *gdn_w)
    hs = _layer_out(hs, oa, ob, *out_w)

    return (hp, hs, kp[None], vp[None], ip[None], S_p[None], buf_p[None],
            ka[None], va[None], ki[None], S_s[None], buf_s[None])
```

```python
import functools
import math

import jax
import jax.numpy as jnp
from jax import lax
from jax.experimental import pallas as pl
from jax.experimental.pallas import tpu as pltpu

D_MODEL = 2048
PAGE_SIZE = 128
HEAD_DIM = 128
ATTN_HEADS = 8
KV_HEADS = 2
GROUP = ATTN_HEADS // KV_HEADS
IDX_HEADS = 16
IDX_DIM = 64
INDEX_TOPK = 256
Q_BLOCK = 128
IDX_SCALE = (IDX_HEADS * IDX_DIM) ** -0.5
GDN_DK = 128
GDN_DV = 128
GDN_HEADS = 8
CONV_W = 4
CONV_DIM = GDN_HEADS * (2 * GDN_DK + GDN_DV)
GDN_CHUNK = 64
IN_SIZES = (ATTN_HEADS * HEAD_DIM, KV_HEADS * HEAD_DIM, KV_HEADS * HEAD_DIM,
            IDX_HEADS * IDX_DIM, IDX_DIM, IDX_HEADS,
            CONV_DIM, GDN_HEADS, GDN_HEADS, GDN_HEADS * GDN_DV)
IN_COLS = sum(IN_SIZES)
PEER_HEADS = 8
PEER_NKEYS = 128
PEER_QDIM = 256
PEER_TOPK = 16
PEER_BLOCK = 128
NORM_EPS = 1e-6

LANES = 128


def _rmsnorm(x, w):
    xf = x.astype(jnp.float32)
    y = xf * lax.rsqrt(jnp.mean(xf * xf, axis=-1, keepdims=True) + NORM_EPS)
    return (y * w.astype(jnp.float32)).astype(x.dtype)


def _l2norm(x):
    return x * lax.rsqrt(jnp.sum(x * x, axis=-1, keepdims=True) + NORM_EPS)


def _norm_matmul_body(x_ref, g_ref, w_ref, o_ref, xn_ref):
    @pl.when(pl.program_id(1) == 0)
    def _():
        x = x_ref[...]
        r = lax.rsqrt(jnp.mean(x * x, axis=-1, keepdims=True) + NORM_EPS)
        xn_ref[...] = (x * r * g_ref[...]).astype(jnp.bfloat16)

    o_ref[...] = jnp.dot(xn_ref[...], w_ref[...], preferred_element_type=jnp.float32)


def _norm_matmul(x, g, w, tm, tn):
    m, k = x.shape
    n = w.shape[1]
    return pl.pallas_call(
        _norm_matmul_body,
        grid=(m // tm, n // tn),
        in_specs=[pl.BlockSpec((tm, k), lambda i, j: (i, 0)),
                  pl.BlockSpec((1, k), lambda i, j: (0, 0)),
                  pl.BlockSpec((k, tn), lambda i, j: (0, j))],
        out_specs=pl.BlockSpec((tm, tn), lambda i, j: (i, j)),
        out_shape=jax.ShapeDtypeStruct((m, n), jnp.float32),
        scratch_shapes=[pltpu.VMEM((tm, k), jnp.bfloat16)],
        compiler_params=pltpu.CompilerParams(
            dimension_semantics=("arbitrary", "arbitrary"),
            vmem_limit_bytes=48 * 1024 * 1024),
        name="norm_matmul",
    )(x, g.reshape(1, k), w)


def _in_projection(x, attn_norm_w, w_in_p, q_norm_w, k_norm_w, idx_k_norm_w):
    B, T, _ = x.shape
    m = B * T
    tm = 512 if m % 512 == 0 else m
    z = _norm_matmul(x.reshape(m, D_MODEL), attn_norm_w, w_in_p, tm, 1024)
    z = z[:, :IN_COLS].reshape(B, T, IN_COLS)
    offs = []
    acc = 0
    for s in IN_SIZES[:-1]:
        acc += s
        offs.append(acc)
    qa, ka, va, qi, ki, wi, qkv, b_raw, a_raw, gate = jnp.split(z, offs, axis=-1)
    qa = _rmsnorm(qa.reshape(B, T, ATTN_HEADS, HEAD_DIM), q_norm_w)
    ka = _rmsnorm(ka.reshape(B, T, KV_HEADS, HEAD_DIM), k_norm_w)
    va = va.reshape(B, T, KV_HEADS, HEAD_DIM)
    qi = qi.reshape(B, T, IDX_HEADS, IDX_DIM)
    ki = _rmsnorm(ki, idx_k_norm_w)
    return (qa, ka, va, qi, ki, wi), (qkv, b_raw, a_raw, gate)


def _index_scores(qi, wi, ki):
    s = jax.nn.relu(jnp.einsum('bqhd,bsd->bqhs', qi, ki).astype(jnp.float32))
    return jnp.einsum('bqhs,bqh->bqs', s, wi.astype(jnp.float32)) * IDX_SCALE


def _gathered_attention(q, kg, vg, valid):
    B, Q = q.shape[:2]
    qg = q.reshape(B, Q, KV_HEADS, GROUP, HEAD_DIM)
    s = jnp.einsum('bqngd,bqknd->bqngk', qg, kg).astype(jnp.float32) * (HEAD_DIM ** -0.5)
    s = jnp.where(valid[:, :, None, None, :], s, -jnp.inf)
    p = jax.nn.softmax(s, axis=-1)
    o = jnp.einsum('bqngk,bqknd->bqngd', p.astype(vg.dtype), vg)
    return o.reshape(B, Q, ATTN_HEADS * HEAD_DIM)


def _dsa_prompt(q, k, v, qi, ki, wi):
    B, T = q.shape[:2]
    ktop = min(INDEX_TOPK, T // 4)
    nb = T // Q_BLOCK
    bidx = jnp.arange(B)[:, None, None]
    spos = jnp.arange(T)

    def blocks(a):
        return jnp.moveaxis(a.reshape((B, nb, Q_BLOCK) + a.shape[2:]), 1, 0)

    def blk(xs):
        qb, qib, wib, start = xs
        tpos = start + jnp.arange(Q_BLOCK)
        sc = _index_scores(qib, wib, ki)
        sc = jnp.where(spos[None, None, :] <= tpos[None, :, None], sc, -jnp.inf)
        _, idx = lax.top_k(sc, ktop)
        valid = idx <= tpos[None, :, None]
        return _gathered_attention(qb, k[bidx, idx], v[bidx, idx], valid)

    o = lax.map(blk, (blocks(q), blocks(qi), blocks(wi), jnp.arange(nb) * Q_BLOCK))
    return jnp.moveaxis(o, 0, 1).reshape(B, T, ATTN_HEADS * HEAD_DIM)


_INT_MIN = -2 ** 31
_NEG_BIG = -1e30
_NT = (((1,), (1,)), ((), ()))


def _dsa_prompt_body(qi_ref, wT_ref, q_ref, ki_ref, k_ref, vT_ref, o_ref,
                     keys_ref, m_ref, l_ref, acc_ref, *, ktop, tq):
    f32 = jnp.float32
    i = pl.program_id(0)
    nkb = i + 1
    col_t = i * tq + lax.broadcasted_iota(jnp.int32, (tq, tq), 1)
    row_s = lax.broadcasted_iota(jnp.int32, (tq, tq), 0)

    def score_blk(kb, carry):
        kib = ki_ref[kb]
        acc = jnp.zeros((tq, tq), f32)
        for h in range(IDX_HEADS):
            s = lax.dot_general(kib, qi_ref[h], _NT, preferred_element_type=f32)
            acc = acc + jnp.maximum(s, 0.0) * wT_ref[h:h + 1, :]
        sc = acc * IDX_SCALE
        bits = lax.bitcast_convert_type(sc, jnp.int32)
        key = bits ^ (lax.shift_right_arithmetic(bits, 31) & 0x7FFFFFFF)
        valid = (kb * tq + row_s) <= col_t
        keys_ref[kb] = jnp.where(valid, key, _INT_MIN)
        return carry

    lax.fori_loop(0, nkb, score_blk, 0)

    def bisect(it, ans_u):
        cand_u = ans_u | lax.shift_left(jnp.int32(1), 31 - it)
        cand_s = cand_u ^ _INT_MIN

        def count_blk(kb, cnt):
            hit = jnp.where(keys_ref[kb] >= cand_s, 1.0, 0.0)
            return cnt + hit.reshape(tq // 8, 8, tq).sum(axis=0)

        cnt = lax.fori_loop(0, nkb, count_blk, jnp.zeros((8, tq), f32))
        cnt = cnt.sum(axis=0, keepdims=True)
        return jnp.where(cnt >= ktop, cand_u, ans_u)

    ans_u = lax.fori_loop(0, 32, bisect, jnp.zeros((1, tq), jnp.int32))
    thr = jnp.maximum(ans_u ^ _INT_MIN, _INT_MIN + 1)

    m_ref[...] = jnp.full(m_ref.shape, _NEG_BIG, f32)
    l_ref[...] = jnp.zeros(l_ref.shape, f32)
    acc_ref[...] = jnp.zeros(acc_ref.shape, f32)
    scale = HEAD_DIM ** -0.5

    def attn_blk(kb, carry):
        sel = keys_ref[kb] >= thr
        kblk = k_ref[kb]
        vT = vT_ref[kb]
        for h in range(ATTN_HEADS):
            n = h // GROUP
            s = lax.dot_general(kblk[:, n * HEAD_DIM:(n + 1) * HEAD_DIM], q_ref[h], _NT,
                                preferred_element_type=f32) * scale
            s = jnp.where(sel, s, _NEG_BIG)
            m_old = m_ref[h]
            m_new = jnp.maximum(m_old, s.max(axis=0, keepdims=True))
            p = jnp.where(sel, jnp.exp(s - m_new), 0.0)
            alpha = jnp.exp(m_old - m_new)
            l_ref[h] = alpha * l_ref[h] + p.sum(axis=0, keepdims=True)
            pv = jnp.dot(vT[n * HEAD_DIM:(n + 1) * HEAD_DIM, :], p.astype(jnp.bfloat16),
                         preferred_element_type=f32)
            acc_ref[h] = alpha * acc_ref[h] + pv
            m_ref[h] = m_new
        return carry

    lax.fori_loop(0, nkb, attn_blk, 0)
    for h in range(ATTN_HEADS):
        o_ref[:, h * HEAD_DIM:(h + 1) * HEAD_DIM] = (acc_ref[h] / l_ref[h]).T


def _dsa_prompt_pallas(q, k, v, qi, ki, wi, interpret=False):
    T = q.shape[0]
    tq = min(256, T)
    nb = T // tq
    ktop = min(INDEX_TOPK, T // 4)
    bf16 = jnp.bfloat16
    qh = jnp.transpose(q.astype(bf16), (1, 0, 2))
    qih = jnp.transpose(qi.astype(bf16), (1, 0, 2))
    wT = wi.astype(jnp.float32).T
    kib = ki.astype(bf16).reshape(nb, tq, IDX_DIM)
    kb = k.astype(bf16).reshape(nb, tq, KV_HEADS * HEAD_DIM)
    vT = jnp.transpose(v.astype(bf16).reshape(nb, tq, KV_HEADS * HEAD_DIM), (0, 2, 1))
    body = functools.partial(_dsa_prompt_body, ktop=ktop, tq=tq)
    return pl.pallas_call(
        body,
        grid=(nb,),
        in_specs=[pl.BlockSpec((IDX_HEADS, tq, IDX_DIM), lambda i: (0, i, 0)),
                  pl.BlockSpec((IDX_HEADS, tq), lambda i: (0, i)),
                  pl.BlockSpec((ATTN_HEADS, tq, HEAD_DIM), lambda i: (0, i, 0)),
                  pl.BlockSpec((nb, tq, IDX_DIM), lambda i: (0, 0, 0)),
                  pl.BlockSpec((nb, tq, KV_HEADS * HEAD_DIM), lambda i: (0, 0, 0)),
                  pl.BlockSpec((nb, KV_HEADS * HEAD_DIM, tq), lambda i: (0, 0, 0))],
        out_specs=pl.BlockSpec((tq, ATTN_HEADS * HEAD_DIM), lambda i: (i, 0)),
        out_shape=jax.ShapeDtypeStruct((T, ATTN_HEADS * HEAD_DIM), jnp.float32),
        scratch_shapes=[pltpu.VMEM((nb, tq, tq), jnp.int32),
                        pltpu.VMEM((ATTN_HEADS, 1, tq), jnp.float32),
                        pltpu.VMEM((ATTN_HEADS, 1, tq), jnp.float32),
                        pltpu.VMEM((ATTN_HEADS, HEAD_DIM, tq), jnp.float32)],
        compiler_params=pltpu.CompilerParams(
            dimension_semantics=("arbitrary",),
            vmem_limit_bytes=56 * 1024 * 1024),
        name="dsa_prompt",
        interpret=interpret,
    )(qih, wT, qh, kib, kb, vT)


def _dsa_sample(q, k, v, qi, ki, wi, cache_k, cache_v, cache_idx_k, page_table):
    B, T = q.shape[:2]
    past = page_table.shape[1] * PAGE_SIZE
    L = past + T
    ktop = min(INDEX_TOPK, L // 4)
    ki_past = cache_idx_k[page_table].reshape(B, past, IDX_DIM).astype(ki.dtype)
    ki_all = jnp.concatenate([ki_past, ki], axis=1)
    tpos = past + jnp.arange(T)
    sc = _index_scores(qi, wi, ki_all)
    sc = jnp.where(jnp.arange(L)[None, None, :] <= tpos[None, :, None], sc, -jnp.inf)
    _, idx = lax.top_k(sc, ktop)
    valid = idx <= tpos[None, :, None]
    bidx = jnp.arange(B)[:, None, None]
    in_past = (idx < past)[..., None, None]
    pidx = jnp.minimum(idx, past - 1)
    phys = page_table[bidx, pidx // PAGE_SIZE]
    off = pidx % PAGE_SIZE
    nidx = jnp.clip(idx - past, 0, T - 1)
    kg = jnp.where(in_past, cache_k[phys, off].astype(k.dtype), k[bidx, nidx])
    vg = jnp.where(in_past, cache_v[phys, off].astype(v.dtype), v[bidx, nidx])
    return _gathered_attention(q, kg, vg, valid)


def _causal_conv(x, buf, conv_w):
    T = x.shape[1]
    xp = jnp.concatenate([buf.astype(x.dtype), x], axis=1)
    y = xp[:, 0:T] * conv_w[0]
    for j in range(1, CONV_W):
        y = y + xp[:, j:j + T] * conv_w[j]
    return jax.nn.silu(y), xp[:, T:]


def _chunk_gated_delta(q, k, v, g, beta, S0):
    B, T, H, DK = q.shape
    DV = v.shape[-1]
    C = min(GDN_CHUNK, T)
    n = -(-T // C)
    pad = n * C - T

    def prep(a):
        a = jnp.pad(a, [(0, 0), (0, pad)] + [(0, 0)] * (a.ndim - 2))
        a = a.reshape((B, n, C) + a.shape[2:])
        return jnp.swapaxes(jnp.moveaxis(a, 1, 0), 2, 3)

    q, k, v, g, beta = prep(q), prep(k), prep(v), prep(g), prep(beta)
    gc = jnp.cumsum(g, axis=-1)
    causal = jnp.tril(jnp.ones((C, C), bool))
    strict = jnp.tril(jnp.ones((C, C), bool), -1)
    decay = jnp.exp(jnp.where(causal, gc[..., :, None] - gc[..., None, :], -jnp.inf))
    kb = k * beta[..., None]
    lower = jnp.where(strict, jnp.einsum('...id,...jd->...ij', kb, k) * decay, 0.0)
    amat = lower + jnp.eye(C, dtype=jnp.float32)
    rhs = jnp.concatenate([v * beta[..., None], kb * jnp.exp(gc)[..., None]], axis=-1)
    sol = lax.linalg.triangular_solve(amat, rhs, left_side=True, lower=True, unit_diagonal=True)
    u, w = sol[..., :DV], sol[..., DV:]
    qk = jnp.einsum('...id,...jd->...ij', q, k) * decay
    qg = q * jnp.exp(gc)[..., None]
    kd = k * jnp.exp(gc[..., -1:] - gc)[..., None]
    g_last = jnp.exp(gc[..., -1])

    def step(S, xs):
        qg_i, kd_i, u_i, w_i, qk_i, gl_i = xs
        v_new = u_i - jnp.einsum('bhck,bhkv->bhcv', w_i, S)
        o = jnp.einsum('bhck,bhkv->bhcv', qg_i, S) + jnp.einsum('bhcj,bhjv->bhcv', qk_i, v_new)
        S = S * gl_i[..., None, None] + jnp.einsum('bhck,bhcv->bhkv', kd_i, v_new)
        return S, o

    S, o = lax.scan(step, S0, (qg, kd, u, w, qk, g_last))
    o = jnp.moveaxis(jnp.swapaxes(o, 2, 3), 0, 1).reshape(B, n * C, H, DV)[:, :T]
    return o, S


def _gated_deltanet(qkv, b_raw, a_raw, gate, conv_buf, S0, conv_w, a_log, dt_bias, gdn_norm_w):
    B, T, _ = qkv.shape
    f32 = jnp.float32
    conv, new_buf = _causal_conv(qkv, conv_buf, conv_w)
    qc, kc, vc = jnp.split(conv, [GDN_HEADS * GDN_DK, 2 * GDN_HEADS * GDN_DK], axis=-1)
    q = _l2norm(qc.reshape(B, T, GDN_HEADS, GDN_DK).astype(f32)) * (GDN_DK ** -0.5)
    k = _l2norm(kc.reshape(B, T, GDN_HEADS, GDN_DK).astype(f32))
    v = vc.reshape(B, T, GDN_HEADS, GDN_DV).astype(f32)
    beta = jax.nn.sigmoid(b_raw.astype(f32))
    g = -jnp.exp(a_log.astype(f32)) * jax.nn.softplus(a_raw.astype(f32) + dt_bias.astype(f32))
    o, S = _chunk_gated_delta(q, k, v, g, beta, S0)
    o = _rmsnorm(o, gdn_norm_w) * jax.nn.silu(gate.reshape(B, T, GDN_HEADS, GDN_DV).astype(f32))
    return o.reshape(B, T, GDN_HEADS * GDN_DV).astype(qkv.dtype), S, new_buf


def _peer(xn, peer_wq, peer_keys, peer_u, peer_v):
    shape = xn.shape
    xf = xn.reshape(-1, D_MODEL)
    N = xf.shape[0]
    nb = -(-N // PEER_BLOCK)
    xb = jnp.pad(xf, ((0, nb * PEER_BLOCK - N), (0, 0))).reshape(nb, PEER_BLOCK, D_MODEL)
    ncand = PEER_TOPK * PEER_TOPK

    def blk(x):
        qh = (x @ peer_wq).reshape(PEER_BLOCK, PEER_HEADS, 2, PEER_QDIM // 2)
        s1 = jnp.einsum('thd,hkd->thk', qh[:, :, 0], peer_keys[0]).astype(jnp.float32)
        s2 = jnp.einsum('thd,hkd->thk', qh[:, :, 1], peer_keys[1]).astype(jnp.float32)
        v1, i1 = lax.top_k(s1, PEER_TOPK)
        v2, i2 = lax.top_k(s2, PEER_TOPK)
        cand = (v1[..., :, None] + v2[..., None, :]).reshape(PEER_BLOCK, PEER_HEADS, ncand)
        cidx = (i1[..., :, None] * PEER_NKEYS + i2[..., None, :]).reshape(PEER_BLOCK, PEER_HEADS, ncand)
        sv, si = lax.top_k(cand, PEER_TOPK)
        eidx = jnp.take_along_axis(cidx, si, axis=-1)
        gsm = jax.nn.softmax(sv, axis=-1)
        act = jax.nn.gelu(jnp.einsum('thkd,td->thk', peer_u[eidx], x).astype(jnp.float32), approximate=False)
        return jnp.einsum('thk,thkd->td', (gsm * act).astype(x.dtype), peer_v[eidx])

    y = lax.map(blk, xb).reshape(nb * PEER_BLOCK, D_MODEL)[:N]
    return y.reshape(shape)


_SQRT_HALF = 0.7071067811865476


def _top_rows(x, k):
    R, n = x.shape
    ri = lax.broadcasted_iota(jnp.int32, (R, n), 0).astype(jnp.float32)
    ki = lax.broadcasted_iota(jnp.int32, (k, n), 0)

    def body(r, c):
        x, out = c
        m = x.max(axis=0, keepdims=True)
        first = jnp.min(jnp.where(x == m, ri, float(R)), axis=0, keepdims=True)
        x = jnp.where(ri == first, -jnp.inf, x)
        out = jnp.where(ki == r, m, out)
        return x, out

    _, out = lax.fori_loop(0, k, body, (x, jnp.zeros((k, n), jnp.float32)))
    return out


def _peer_front_body(x_ref, oa_ref, ob_ref, wo_ref, g_ref, wq_ref, keys_ref,
                     h_ref, xn_ref, s1_ref, s2_ref, e2_ref, aux_ref):
    f32, bf16 = jnp.float32, jnp.bfloat16
    half_w = ATTN_HEADS * HEAD_DIM
    h = (x_ref[...]
         + jnp.dot(oa_ref[...].astype(bf16), wo_ref[:half_w, :], preferred_element_type=f32)
         + jnp.dot(ob_ref[...].astype(bf16), wo_ref[half_w:, :], preferred_element_type=f32))
    h_ref[...] = h
    xn = (h * lax.rsqrt(jnp.mean(h * h, axis=-1, keepdims=True) + NORM_EPS) * g_ref[...]).astype(bf16)
    xn_ref[...] = xn
    qh = jnp.dot(xn, wq_ref[...], preferred_element_type=f32).astype(bf16)
    tq = qh.shape[0]
    hq = PEER_QDIM // 2
    for hh in range(PEER_HEADS):
        tops = []
        for half in range(2):
            col = (hh * 2 + half) * hq
            sT = lax.dot_general(keys_ref[half, hh], qh[:, col:col + hq], _NT,
                                 preferred_element_type=f32)
            (s1_ref if half == 0 else s2_ref)[hh] = sT
            tops.append(_top_rows(sT, PEER_TOPK))
        a16, b16 = tops
        cand = jnp.concatenate([a16[r:r + 1, :] + b16 for r in range(PEER_TOPK)], axis=0)
        tau = _top_rows(cand, PEER_TOPK)[PEER_TOPK - 1:PEER_TOPK, :]
        top_sum = a16[0:1, :] + b16[0:1, :]
        z = jnp.sum(jnp.where(cand >= tau, jnp.exp(cand - top_sum), 0.0), axis=0, keepdims=True)
        e2_ref[hh] = jnp.exp(s2_ref[hh] - b16[0:1, :]) / z
        aux_ref[hh] = jnp.concatenate([tau, a16[0:1, :], jnp.zeros((6, tq), f32)], axis=0)


def _peer_front(x, oa, ob, wo, g, wq, keys, tq, interpret=False):
    n = x.shape[0]
    half_w = ATTN_HEADS * HEAD_DIM
    tok = lambda i: (i, 0)
    fix2 = lambda i: (0, 0)
    colT = lambda i: (0, 0, i)
    f32 = jnp.float32
    plane = jax.ShapeDtypeStruct((PEER_HEADS, PEER_NKEYS, n), f32)
    plane_spec = pl.BlockSpec((PEER_HEADS, PEER_NKEYS, tq), colT)
    return pl.pallas_call(
        _peer_front_body,
        grid=(n // tq,),
        in_specs=[pl.BlockSpec((tq, D_MODEL), tok),
                  pl.BlockSpec((tq, half_w), tok),
                  pl.BlockSpec((tq, half_w), tok),
                  pl.BlockSpec((D_MODEL, D_MODEL), fix2),
                  pl.BlockSpec((1, D_MODEL), fix2),
                  pl.BlockSpec((D_MODEL, PEER_HEADS * PEER_QDIM), fix2),
                  pl.BlockSpec((2, PEER_HEADS, PEER_NKEYS, PEER_QDIM // 2), lambda i: (0, 0, 0, 0))],
        out_specs=[pl.BlockSpec((tq, D_MODEL), tok),
                   pl.BlockSpec((tq, D_MODEL), tok),
                   plane_spec, plane_spec, plane_spec,
                   pl.BlockSpec((PEER_HEADS, 8, tq), colT)],
        out_shape=[jax.ShapeDtypeStruct((n, D_MODEL), f32),
                   jax.ShapeDtypeStruct((n, D_MODEL), jnp.bfloat16),
                   plane, plane, plane,
                   jax.ShapeDtypeStruct((PEER_HEADS, 8, n), f32)],
        compiler_params=pltpu.CompilerParams(
            dimension_semantics=("arbitrary",),
            vmem_limit_bytes=56 * 1024 * 1024),
        name="peer_front",
        interpret=interpret,
    )(x, oa, ob, wo, g.reshape(1, D_MODEL), wq, keys)


def _peer_dense_body(xn_ref, u_ref, vT_ref, s1_ref, s2_ref, e2_ref, aux_ref, yT_ref, *, eblk):
    f32 = jnp.float32
    eb = pl.program_id(1)

    @pl.when(eb == 0)
    def _():
        yT_ref[...] = jnp.zeros(yT_ref.shape, f32)

    a = lax.dot_general(u_ref[...], xn_ref[...], _NT, preferred_element_type=f32)
    act = 0.5 * a * (1.0 + lax.erf(a * _SQRT_HALF))
    sub = eblk // PEER_NKEYS
    pieces = []
    for r in range(sub):
        i1 = eb * sub + r
        gate = jnp.zeros((PEER_NKEYS, a.shape[1]), f32)
        for hh in range(PEER_HEADS):
            s1row = s1_ref[hh, pl.ds(i1, 1), :]
            tau = aux_ref[hh, 0:1, :]
            e1row = jnp.exp(s1row - aux_ref[hh, 1:2, :])
            gate = gate + jnp.where(s1row + s2_ref[hh] >= tau, e1row * e2_ref[hh], 0.0)
        pieces.append((gate * act[r * PEER_NKEYS:(r + 1) * PEER_NKEYS, :]).astype(jnp.bfloat16))
    hT = jnp.concatenate(pieces, axis=0)
    yT_ref[...] += jnp.dot(vT_ref[...], hT, preferred_element_type=f32)


def _peer_dense(xn, u, vT, s1, s2, e2, aux, tq, eblk, interpret=False):
    n = xn.shape[0]
    ne = u.shape[0]
    plane_spec = pl.BlockSpec((PEER_HEADS, PEER_NKEYS, tq), lambda i, e: (0, 0, i))
    return pl.pallas_call(
        functools.partial(_peer_dense_body, eblk=eblk),
        grid=(n // tq, ne // eblk),
        in_specs=[pl.BlockSpec((tq, D_MODEL), lambda i, e: (i, 0)),
                  pl.BlockSpec((eblk, D_MODEL), lambda i, e: (e, 0)),
                  pl.BlockSpec((D_MODEL, eblk), lambda i, e: (0, e)),
                  plane_spec, plane_spec, plane_spec,
                  pl.BlockSpec((PEER_HEADS, 8, tq), lambda i, e: (0, 0, i))],
        out_specs=pl.BlockSpec((D_MODEL, tq), lambda i, e: (0, i)),
        out_shape=jax.ShapeDtypeStruct((D_MODEL, n), jnp.float32),
        compiler_params=pltpu.CompilerParams(
            dimension_semantics=("arbitrary", "arbitrary"),
            vmem_limit_bytes=56 * 1024 * 1024),
        name="peer_dense",
        interpret=interpret,
    )(xn, u, vT, s1, s2, e2, aux)


def _layer_out_pallas(x, oa, ob, wo_b, ffn_norm_w, wq_b, keys_b, u_b, vT_b, interpret=False):
    n = x.shape[0]
    tq1 = min(256, n)
    tq2 = 512 if n % 512 == 0 else min(256, n)
    h, xn, s1, s2, e2, aux = _peer_front(x, oa, ob, wo_b, ffn_norm_w, wq_b, keys_b, tq1, interpret)
    yT = _peer_dense(xn, u_b, vT_b, s1, s2, e2, aux, tq2, 512, interpret)
    return h + yT.T


def _layer_out(x, oa, ob, w_out, ffn_norm_w, peer_wq, peer_keys, peer_u, peer_v):
    h = x + jnp.concatenate([oa, ob], axis=-1) @ w_out
    return h + _peer(_rmsnorm(h, ffn_norm_w), peer_wq, peer_keys, peer_u, peer_v)


def kernel(x_prompt, x_sample, cache_k, cache_v, cache_idx_k, state_ssm, state_conv, page_table,
           attn_norm_w, w_in, q_norm_w, k_norm_w, idx_k_norm_w, conv_w, a_log, dt_bias, gdn_norm_w,
           w_out, ffn_norm_w, peer_wq, peer_keys, peer_u, peer_v):
    l = 0
    n_pad = -(-IN_COLS // 1024) * 1024
    w_in_p = jnp.pad(w_in[l], ((0, 0), (0, n_pad - IN_COLS))).astype(jnp.bfloat16)
    proj_w = (attn_norm_w[l], w_in_p, q_norm_w[l], k_norm_w[l], idx_k_norm_w[l])
    gdn_w = (conv_w[l], a_log[l], dt_bias[l], gdn_norm_w[l])
    bf16 = jnp.bfloat16
    out_w = (w_out[l].astype(bf16), ffn_norm_w[l], peer_wq[l].astype(bf16), peer_keys[l].astype(bf16),
             peer_u[l].astype(bf16), peer_v[l].astype(bf16).T)

    hp, hs = x_prompt, x_sample
    (qa, ka, va, qi, ki, wi), gdn_in = _in_projection(hp, *proj_w)
    oa = _dsa_prompt_pallas(qa[0], ka[0], va[0], qi[0], ki[0], wi[0])[None]
    Bp = hp.shape[0]
    buf0 = jnp.zeros((Bp, CONV_W - 1, CONV_DIM), hp.dtype)
    S0 = jnp.zeros((Bp, GDN_HEADS, GDN_DK, GDN_DV), jnp.float32)
    ob, S_p, buf_p = _gated_deltanet(*gdn_in, buf0, S0, *gdn_w)
    half_w = ATTN_HEADS * HEAD_DIM
    hp = _layer_out_pallas(hp[0], oa[0], ob[0], *out_w)[None]
    kp, vp, ip = ka, va, ki

    (qa, ka, va, qi, ki, wi), gdn_in = _in_projection(hs, *proj_w)
    oa = _dsa_sample(qa, ka, va, qi, ki, wi, cache_k[l], cache_v[l], cache_idx_k[l], page_table)
    ob, S_s, buf_s = _gated_deltanet(*gdn_in, state_conv[l], state_ssm[l].astype(jnp.float32), *gdn_w)
    ns = hs.shape[0] * hs.shape[1]
    hs = _layer_out_pallas(hs.reshape(ns, D_MODEL), oa.reshape(ns, half_w), ob.reshape(ns, half_w),
                           *out_w).reshape(hs.shape)

    return (hp, hs, kp[None], vp[None], ip[None], S_p[None], buf_p[None],
            ka[None], va[None], ki[None], S_s[None], buf_s[None])
```

```python
import functools
import math

import jax
import jax.numpy as jnp
from jax import lax
from jax.experimental import pallas as pl
from jax.experimental.pallas import tpu as pltpu

D_MODEL = 2048
PAGE_SIZE = 128
HEAD_DIM = 128
ATTN_HEADS = 8
KV_HEADS = 2
GROUP = ATTN_HEADS // KV_HEADS
IDX_HEADS = 16
IDX_DIM = 64
INDEX_TOPK = 256
Q_BLOCK = 128
IDX_SCALE = (IDX_HEADS * IDX_DIM) ** -0.5
GDN_DK = 128
GDN_DV = 128
GDN_HEADS = 8
CONV_W = 4
CONV_DIM = GDN_HEADS * (2 * GDN_DK + GDN_DV)
GDN_CHUNK = 64
IN_SIZES = (ATTN_HEADS * HEAD_DIM, KV_HEADS * HEAD_DIM, KV_HEADS * HEAD_DIM,
            IDX_HEADS * IDX_DIM, IDX_DIM, IDX_HEADS,
            CONV_DIM, GDN_HEADS, GDN_HEADS, GDN_HEADS * GDN_DV)
IN_COLS = sum(IN_SIZES)
PEER_HEADS = 8
PEER_NKEYS = 128
PEER_QDIM = 256
PEER_TOPK = 16
PEER_BLOCK = 128
NORM_EPS = 1e-6

LANES = 128


def _rmsnorm(x, w):
    xf = x.astype(jnp.float32)
    y = xf * lax.rsqrt(jnp.mean(xf * xf, axis=-1, keepdims=True) + NORM_EPS)
    return (y * w.astype(jnp.float32)).astype(x.dtype)


def _l2norm(x):
    return x * lax.rsqrt(jnp.sum(x * x, axis=-1, keepdims=True) + NORM_EPS)


def _norm_matmul_body(x_ref, g_ref, w_ref, o_ref, xn_ref):
    @pl.when(pl.program_id(1) == 0)
    def _():
        x = x_ref[...]
        r = lax.rsqrt(jnp.mean(x * x, axis=-1, keepdims=True) + NORM_EPS)
        xn_ref[...] = (x * r * g_ref[...]).astype(jnp.bfloat16)

    o_ref[...] = jnp.dot(xn_ref[...], w_ref[...], preferred_element_type=jnp.float32)


def _norm_matmul(x, g, w, tm, tn):
    m, k = x.shape
    n = w.shape[1]
    return pl.pallas_call(
        _norm_matmul_body,
        grid=(m // tm, n // tn),
        in_specs=[pl.BlockSpec((tm, k), lambda i, j: (i, 0)),
                  pl.BlockSpec((1, k), lambda i, j: (0, 0)),
                  pl.BlockSpec((k, tn), lambda i, j: (0, j))],
        out_specs=pl.BlockSpec((tm, tn), lambda i, j: (i, j)),
        out_shape=jax.ShapeDtypeStruct((m, n), jnp.float32),
        scratch_shapes=[pltpu.VMEM((tm, k), jnp.bfloat16)],
        compiler_params=pltpu.CompilerParams(
            dimension_semantics=("arbitrary", "arbitrary"),
            vmem_limit_bytes=48 * 1024 * 1024),
        name="norm_matmul",
    )(x, g.reshape(1, k), w)


def _in_projection(x, attn_norm_w, w_in_p, q_norm_w, k_norm_w, idx_k_norm_w):
    B, T, _ = x.shape
    m = B * T
    tm = 512 if m % 512 == 0 else m
    z = _norm_matmul(x.reshape(m, D_MODEL), attn_norm_w, w_in_p, tm, 1024)
    z = z[:, :IN_COLS].reshape(B, T, IN_COLS)
    offs = []
    acc = 0
    for s in IN_SIZES[:-1]:
        acc += s
        offs.append(acc)
    qa, ka, va, qi, ki, wi, qkv, b_raw, a_raw, gate = jnp.split(z, offs, axis=-1)
    qa = _rmsnorm(qa.reshape(B, T, ATTN_HEADS, HEAD_DIM), q_norm_w)
    ka = _rmsnorm(ka.reshape(B, T, KV_HEADS, HEAD_DIM), k_norm_w)
    va = va.reshape(B, T, KV_HEADS, HEAD_DIM)
    qi = qi.reshape(B, T, IDX_HEADS, IDX_DIM)
    ki = _rmsnorm(ki, idx_k_norm_w)
    return (qa, ka, va, qi, ki, wi), (qkv, b_raw, a_raw, gate)


def _index_scores(qi, wi, ki):
    s = jax.nn.relu(jnp.einsum('bqhd,bsd->bqhs', qi, ki).astype(jnp.float32))
    return jnp.einsum('bqhs,bqh->bqs', s, wi.astype(jnp.float32)) * IDX_SCALE


def _gathered_attention(q, kg, vg, valid):
    B, Q = q.shape[:2]
    qg = q.reshape(B, Q, KV_HEADS, GROUP, HEAD_DIM)
    s = jnp.einsum('bqngd,bqknd->bqngk', qg, kg).astype(jnp.float32) * (HEAD_DIM ** -0.5)
    s = jnp.where(valid[:, :, None, None, :], s, -jnp.inf)
    p = jax.nn.softmax(s, axis=-1)
    o = jnp.einsum('bqngk,bqknd->bqngd', p.astype(vg.dtype), vg)
    return o.reshape(B, Q, ATTN_HEADS * HEAD_DIM)


def _dsa_prompt(q, k, v, qi, ki, wi):
    B, T = q.shape[:2]
    ktop = min(INDEX_TOPK, T // 4)
    nb = T // Q_BLOCK
    bidx = jnp.arange(B)[:, None, None]
    spos = jnp.arange(T)

    def blocks(a):
        return jnp.moveaxis(a.reshape((B, nb, Q_BLOCK) + a.shape[2:]), 1, 0)

    def blk(xs):
        qb, qib, wib, start = xs
        tpos = start + jnp.arange(Q_BLOCK)
        sc = _index_scores(qib, wib, ki)
        sc = jnp.where(spos[None, None, :] <= tpos[None, :, None], sc, -jnp.inf)
        _, idx = lax.top_k(sc, ktop)
        valid = idx <= tpos[None, :, None]
        return _gathered_attention(qb, k[bidx, idx], v[bidx, idx], valid)

    o = lax.map(blk, (blocks(q), blocks(qi), blocks(wi), jnp.arange(nb) * Q_BLOCK))
    return jnp.moveaxis(o, 0, 1).reshape(B, T, ATTN_HEADS * HEAD_DIM)


_INT_MIN = -2 ** 31
_NEG_BIG = -1e30
_NT = (((1,), (1,)), ((), ()))


def _dsa_prompt_body(qi_ref, wT_ref, q_ref, ki_ref, k_ref, vT_ref, o_ref,
                     keys_ref, m_ref, l_ref, acc_ref, *, ktop, tq):
    f32 = jnp.float32
    i = pl.program_id(0)
    nkb = i + 1
    col_t = i * tq + lax.broadcasted_iota(jnp.int32, (tq, tq), 1)
    row_s = lax.broadcasted_iota(jnp.int32, (tq, tq), 0)

    def score_blk(kb, carry):
        kib = ki_ref[kb]
        acc = jnp.zeros((tq, tq), f32)
        for h in range(IDX_HEADS):
            s = lax.dot_general(kib, qi_ref[h], _NT, preferred_element_type=f32)
            acc = acc + jnp.maximum(s, 0.0) * wT_ref[h:h + 1, :]
        sc = acc * IDX_SCALE
        bits = lax.bitcast_convert_type(sc, jnp.int32)
        key = bits ^ (lax.shift_right_arithmetic(bits, 31) & 0x7FFFFFFF)
        valid = (kb * tq + row_s) <= col_t
        keys_ref[kb] = jnp.where(valid, key, _INT_MIN)
        return carry

    lax.fori_loop(0, nkb, score_blk, 0)

    def bisect(it, ans_u):
        cand_u = ans_u | lax.shift_left(jnp.int32(1), 31 - it)
        cand_s = cand_u ^ _INT_MIN

        def count_blk(kb, cnt):
            hit = jnp.where(keys_ref[kb] >= cand_s, 1.0, 0.0)
            return cnt + hit.reshape(tq // 8, 8, tq).sum(axis=0)

        cnt = lax.fori_loop(0, nkb, count_blk, jnp.zeros((8, tq), f32))
        cnt = cnt.sum(axis=0, keepdims=True)
        return jnp.where(cnt >= ktop, cand_u, ans_u)

    ans_u = lax.fori_loop(0, 32, bisect, jnp.zeros((1, tq), jnp.int32))
    thr = jnp.maximum(ans_u ^ _INT_MIN, _INT_MIN + 1)

    m_ref[...] = jnp.full(m_ref.shape, _NEG_BIG, f32)
    l_ref[...] = jnp.zeros(l_ref.shape, f32)
    acc_ref[...] = jnp.zeros(acc_ref.shape, f32)
    scale = HEAD_DIM ** -0.5

    def attn_blk(kb, carry):
        sel = keys_ref[kb] >= thr
        kblk = k_ref[kb]
        vT = vT_ref[kb]
        for h in range(ATTN_HEADS):
            n = h // GROUP
            s = lax.dot_general(kblk[:, n * HEAD_DIM:(n + 1) * HEAD_DIM], q_ref[h], _NT,
                                preferred_element_type=f32) * scale
            s = jnp.where(sel, s, _NEG_BIG)
            m_old = m_ref[h]
            m_new = jnp.maximum(m_old, s.max(axis=0, keepdims=True))
            p = jnp.where(sel, jnp.exp(s - m_new), 0.0)
            alpha = jnp.exp(m_old - m_new)
            l_ref[h] = alpha * l_ref[h] + p.sum(axis=0, keepdims=True)
            pv = jnp.dot(vT[n * HEAD_DIM:(n + 1) * HEAD_DIM, :], p.astype(jnp.bfloat16),
                         preferred_element_type=f32)
            acc_ref[h] = alpha * acc_ref[h] + pv
            m_ref[h] = m_new
        return carry

    lax.fori_loop(0, nkb, attn_blk, 0)
    for h in range(ATTN_HEADS):
        o_ref[:, h * HEAD_DIM:(h + 1) * HEAD_DIM] = (acc_ref[h] / l_ref[h]).T


def _dsa_prompt_pallas(q, k, v, qi, ki, wi, interpret=False):
    T = q.shape[0]
    tq = min(256, T)
    nb = T // tq
    ktop = min(INDEX_TOPK, T // 4)
    bf16 = jnp.bfloat16
    qh = jnp.transpose(q.astype(bf16), (1, 0, 2))
    qih = jnp.transpose(qi.astype(bf16), (1, 0, 2))
    wT = wi.astype(jnp.float32).T
    kib = ki.astype(bf16).reshape(nb, tq, IDX_DIM)
    kb = k.astype(bf16).reshape(nb, tq, KV_HEADS * HEAD_DIM)
    vT = jnp.transpose(v.astype(bf16).reshape(nb, tq, KV_HEADS * HEAD_DIM), (0, 2, 1))
    body = functools.partial(_dsa_prompt_body, ktop=ktop, tq=tq)
    return pl.pallas_call(
        body,
        grid=(nb,),
        in_specs=[pl.BlockSpec((IDX_HEADS, tq, IDX_DIM), lambda i: (0, i, 0)),
                  pl.BlockSpec((IDX_HEADS, tq), lambda i: (0, i)),
                  pl.BlockSpec((ATTN_HEADS, tq, HEAD_DIM), lambda i: (0, i, 0)),
                  pl.BlockSpec((nb, tq, IDX_DIM), lambda i: (0, 0, 0)),
                  pl.BlockSpec((nb, tq, KV_HEADS * HEAD_DIM), lambda i: (0, 0, 0)),
                  pl.BlockSpec((nb, KV_HEADS * HEAD_DIM, tq), lambda i: (0, 0, 0))],
        out_specs=pl.BlockSpec((tq, ATTN_HEADS * HEAD_DIM), lambda i: (i, 0)),
        out_shape=jax.ShapeDtypeStruct((T, ATTN_HEADS * HEAD_DIM), jnp.float32),
        scratch_shapes=[pltpu.VMEM((nb, tq, tq), jnp.int32),
                        pltpu.VMEM((ATTN_HEADS, 1, tq), jnp.float32),
                        pltpu.VMEM((ATTN_HEADS, 1, tq), jnp.float32),
                        pltpu.VMEM((ATTN_HEADS, HEAD_DIM, tq), jnp.float32)],
        compiler_params=pltpu.CompilerParams(
            dimension_semantics=("arbitrary",),
            vmem_limit_bytes=56 * 1024 * 1024),
        name="dsa_prompt",
        interpret=interpret,
    )(qih, wT, qh, kib, kb, vT)


PAGES_PER_STEP = 8


def _sortable_key(x):
    bits = lax.bitcast_convert_type(x, jnp.int32)
    return bits ^ (lax.shift_right_arithmetic(bits, 31) & 0x7FFFFFFF)


def _sample_index_body(pt_ref, qi_ref, w_ref, kin_ref, *rest, ktop, nq, n_steps):
    del pt_ref
    pages = rest[:PAGES_PER_STEP]
    keys_out, knew_out, thr_out, keys_scr = rest[PAGES_PER_STEP:]
    f32 = jnp.float32
    j = pl.program_id(1)
    step_w = PAGES_PER_STEP * PAGE_SIZE
    past = n_steps * step_w
    qi = qi_ref[...]
    w = w_ref[...]

    def page_keys(page):
        s = lax.dot_general(qi, page.astype(jnp.bfloat16), _NT, preferred_element_type=f32)
        s = jnp.maximum(s, 0.0) * w
        return s.reshape(IDX_HEADS, nq, PAGE_SIZE).sum(axis=0) * IDX_SCALE

    blk = jnp.concatenate([_sortable_key(page_keys(p[...])) for p in pages], axis=1)
    keys_out[...] = blk
    keys_scr[j] = blk

    @pl.when(j == n_steps - 1)
    def _():
        kn = _sortable_key(page_keys(kin_ref[...]))
        col = lax.broadcasted_iota(jnp.int32, (nq, PAGE_SIZE), 1)
        row = lax.broadcasted_iota(jnp.int32, (nq, PAGE_SIZE), 0)
        kn = jnp.where(col <= row, kn, _INT_MIN)
        knew_out[...] = kn
        keys_scr[n_steps] = jnp.concatenate(
            [kn, jnp.full((nq, step_w - PAGE_SIZE), _INT_MIN, jnp.int32)], axis=1)

        def bisect(it, ans_u):
            cand_u = ans_u | lax.shift_left(jnp.int32(1), 31 - it)
            cand_s = cand_u ^ _INT_MIN
            cnt = jnp.zeros((nq, PAGE_SIZE), f32)
            for st in range(n_steps + 1):
                for c in range(PAGES_PER_STEP):
                    chunk = keys_scr[st, :, c * PAGE_SIZE:(c + 1) * PAGE_SIZE]
                    cnt = cnt + jnp.where(chunk >= cand_s, 1.0, 0.0)
            cnt = cnt.sum(axis=1, keepdims=True)
            return jnp.where(cnt >= ktop, cand_u, ans_u)

        ans_u = lax.fori_loop(0, 32, bisect, jnp.zeros((nq, 1), jnp.int32))
        thr = jnp.maximum(ans_u ^ _INT_MIN, _INT_MIN + 1)
        thr_out[...] = jnp.broadcast_to(thr, (nq, PAGE_SIZE))


def _sample_attn_body(pt_ref, q_ref, keys_ref, knew_ref, thr_ref, kn_ref, vn_ref, *rest, nq, n_steps):
    del pt_ref
    kp = rest[:PAGES_PER_STEP]
    vp = rest[PAGES_PER_STEP:2 * PAGES_PER_STEP]
    o_ref, m_ref, l_ref, acc_ref = rest[2 * PAGES_PER_STEP:]
    f32, bf16 = jnp.float32, jnp.bfloat16
    j = pl.program_id(1)
    scale = HEAD_DIM ** -0.5
    thr = thr_ref[...]

    def update(keys_q, thr_q, k_blocks, v_blocks):
        sel = jnp.concatenate([keys_q] * GROUP, axis=0) >= jnp.concatenate([thr_q] * GROUP, axis=0)
        for n in range(KV_HEADS):
            lo, hi = n * HEAD_DIM, (n + 1) * HEAD_DIM
            qn = q_ref[n]
            s = jnp.concatenate(
                [lax.dot_general(qn, kb[:, lo:hi].astype(bf16), _NT, preferred_element_type=f32)
                 for kb in k_blocks], axis=1) * scale
            s = jnp.where(sel, s, _NEG_BIG)
            m_old = m_ref[n]
            m_new = jnp.maximum(m_old, s.max(axis=1, keepdims=True))
            p = jnp.where(sel, jnp.exp(s - m_new), 0.0)
            alpha = jnp.exp(m_old - m_new)
            l_ref[n] = alpha * l_ref[n] + p.sum(axis=1, keepdims=True)
            pb = p.astype(bf16)
            pv = jnp.zeros((GROUP * nq, HEAD_DIM), f32)
            for c, vb in enumerate(v_blocks):
                pv = pv + jnp.dot(pb[:, c * PAGE_SIZE:(c + 1) * PAGE_SIZE], vb[:, lo:hi].astype(bf16),
                                  preferred_element_type=f32)
            acc_ref[n] = alpha * acc_ref[n] + pv
            m_ref[n] = m_new

    @pl.when(j == 0)
    def _():
        m_ref[...] = jnp.full(m_ref.shape, _NEG_BIG, f32)
        l_ref[...] = jnp.zeros(l_ref.shape, f32)
        acc_ref[...] = jnp.zeros(acc_ref.shape, f32)
        update(knew_ref[...], thr, [kn_ref[...]], [vn_ref[...]])

    thr_w = jnp.concatenate([thr] * PAGES_PER_STEP, axis=1)
    update(keys_ref[...], thr_w, [r[...] for r in kp], [r[...] for r in vp])

    @pl.when(j == n_steps - 1)
    def _():
        for n in range(KV_HEADS):
            o_ref[n] = acc_ref[n] / l_ref[n]


def _dsa_sample_pallas(q, k, v, qi, ki, wi, cache_k, cache_v, cache_idx_k, page_table, interpret=False):
    B, T = q.shape[:2]
    n_pages = page_table.shape[1]
    past = n_pages * PAGE_SIZE
    ktop = min(INDEX_TOPK, (past + T) // 4)
    n_steps = n_pages // PAGES_PER_STEP
    step_w = PAGES_PER_STEP * PAGE_SIZE
    f32, bf16 = jnp.float32, jnp.bfloat16
    kvw = KV_HEADS * HEAD_DIM
    qi_s = jnp.transpose(qi.astype(bf16), (0, 2, 1, 3)).reshape(B, IDX_HEADS * T, IDX_DIM)
    w_s = jnp.transpose(wi.astype(f32), (0, 2, 1)).reshape(B, IDX_HEADS * T, 1)
    pad_rows = lambda a: jnp.pad(a, ((0, 0), (0, PAGE_SIZE - T), (0, 0)))
    ki_new = pad_rows(ki.astype(f32))
    k_new = pad_rows(k.reshape(B, T, kvw))
    v_new = pad_rows(v.reshape(B, T, kvw))
    q_s = jnp.transpose(q.astype(bf16).reshape(B, T, KV_HEADS, GROUP, HEAD_DIM),
                        (0, 2, 3, 1, 4)).reshape(B, KV_HEADS, GROUP * T, HEAD_DIM)
    ck = cache_k.reshape(cache_k.shape[0], PAGE_SIZE, kvw)
    cv = cache_v.reshape(cache_v.shape[0], PAGE_SIZE, kvw)

    def page_map(r):
        return lambda b, j, pt: (pt[b, j * PAGES_PER_STEP + r], 0, 0)

    per_b3 = lambda b, j, pt: (b, 0, 0)
    idx_pages = [pl.BlockSpec((None, PAGE_SIZE, IDX_DIM), page_map(r)) for r in range(PAGES_PER_STEP)]
    keys, knew, thr = pl.pallas_call(
        functools.partial(_sample_index_body, ktop=ktop, nq=T, n_steps=n_steps),
        grid_spec=pltpu.PrefetchScalarGridSpec(
            num_scalar_prefetch=1,
            grid=(B, n_steps),
            in_specs=[pl.BlockSpec((None, IDX_HEADS * T, IDX_DIM), per_b3),
                      pl.BlockSpec((None, IDX_HEADS * T, 1), per_b3),
                      pl.BlockSpec((None, PAGE_SIZE, IDX_DIM), per_b3)] + idx_pages,
            out_specs=[pl.BlockSpec((None, T, step_w), lambda b, j, pt: (b, 0, j)),
                       pl.BlockSpec((None, T, PAGE_SIZE), per_b3),
                       pl.BlockSpec((None, T, PAGE_SIZE), per_b3)],
            scratch_shapes=[pltpu.VMEM((n_steps + 1, T, step_w), jnp.int32)]),
        out_shape=[jax.ShapeDtypeStruct((B, T, past), jnp.int32),
                   jax.ShapeDtypeStruct((B, T, PAGE_SIZE), jnp.int32),
                   jax.ShapeDtypeStruct((B, T, PAGE_SIZE), jnp.int32)],
        compiler_params=pltpu.CompilerParams(dimension_semantics=("arbitrary", "arbitrary")),
        name="sample_index",
        interpret=interpret,
    )(page_table, qi_s, w_s, ki_new, *([cache_idx_k] * PAGES_PER_STEP))

    kv_pages = [pl.BlockSpec((None, PAGE_SIZE, kvw), page_map(r)) for r in range(PAGES_PER_STEP)]
    o = pl.pallas_call(
        functools.partial(_sample_attn_body, nq=T, n_steps=n_steps),
        grid_spec=pltpu.PrefetchScalarGridSpec(
            num_scalar_prefetch=1,
            grid=(B, n_steps),
            in_specs=[pl.BlockSpec((None, KV_HEADS, GROUP * T, HEAD_DIM), lambda b, j, pt: (b, 0, 0, 0)),
                      pl.BlockSpec((None, T, step_w), lambda b, j, pt: (b, 0, j)),
                      pl.BlockSpec((None, T, PAGE_SIZE), per_b3),
                      pl.BlockSpec((None, T, PAGE_SIZE), per_b3),
                      pl.BlockSpec((None, PAGE_SIZE, kvw), per_b3),
                      pl.BlockSpec((None, PAGE_SIZE, kvw), per_b3)] + kv_pages + kv_pages,
            out_specs=pl.BlockSpec((None, KV_HEADS, GROUP * T, HEAD_DIM), lambda b, j, pt: (b, 0, 0, 0)),
            scratch_shapes=[pltpu.VMEM((KV_HEADS, GROUP * T, 1), f32),
                            pltpu.VMEM((KV_HEADS, GROUP * T, 1), f32),
                            pltpu.VMEM((KV_HEADS, GROUP * T, HEAD_DIM), f32)]),
        out_shape=jax.ShapeDtypeStruct((B, KV_HEADS, GROUP * T, HEAD_DIM), f32),
        compiler_params=pltpu.CompilerParams(dimension_semantics=("arbitrary", "arbitrary")),
        name="sample_attn",
        interpret=interpret,
    )(page_table, q_s, keys, knew, thr, k_new, v_new, *([ck] * PAGES_PER_STEP), *([cv] * PAGES_PER_STEP))
    o = o.reshape(B, KV_HEADS, GROUP, T, HEAD_DIM)
    return jnp.transpose(o, (0, 3, 1, 2, 4)).reshape(B, T, ATTN_HEADS * HEAD_DIM)


def _dsa_sample(q, k, v, qi, ki, wi, cache_k, cache_v, cache_idx_k, page_table):
    B, T = q.shape[:2]
    past = page_table.shape[1] * PAGE_SIZE
    L = past + T
    ktop = min(INDEX_TOPK, L // 4)
    ki_past = cache_idx_k[page_table].reshape(B, past, IDX_DIM).astype(ki.dtype)
    ki_all = jnp.concatenate([ki_past, ki], axis=1)
    tpos = past + jnp.arange(T)
    sc = _index_scores(qi, wi, ki_all)
    sc = jnp.where(jnp.arange(L)[None, None, :] <= tpos[None, :, None], sc, -jnp.inf)
    _, idx = lax.top_k(sc, ktop)
    valid = idx <= tpos[None, :, None]
    bidx = jnp.arange(B)[:, None, None]
    in_past = (idx < past)[..., None, None]
    pidx = jnp.minimum(idx, past - 1)
    phys = page_table[bidx, pidx // PAGE_SIZE]
    off = pidx % PAGE_SIZE
    nidx = jnp.clip(idx - past, 0, T - 1)
    kg = jnp.where(in_past, cache_k[phys, off].astype(k.dtype), k[bidx, nidx])
    vg = jnp.where(in_past, cache_v[phys, off].astype(v.dtype), v[bidx, nidx])
    return _gathered_attention(q, kg, vg, valid)


def _causal_conv(x, buf, conv_w):
    T = x.shape[1]
    xp = jnp.concatenate([buf.astype(x.dtype), x], axis=1)
    y = xp[:, 0:T] * conv_w[0]
    for j in range(1, CONV_W):
        y = y + xp[:, j:j + T] * conv_w[j]
    return jax.nn.silu(y), xp[:, T:]


def _chunk_gated_delta(q, k, v, g, beta, S0):
    B, T, H, DK = q.shape
    DV = v.shape[-1]
    C = min(GDN_CHUNK, T)
    n = -(-T // C)
    pad = n * C - T

    def prep(a):
        a = jnp.pad(a, [(0, 0), (0, pad)] + [(0, 0)] * (a.ndim - 2))
        a = a.reshape((B, n, C) + a.shape[2:])
        return jnp.swapaxes(jnp.moveaxis(a, 1, 0), 2, 3)

    q, k, v, g, beta = prep(q), prep(k), prep(v), prep(g), prep(beta)
    gc = jnp.cumsum(g, axis=-1)
    causal = jnp.tril(jnp.ones((C, C), bool))
    strict = jnp.tril(jnp.ones((C, C), bool), -1)
    decay = jnp.exp(jnp.where(causal, gc[..., :, None] - gc[..., None, :], -jnp.inf))
    kb = k * beta[..., None]
    lower = jnp.where(strict, jnp.einsum('...id,...jd->...ij', kb, k) * decay, 0.0)
    amat = lower + jnp.eye(C, dtype=jnp.float32)
    rhs = jnp.concatenate([v * beta[..., None], kb * jnp.exp(gc)[..., None]], axis=-1)
    sol = lax.linalg.triangular_solve(amat, rhs, left_side=True, lower=True, unit_diagonal=True)
    u, w = sol[..., :DV], sol[..., DV:]
    qk = jnp.einsum('...id,...jd->...ij', q, k) * decay
    qg = q * jnp.exp(gc)[..., None]
    kd = k * jnp.exp(gc[..., -1:] - gc)[..., None]
    g_last = jnp.exp(gc[..., -1])

    def step(S, xs):
        qg_i, kd_i, u_i, w_i, qk_i, gl_i = xs
        v_new = u_i - jnp.einsum('bhck,bhkv->bhcv', w_i, S)
        o = jnp.einsum('bhck,bhkv->bhcv', qg_i, S) + jnp.einsum('bhcj,bhjv->bhcv', qk_i, v_new)
        S = S * gl_i[..., None, None] + jnp.einsum('bhck,bhcv->bhkv', kd_i, v_new)
        return S, o

    S, o = lax.scan(step, S0, (qg, kd, u, w, qk, g_last))
    o = jnp.moveaxis(jnp.swapaxes(o, 2, 3), 0, 1).reshape(B, n * C, H, DV)[:, :T]
    return o, S


def _gated_deltanet(qkv, b_raw, a_raw, gate, conv_buf, S0, conv_w, a_log, dt_bias, gdn_norm_w):
    B, T, _ = qkv.shape
    f32 = jnp.float32
    conv, new_buf = _causal_conv(qkv, conv_buf, conv_w)
    qc, kc, vc = jnp.split(conv, [GDN_HEADS * GDN_DK, 2 * GDN_HEADS * GDN_DK], axis=-1)
    q = _l2norm(qc.reshape(B, T, GDN_HEADS, GDN_DK).astype(f32)) * (GDN_DK ** -0.5)
    k = _l2norm(kc.reshape(B, T, GDN_HEADS, GDN_DK).astype(f32))
    v = vc.reshape(B, T, GDN_HEADS, GDN_DV).astype(f32)
    beta = jax.nn.sigmoid(b_raw.astype(f32))
    g = -jnp.exp(a_log.astype(f32)) * jax.nn.softplus(a_raw.astype(f32) + dt_bias.astype(f32))
    o, S = _chunk_gated_delta(q, k, v, g, beta, S0)
    o = _rmsnorm(o, gdn_norm_w) * jax.nn.silu(gate.reshape(B, T, GDN_HEADS, GDN_DV).astype(f32))
    return o.reshape(B, T, GDN_HEADS * GDN_DV).astype(qkv.dtype), S, new_buf


def _peer(xn, peer_wq, peer_keys, peer_u, peer_v):
    shape = xn.shape
    xf = xn.reshape(-1, D_MODEL)
    N = xf.shape[0]
    nb = -(-N // PEER_BLOCK)
    xb = jnp.pad(xf, ((0, nb * PEER_BLOCK - N), (0, 0))).reshape(nb, PEER_BLOCK, D_MODEL)
    ncand = PEER_TOPK * PEER_TOPK

    def blk(x):
        qh = (x @ peer_wq).reshape(PEER_BLOCK, PEER_HEADS, 2, PEER_QDIM // 2)
        s1 = jnp.einsum('thd,hkd->thk', qh[:, :, 0], peer_keys[0]).astype(jnp.float32)
        s2 = jnp.einsum('thd,hkd->thk', qh[:, :, 1], peer_keys[1]).astype(jnp.float32)
        v1, i1 = lax.top_k(s1, PEER_TOPK)
        v2, i2 = lax.top_k(s2, PEER_TOPK)
        cand = (v1[..., :, None] + v2[..., None, :]).reshape(PEER_BLOCK, PEER_HEADS, ncand)
        cidx = (i1[..., :, None] * PEER_NKEYS + i2[..., None, :]).reshape(PEER_BLOCK, PEER_HEADS, ncand)
        sv, si = lax.top_k(cand, PEER_TOPK)
        eidx = jnp.take_along_axis(cidx, si, axis=-1)
        gsm = jax.nn.softmax(sv, axis=-1)
        act = jax.nn.gelu(jnp.einsum('thkd,td->thk', peer_u[eidx], x).astype(jnp.float32), approximate=False)
        return jnp.einsum('thk,thkd->td', (gsm * act).astype(x.dtype), peer_v[eidx])

    y = lax.map(blk, xb).reshape(nb * PEER_BLOCK, D_MODEL)[:N]
    return y.reshape(shape)


_SQRT_HALF = 0.7071067811865476


def _top_rows(x, k):
    R, n = x.shape
    ri = lax.broadcasted_iota(jnp.int32, (R, n), 0).astype(jnp.float32)
    ki = lax.broadcasted_iota(jnp.int32, (k, n), 0)

    def body(r, c):
        x, out = c
        m = x.max(axis=0, keepdims=True)
        first = jnp.min(jnp.where(x == m, ri, float(R)), axis=0, keepdims=True)
        x = jnp.where(ri == first, -jnp.inf, x)
        out = jnp.where(ki == r, m, out)
        return x, out

    _, out = lax.fori_loop(0, k, body, (x, jnp.zeros((k, n), jnp.float32)))
    return out


def _peer_front_body(x_ref, oa_ref, ob_ref, wo_ref, g_ref, wq_ref, keys_ref,
                     h_ref, xn_ref, s1_ref, s2_ref, e2_ref, aux_ref):
    f32, bf16 = jnp.float32, jnp.bfloat16
    half_w = ATTN_HEADS * HEAD_DIM
    h = (x_ref[...]
         + jnp.dot(oa_ref[...].astype(bf16), wo_ref[:half_w, :], preferred_element_type=f32)
         + jnp.dot(ob_ref[...].astype(bf16), wo_ref[half_w:, :], preferred_element_type=f32))
    h_ref[...] = h
    xn = (h * lax.rsqrt(jnp.mean(h * h, axis=-1, keepdims=True) + NORM_EPS) * g_ref[...]).astype(bf16)
    xn_ref[...] = xn
    qh = jnp.dot(xn, wq_ref[...], preferred_element_type=f32).astype(bf16)
    tq = qh.shape[0]
    hq = PEER_QDIM // 2
    for hh in range(PEER_HEADS):
        tops = []
        for half in range(2):
            col = (hh * 2 + half) * hq
            sT = lax.dot_general(keys_ref[half, hh], qh[:, col:col + hq], _NT,
                                 preferred_element_type=f32)
            (s1_ref if half == 0 else s2_ref)[hh] = sT
            tops.append(_top_rows(sT, PEER_TOPK))
        a16, b16 = tops
        cand = jnp.concatenate([a16[r:r + 1, :] + b16 for r in range(PEER_TOPK)], axis=0)
        tau = _top_rows(cand, PEER_TOPK)[PEER_TOPK - 1:PEER_TOPK, :]
        top_sum = a16[0:1, :] + b16[0:1, :]
        z = jnp.sum(jnp.where(cand >= tau, jnp.exp(cand - top_sum), 0.0), axis=0, keepdims=True)
        e2_ref[hh] = jnp.exp(s2_ref[hh] - b16[0:1, :]) / z
        aux_ref[hh] = jnp.concatenate([tau, a16[0:1, :], jnp.zeros((6, tq), f32)], axis=0)


def _peer_front(x, oa, ob, wo, g, wq, keys, tq, interpret=False):
    n = x.shape[0]
    half_w = ATTN_HEADS * HEAD_DIM
    tok = lambda i: (i, 0)
    fix2 = lambda i: (0, 0)
    colT = lambda i: (0, 0, i)
    f32 = jnp.float32
    plane = jax.ShapeDtypeStruct((PEER_HEADS, PEER_NKEYS, n), f32)
    plane_spec = pl.BlockSpec((PEER_HEADS, PEER_NKEYS, tq), colT)
    return pl.pallas_call(
        _peer_front_body,
        grid=(n // tq,),
        in_specs=[pl.BlockSpec((tq, D_MODEL), tok),
                  pl.BlockSpec((tq, half_w), tok),
                  pl.BlockSpec((tq, half_w), tok),
                  pl.BlockSpec((D_MODEL, D_MODEL), fix2),
                  pl.BlockSpec((1, D_MODEL), fix2),
                  pl.BlockSpec((D_MODEL, PEER_HEADS * PEER_QDIM), fix2),
                  pl.BlockSpec((2, PEER_HEADS, PEER_NKEYS, PEER_QDIM // 2), lambda i: (0, 0, 0, 0))],
        out_specs=[pl.BlockSpec((tq, D_MODEL), tok),
                   pl.BlockSpec((tq, D_MODEL), tok),
                   plane_spec, plane_spec, plane_spec,
                   pl.BlockSpec((PEER_HEADS, 8, tq), colT)],
        out_shape=[jax.ShapeDtypeStruct((n, D_MODEL), f32),
                   jax.ShapeDtypeStruct((n, D_MODEL), jnp.bfloat16),
                   plane, plane, plane,
                   jax.ShapeDtypeStruct((PEER_HEADS, 8, n), f32)],
        compiler_params=pltpu.CompilerParams(
            dimension_semantics=("arbitrary",),
            vmem_limit_bytes=56 * 1024 * 1024),
        name="peer_front",
        interpret=interpret,
    )(x, oa, ob, wo, g.reshape(1, D_MODEL), wq, keys)


def _peer_dense_body(xn_ref, u_ref, vT_ref, s1_ref, s2_ref, e2_ref, aux_ref, yT_ref, *, eblk):
    f32 = jnp.float32
    eb = pl.program_id(1)

    @pl.when(eb == 0)
    def _():
        yT_ref[...] = jnp.zeros(yT_ref.shape, f32)

    a = lax.dot_general(u_ref[...], xn_ref[...], _NT, preferred_element_type=f32)
    act = 0.5 * a * (1.0 + lax.erf(a * _SQRT_HALF))
    sub = eblk // PEER_NKEYS
    pieces = []
    for r in range(sub):
        i1 = eb * sub + r
        gate = jnp.zeros((PEER_NKEYS, a.shape[1]), f32)
        for hh in range(PEER_HEADS):
            s1row = s1_ref[hh, pl.ds(i1, 1), :]
            tau = aux_ref[hh, 0:1, :]
            e1row = jnp.exp(s1row - aux_ref[hh, 1:2, :])
            gate = gate + jnp.where(s1row + s2_ref[hh] >= tau, e1row * e2_ref[hh], 0.0)
        pieces.append((gate * act[r * PEER_NKEYS:(r + 1) * PEER_NKEYS, :]).astype(jnp.bfloat16))
    hT = jnp.concatenate(pieces, axis=0)
    yT_ref[...] += jnp.dot(vT_ref[...], hT, preferred_element_type=f32)


def _peer_dense(xn, u, vT, s1, s2, e2, aux, tq, eblk, interpret=False):
    n = xn.shape[0]
    ne = u.shape[0]
    plane_spec = pl.BlockSpec((PEER_HEADS, PEER_NKEYS, tq), lambda i, e: (0, 0, i))
    return pl.pallas_call(
        functools.partial(_peer_dense_body, eblk=eblk),
        grid=(n // tq, ne // eblk),
        in_specs=[pl.BlockSpec((tq, D_MODEL), lambda i, e: (i, 0)),
                  pl.BlockSpec((eblk, D_MODEL), lambda i, e: (e, 0)),
                  pl.BlockSpec((D_MODEL, eblk), lambda i, e: (0, e)),
                  plane_spec, plane_spec, plane_spec,
                  pl.BlockSpec((PEER_HEADS, 8, tq), lambda i, e: (0, 0, i))],
        out_specs=pl.BlockSpec((D_MODEL, tq), lambda i, e: (0, i)),
        out_shape=jax.ShapeDtypeStruct((D_MODEL, n), jnp.float32),
        compiler_params=pltpu.CompilerParams(
            dimension_semantics=("arbitrary", "arbitrary"),
            vmem_limit_bytes=56 * 1024 * 1024),
        name="peer_dense",
        interpret=interpret,
    )(xn, u, vT, s1, s2, e2, aux)


def _layer_out_pallas(x, oa, ob, wo_b, ffn_norm_w, wq_b, keys_b, u_b, vT_b, interpret=False):
    n = x.shape[0]
    tq1 = min(256, n)
    tq2 = 512 if n % 512 == 0 else min(256, n)
    h, xn, s1, s2, e2, aux = _peer_front(x, oa, ob, wo_b, ffn_norm_w, wq_b, keys_b, tq1, interpret)
    yT = _peer_dense(xn, u_b, vT_b, s1, s2, e2, aux, tq2, 512, interpret)
    return h + yT.T


def _layer_out(x, oa, ob, w_out, ffn_norm_w, peer_wq, peer_keys, peer_u, peer_v):
    h = x + jnp.concatenate([oa, ob], axis=-1) @ w_out
    return h + _peer(_rmsnorm(h, ffn_norm_w), peer_wq, peer_keys, peer_u, peer_v)


def kernel(x_prompt, x_sample, cache_k, cache_v, cache_idx_k, state_ssm, state_conv, page_table,
           attn_norm_w, w_in, q_norm_w, k_norm_w, idx_k_norm_w, conv_w, a_log, dt_bias, gdn_norm_w,
           w_out, ffn_norm_w, peer_wq, peer_keys, peer_u, peer_v):
    l = 0
    n_pad = -(-IN_COLS // 1024) * 1024
    w_in_p = jnp.pad(w_in[l], ((0, 0), (0, n_pad - IN_COLS))).astype(jnp.bfloat16)
    proj_w = (attn_norm_w[l], w_in_p, q_norm_w[l], k_norm_w[l], idx_k_norm_w[l])
    gdn_w = (conv_w[l], a_log[l], dt_bias[l], gdn_norm_w[l])
    bf16 = jnp.bfloat16
    out_w = (w_out[l].astype(bf16), ffn_norm_w[l], peer_wq[l].astype(bf16), peer_keys[l].astype(bf16),
             peer_u[l].astype(bf16), peer_v[l].astype(bf16).T)

    hp, hs = x_prompt, x_sample
    (qa, ka, va, qi, ki, wi), gdn_in = _in_projection(hp, *proj_w)
    oa = _dsa_prompt_pallas(qa[0], ka[0], va[0], qi[0], ki[0], wi[0])[None]
    Bp = hp.shape[0]
    buf0 = jnp.zeros((Bp, CONV_W - 1, CONV_DIM), hp.dtype)
    S0 = jnp.zeros((Bp, GDN_HEADS, GDN_DK, GDN_DV), jnp.float32)
    ob, S_p, buf_p = _gated_deltanet(*gdn_in, buf0, S0, *gdn_w)
    half_w = ATTN_HEADS * HEAD_DIM
    hp = _layer_out_pallas(hp[0], oa[0], ob[0], *out_w)[None]
    kp, vp, ip = ka, va, ki

    (qa, ka, va, qi, ki, wi), gdn_in = _in_projection(hs, *proj_w)
    oa = _dsa_sample_pallas(qa, ka, va, qi, ki, wi, cache_k[l], cache_v[l], cache_idx_k[l], page_table)
    ob, S_s, buf_s = _gated_deltanet(*gdn_in, state_conv[l], state_ssm[l].astype(jnp.float32), *gdn_w)
    ns = hs.shape[0] * hs.shape[1]
    hs = _layer_out_pallas(hs.reshape(ns, D_MODEL), oa.reshape(ns, half_w), ob.reshape(ns, half_w),
                           *out_w).reshape(hs.shape)

    return (hp, hs, kp[None], vp[None], ip[None], S_p[None], buf_p[None],
            ka[None], va[None], ki[None], S_s[None], buf_s[None])
```

```python
import functools
import math

import jax
import jax.numpy as jnp
from jax import lax
from jax.experimental import pallas as pl
from jax.experimental.pallas import tpu as pltpu

D_MODEL = 2048
PAGE_SIZE = 128
HEAD_DIM = 128
ATTN_HEADS = 8
KV_HEADS = 2
GROUP = ATTN_HEADS // KV_HEADS
IDX_HEADS = 16
IDX_DIM = 64
INDEX_TOPK = 256
Q_BLOCK = 128
IDX_SCALE = (IDX_HEADS * IDX_DIM) ** -0.5
GDN_DK = 128
GDN_DV = 128
GDN_HEADS = 8
CONV_W = 4
CONV_DIM = GDN_HEADS * (2 * GDN_DK + GDN_DV)
GDN_CHUNK = 64
IN_SIZES = (ATTN_HEADS * HEAD_DIM, KV_HEADS * HEAD_DIM, KV_HEADS * HEAD_DIM,
            IDX_HEADS * IDX_DIM, IDX_DIM, IDX_HEADS,
            CONV_DIM, GDN_HEADS, GDN_HEADS, GDN_HEADS * GDN_DV)
IN_COLS = sum(IN_SIZES)
PEER_HEADS = 8
PEER_NKEYS = 128
PEER_QDIM = 256
PEER_TOPK = 16
PEER_BLOCK = 128
NORM_EPS = 1e-6

LANES = 128


def _rmsnorm(x, w):
    xf = x.astype(jnp.float32)
    y = xf * lax.rsqrt(jnp.mean(xf * xf, axis=-1, keepdims=True) + NORM_EPS)
    return (y * w.astype(jnp.float32)).astype(x.dtype)


def _l2norm(x):
    return x * lax.rsqrt(jnp.sum(x * x, axis=-1, keepdims=True) + NORM_EPS)


def _norm_matmul_body(x_ref, g_ref, w_ref, o_ref, xn_ref):
    @pl.when(pl.program_id(1) == 0)
    def _():
        x = x_ref[...]
        r = lax.rsqrt(jnp.mean(x * x, axis=-1, keepdims=True) + NORM_EPS)
        xn_ref[...] = (x * r * g_ref[...]).astype(jnp.bfloat16)

    o_ref[...] = jnp.dot(xn_ref[...], w_ref[...], preferred_element_type=jnp.float32)


def _norm_matmul(x, g, w, tm, tn):
    m, k = x.shape
    n = w.shape[1]
    return pl.pallas_call(
        _norm_matmul_body,
        grid=(m // tm, n // tn),
        in_specs=[pl.BlockSpec((tm, k), lambda i, j: (i, 0)),
                  pl.BlockSpec((1, k), lambda i, j: (0, 0)),
                  pl.BlockSpec((k, tn), lambda i, j: (0, j))],
        out_specs=pl.BlockSpec((tm, tn), lambda i, j: (i, j)),
        out_shape=jax.ShapeDtypeStruct((m, n), jnp.float32),
        scratch_shapes=[pltpu.VMEM((tm, k), jnp.bfloat16)],
        compiler_params=pltpu.CompilerParams(
            dimension_semantics=("arbitrary", "arbitrary"),
            vmem_limit_bytes=48 * 1024 * 1024),
        name="norm_matmul",
    )(x, g.reshape(1, k), w)


def _in_projection(x, attn_norm_w, w_in_p, q_norm_w, k_norm_w, idx_k_norm_w):
    B, T, _ = x.shape
    m = B * T
    tm = 512 if m % 512 == 0 else m
    z = _norm_matmul(x.reshape(m, D_MODEL), attn_norm_w, w_in_p, tm, 1024)
    z = z[:, :IN_COLS].reshape(B, T, IN_COLS)
    offs = []
    acc = 0
    for s in IN_SIZES[:-1]:
        acc += s
        offs.append(acc)
    qa, ka, va, qi, ki, wi, qkv, b_raw, a_raw, gate = jnp.split(z, offs, axis=-1)
    qa = _rmsnorm(qa.reshape(B, T, ATTN_HEADS, HEAD_DIM), q_norm_w)
    ka = _rmsnorm(ka.reshape(B, T, KV_HEADS, HEAD_DIM), k_norm_w)
    va = va.reshape(B, T, KV_HEADS, HEAD_DIM)
    qi = qi.reshape(B, T, IDX_HEADS, IDX_DIM)
    ki = _rmsnorm(ki, idx_k_norm_w)
    return (qa, ka, va, qi, ki, wi), (qkv, b_raw, a_raw, gate)


def _index_scores(qi, wi, ki):
    s = jax.nn.relu(jnp.einsum('bqhd,bsd->bqhs', qi, ki).astype(jnp.float32))
    return jnp.einsum('bqhs,bqh->bqs', s, wi.astype(jnp.float32)) * IDX_SCALE


def _gathered_attention(q, kg, vg, valid):
    B, Q = q.shape[:2]
    qg = q.reshape(B, Q, KV_HEADS, GROUP, HEAD_DIM)
    s = jnp.einsum('bqngd,bqknd->bqngk', qg, kg).astype(jnp.float32) * (HEAD_DIM ** -0.5)
    s = jnp.where(valid[:, :, None, None, :], s, -jnp.inf)
    p = jax.nn.softmax(s, axis=-1)
    o = jnp.einsum('bqngk,bqknd->bqngd', p.astype(vg.dtype), vg)
    return o.reshape(B, Q, ATTN_HEADS * HEAD_DIM)


def _dsa_prompt(q, k, v, qi, ki, wi):
    B, T = q.shape[:2]
    ktop = min(INDEX_TOPK, T // 4)
    nb = T // Q_BLOCK
    bidx = jnp.arange(B)[:, None, None]
    spos = jnp.arange(T)

    def blocks(a):
        return jnp.moveaxis(a.reshape((B, nb, Q_BLOCK) + a.shape[2:]), 1, 0)

    def blk(xs):
        qb, qib, wib, start = xs
        tpos = start + jnp.arange(Q_BLOCK)
        sc = _index_scores(qib, wib, ki)
        sc = jnp.where(spos[None, None, :] <= tpos[None, :, None], sc, -jnp.inf)
        _, idx = lax.top_k(sc, ktop)
        valid = idx <= tpos[None, :, None]
        return _gathered_attention(qb, k[bidx, idx], v[bidx, idx], valid)

    o = lax.map(blk, (blocks(q), blocks(qi), blocks(wi), jnp.arange(nb) * Q_BLOCK))
    return jnp.moveaxis(o, 0, 1).reshape(B, T, ATTN_HEADS * HEAD_DIM)


_INT_MIN = -2 ** 31
_NEG_BIG = -1e30
_NT = (((1,), (1,)), ((), ()))


def _dsa_prompt_body(qi_ref, wT_ref, q_ref, ki_ref, k_ref, vT_ref, o_ref,
                     keys_ref, m_ref, l_ref, acc_ref, *, ktop, tq):
    f32 = jnp.float32
    i = pl.program_id(0)
    nkb = i + 1
    col_t = i * tq + lax.broadcasted_iota(jnp.int32, (tq, tq), 1)
    row_s = lax.broadcasted_iota(jnp.int32, (tq, tq), 0)

    def score_blk(kb, carry):
        kib = ki_ref[kb]
        acc = jnp.zeros((tq, tq), f32)
        for h in range(IDX_HEADS):
            s = lax.dot_general(kib, qi_ref[h], _NT, preferred_element_type=f32)
            acc = acc + jnp.maximum(s, 0.0) * wT_ref[h:h + 1, :]
        sc = acc * IDX_SCALE
        bits = lax.bitcast_convert_type(sc, jnp.int32)
        key = bits ^ (lax.shift_right_arithmetic(bits, 31) & 0x7FFFFFFF)
        valid = (kb * tq + row_s) <= col_t
        keys_ref[kb] = jnp.where(valid, key, _INT_MIN)
        return carry

    lax.fori_loop(0, nkb, score_blk, 0)

    def bisect(it, ans_u):
        cand_u = ans_u | lax.shift_left(jnp.int32(1), 31 - it)
        cand_s = cand_u ^ _INT_MIN

        def count_blk(kb, cnt):
            hit = jnp.where(keys_ref[kb] >= cand_s, 1.0, 0.0)
            return cnt + hit.reshape(tq // 8, 8, tq).sum(axis=0)

        cnt = lax.fori_loop(0, nkb, count_blk, jnp.zeros((8, tq), f32))
        cnt = cnt.sum(axis=0, keepdims=True)
        return jnp.where(cnt >= ktop, cand_u, ans_u)

    ans_u = lax.fori_loop(0, 32, bisect, jnp.zeros((1, tq), jnp.int32))
    thr = jnp.maximum(ans_u ^ _INT_MIN, _INT_MIN + 1)

    m_ref[...] = jnp.full(m_ref.shape, _NEG_BIG, f32)
    l_ref[...] = jnp.zeros(l_ref.shape, f32)
    acc_ref[...] = jnp.zeros(acc_ref.shape, f32)
    scale = HEAD_DIM ** -0.5

    def attn_blk(kb, carry):
        sel = keys_ref[kb] >= thr
        kblk = k_ref[kb]
        vT = vT_ref[kb]
        s_all = [lax.dot_general(kblk[:, (h // GROUP) * HEAD_DIM:(h // GROUP + 1) * HEAD_DIM], q_ref[h], _NT,
                                 preferred_element_type=f32) for h in range(ATTN_HEADS)]
        for h in range(ATTN_HEADS):
            n = h // GROUP
            s = jnp.where(sel, s_all[h] * scale, _NEG_BIG)
            m_old = m_ref[h]
            m_new = jnp.maximum(m_old, s.max(axis=0, keepdims=True))
            p = jnp.where(sel, jnp.exp(s - m_new), 0.0)
            alpha = jnp.exp(m_old - m_new)
            l_ref[h] = alpha * l_ref[h] + p.sum(axis=0, keepdims=True)
            pv = jnp.dot(vT[n * HEAD_DIM:(n + 1) * HEAD_DIM, :], p.astype(jnp.bfloat16),
                         preferred_element_type=f32)
            acc_ref[h] = alpha * acc_ref[h] + pv
            m_ref[h] = m_new
        return carry

    lax.fori_loop(0, nkb, attn_blk, 0)
    for h in range(ATTN_HEADS):
        o_ref[:, h * HEAD_DIM:(h + 1) * HEAD_DIM] = (acc_ref[h] / l_ref[h]).T


def _dsa_prompt_pallas(q, k, v, qi, ki, wi, interpret=False):
    T = q.shape[0]
    tq = min(256, T)
    nb = T // tq
    ktop = min(INDEX_TOPK, T // 4)
    bf16 = jnp.bfloat16
    qh = jnp.transpose(q.astype(bf16), (1, 0, 2))
    qih = jnp.transpose(qi.astype(bf16), (1, 0, 2))
    wT = wi.astype(jnp.float32).T
    kib = ki.astype(bf16).reshape(nb, tq, IDX_DIM)
    kb = k.astype(bf16).reshape(nb, tq, KV_HEADS * HEAD_DIM)
    vT = jnp.transpose(v.astype(bf16).reshape(nb, tq, KV_HEADS * HEAD_DIM), (0, 2, 1))
    body = functools.partial(_dsa_prompt_body, ktop=ktop, tq=tq)
    return pl.pallas_call(
        body,
        grid=(nb,),
        in_specs=[pl.BlockSpec((IDX_HEADS, tq, IDX_DIM), lambda i: (0, i, 0)),
                  pl.BlockSpec((IDX_HEADS, tq), lambda i: (0, i)),
                  pl.BlockSpec((ATTN_HEADS, tq, HEAD_DIM), lambda i: (0, i, 0)),
                  pl.BlockSpec((nb, tq, IDX_DIM), lambda i: (0, 0, 0)),
                  pl.BlockSpec((nb, tq, KV_HEADS * HEAD_DIM), lambda i: (0, 0, 0)),
                  pl.BlockSpec((nb, KV_HEADS * HEAD_DIM, tq), lambda i: (0, 0, 0))],
        out_specs=pl.BlockSpec((tq, ATTN_HEADS * HEAD_DIM), lambda i: (i, 0)),
        out_shape=jax.ShapeDtypeStruct((T, ATTN_HEADS * HEAD_DIM), jnp.float32),
        scratch_shapes=[pltpu.VMEM((nb, tq, tq), jnp.int32),
                        pltpu.VMEM((ATTN_HEADS, 1, tq), jnp.float32),
                        pltpu.VMEM((ATTN_HEADS, 1, tq), jnp.float32),
                        pltpu.VMEM((ATTN_HEADS, HEAD_DIM, tq), jnp.float32)],
        compiler_params=pltpu.CompilerParams(
            dimension_semantics=("arbitrary",),
            vmem_limit_bytes=56 * 1024 * 1024),
        name="dsa_prompt",
        interpret=interpret,
    )(qih, wT, qh, kib, kb, vT)


PAGES_PER_STEP = 8


def _sortable_key(x):
    bits = lax.bitcast_convert_type(x, jnp.int32)
    return bits ^ (lax.shift_right_arithmetic(bits, 31) & 0x7FFFFFFF)


def _sample_index_body(pt_ref, qi_ref, w_ref, kin_ref, *rest, ktop, nq, n_steps):
    del pt_ref
    pages = rest[:PAGES_PER_STEP]
    keys_out, knew_out, thr_out, keys_scr = rest[PAGES_PER_STEP:]
    f32 = jnp.float32
    j = pl.program_id(1)
    step_w = PAGES_PER_STEP * PAGE_SIZE
    past = n_steps * step_w
    qi = qi_ref[...]
    w = w_ref[...]

    def page_keys(page):
        s = lax.dot_general(qi, page.astype(jnp.bfloat16), _NT, preferred_element_type=f32)
        s = jnp.maximum(s, 0.0) * w
        return s.reshape(IDX_HEADS, nq, PAGE_SIZE).sum(axis=0) * IDX_SCALE

    blk = jnp.concatenate([_sortable_key(page_keys(p[...])) for p in pages], axis=1)
    keys_out[...] = blk
    keys_scr[j] = blk

    @pl.when(j == n_steps - 1)
    def _():
        kn = _sortable_key(page_keys(kin_ref[...]))
        col = lax.broadcasted_iota(jnp.int32, (nq, PAGE_SIZE), 1)
        row = lax.broadcasted_iota(jnp.int32, (nq, PAGE_SIZE), 0)
        kn = jnp.where(col <= row, kn, _INT_MIN)
        knew_out[...] = kn
        keys_scr[n_steps] = jnp.concatenate(
            [kn, jnp.full((nq, step_w - PAGE_SIZE), _INT_MIN, jnp.int32)], axis=1)

        def bisect(it, ans_u):
            cand_u = ans_u | lax.shift_left(jnp.int32(1), 31 - it)
            cand_s = cand_u ^ _INT_MIN
            cnt = jnp.zeros((nq, PAGE_SIZE), f32)
            for st in range(n_steps + 1):
                for c in range(PAGES_PER_STEP):
                    chunk = keys_scr[st, :, c * PAGE_SIZE:(c + 1) * PAGE_SIZE]
                    cnt = cnt + jnp.where(chunk >= cand_s, 1.0, 0.0)
            cnt = cnt.sum(axis=1, keepdims=True)
            return jnp.where(cnt >= ktop, cand_u, ans_u)

        ans_u = lax.fori_loop(0, 32, bisect, jnp.zeros((nq, 1), jnp.int32))
        thr = jnp.maximum(ans_u ^ _INT_MIN, _INT_MIN + 1)
        thr_out[...] = jnp.broadcast_to(thr, (nq, PAGE_SIZE))


def _sample_attn_body(pt_ref, q_ref, keys_ref, knew_ref, thr_ref, kn_ref, vn_ref, *rest, nq, n_steps):
    del pt_ref
    kp = rest[:PAGES_PER_STEP]
    vp = rest[PAGES_PER_STEP:2 * PAGES_PER_STEP]
    o_ref, m_ref, l_ref, acc_ref = rest[2 * PAGES_PER_STEP:]
    f32, bf16 = jnp.float32, jnp.bfloat16
    j = pl.program_id(1)
    scale = HEAD_DIM ** -0.5
    thr = thr_ref[...]

    def update(keys_q, thr_q, k_blocks, v_blocks):
        sel = jnp.concatenate([keys_q] * GROUP, axis=0) >= jnp.concatenate([thr_q] * GROUP, axis=0)
        for n in range(KV_HEADS):
            qn = q_ref[n]
            s = jnp.concatenate(
                [lax.dot_general(qn, kb[:, n, :].astype(bf16), _NT, preferred_element_type=f32)
                 for kb in k_blocks], axis=1) * scale
            s = jnp.where(sel, s, _NEG_BIG)
            m_old = m_ref[n]
            m_new = jnp.maximum(m_old, s.max(axis=1, keepdims=True))
            p = jnp.where(sel, jnp.exp(s - m_new), 0.0)
            alpha = jnp.exp(m_old - m_new)
            l_ref[n] = alpha * l_ref[n] + p.sum(axis=1, keepdims=True)
            pb = p.astype(bf16)
            pv = jnp.zeros((GROUP * nq, HEAD_DIM), f32)
            for c, vb in enumerate(v_blocks):
                pv = pv + jnp.dot(pb[:, c * PAGE_SIZE:(c + 1) * PAGE_SIZE], vb[:, n, :].astype(bf16),
                                  preferred_element_type=f32)
            acc_ref[n] = alpha * acc_ref[n] + pv
            m_ref[n] = m_new

    @pl.when(j == 0)
    def _():
        m_ref[...] = jnp.full(m_ref.shape, _NEG_BIG, f32)
        l_ref[...] = jnp.zeros(l_ref.shape, f32)
        acc_ref[...] = jnp.zeros(acc_ref.shape, f32)
        update(knew_ref[...], thr, [kn_ref], [vn_ref])

    thr_w = jnp.concatenate([thr] * PAGES_PER_STEP, axis=1)
    update(keys_ref[...], thr_w, kp, vp)

    @pl.when(j == n_steps - 1)
    def _():
        for n in range(KV_HEADS):
            o_ref[n] = acc_ref[n] / l_ref[n]


def _dsa_sample_pallas(q, k, v, qi, ki, wi, cache_k, cache_v, cache_idx_k, page_table, interpret=False):
    B, T = q.shape[:2]
    n_pages = page_table.shape[1]
    past = n_pages * PAGE_SIZE
    ktop = min(INDEX_TOPK, (past + T) // 4)
    n_steps = n_pages // PAGES_PER_STEP
    step_w = PAGES_PER_STEP * PAGE_SIZE
    f32, bf16 = jnp.float32, jnp.bfloat16
    kvw = KV_HEADS * HEAD_DIM
    qi_s = jnp.transpose(qi.astype(bf16), (0, 2, 1, 3)).reshape(B, IDX_HEADS * T, IDX_DIM)
    w_s = jnp.transpose(wi.astype(f32), (0, 2, 1)).reshape(B, IDX_HEADS * T, 1)
    pad_rows = lambda a: jnp.pad(a, ((0, 0), (0, PAGE_SIZE - T)) + ((0, 0),) * (a.ndim - 2))
    ki_new = pad_rows(ki.astype(f32))
    k_new = pad_rows(k)
    v_new = pad_rows(v)
    q_s = jnp.transpose(q.astype(bf16).reshape(B, T, KV_HEADS, GROUP, HEAD_DIM),
                        (0, 2, 3, 1, 4)).reshape(B, KV_HEADS, GROUP * T, HEAD_DIM)
    ck, cv = cache_k, cache_v
    per_b4 = lambda b, j, pt: (b, 0, 0, 0)

    def page_map(r):
        return lambda b, j, pt: (pt[b, j * PAGES_PER_STEP + r], 0, 0)

    per_b3 = lambda b, j, pt: (b, 0, 0)
    idx_pages = [pl.BlockSpec((None, PAGE_SIZE, IDX_DIM), page_map(r)) for r in range(PAGES_PER_STEP)]
    keys, knew, thr = pl.pallas_call(
        functools.partial(_sample_index_body, ktop=ktop, nq=T, n_steps=n_steps),
        grid_spec=pltpu.PrefetchScalarGridSpec(
            num_scalar_prefetch=1,
            grid=(B, n_steps),
            in_specs=[pl.BlockSpec((None, IDX_HEADS * T, IDX_DIM), per_b3),
                      pl.BlockSpec((None, IDX_HEADS * T, 1), per_b3),
                      pl.BlockSpec((None, PAGE_SIZE, IDX_DIM), per_b3)] + idx_pages,
            out_specs=[pl.BlockSpec((None, T, step_w), lambda b, j, pt: (b, 0, j)),
                       pl.BlockSpec((None, T, PAGE_SIZE), per_b3),
                       pl.BlockSpec((None, T, PAGE_SIZE), per_b3)],
            scratch_shapes=[pltpu.VMEM((n_steps + 1, T, step_w), jnp.int32)]),
        out_shape=[jax.ShapeDtypeStruct((B, T, past), jnp.int32),
                   jax.ShapeDtypeStruct((B, T, PAGE_SIZE), jnp.int32),
                   jax.ShapeDtypeStruct((B, T, PAGE_SIZE), jnp.int32)],
        compiler_params=pltpu.CompilerParams(dimension_semantics=("arbitrary", "arbitrary")),
        name="sample_index",
        interpret=interpret,
    )(page_table, qi_s, w_s, ki_new, *([cache_idx_k] * PAGES_PER_STEP))

    def kv_page_map(r):
        return lambda b, j, pt: (pt[b, j * PAGES_PER_STEP + r], 0, 0, 0)

    kv_pages = [pl.BlockSpec((None, PAGE_SIZE, KV_HEADS, HEAD_DIM), kv_page_map(r))
                for r in range(PAGES_PER_STEP)]
    o = pl.pallas_call(
        functools.partial(_sample_attn_body, nq=T, n_steps=n_steps),
        grid_spec=pltpu.PrefetchScalarGridSpec(
            num_scalar_prefetch=1,
            grid=(B, n_steps),
            in_specs=[pl.BlockSpec((None, KV_HEADS, GROUP * T, HEAD_DIM), lambda b, j, pt: (b, 0, 0, 0)),
                      pl.BlockSpec((None, T, step_w), lambda b, j, pt: (b, 0, j)),
                      pl.BlockSpec((None, T, PAGE_SIZE), per_b3),
                      pl.BlockSpec((None, T, PAGE_SIZE), per_b3),
                      pl.BlockSpec((None, PAGE_SIZE, KV_HEADS, HEAD_DIM), per_b4),
                      pl.BlockSpec((None, PAGE_SIZE, KV_HEADS, HEAD_DIM), per_b4)] + kv_pages + kv_pages,
            out_specs=pl.BlockSpec((None, KV_HEADS, GROUP * T, HEAD_DIM), lambda b, j, pt: (b, 0, 0, 0)),
            scratch_shapes=[pltpu.VMEM((KV_HEADS, GROUP * T, 1), f32),
                            pltpu.VMEM((KV_HEADS, GROUP * T, 1), f32),
                            pltpu.VMEM((KV_HEADS, GROUP * T, HEAD_DIM), f32)]),
        out_shape=jax.ShapeDtypeStruct((B, KV_HEADS, GROUP * T, HEAD_DIM), f32),
        compiler_params=pltpu.CompilerParams(dimension_semantics=("arbitrary", "arbitrary")),
        name="sample_attn",
        interpret=interpret,
    )(page_table, q_s, keys, knew, thr, k_new, v_new, *([ck] * PAGES_PER_STEP), *([cv] * PAGES_PER_STEP))
    o = o.reshape(B, KV_HEADS, GROUP, T, HEAD_DIM)
    return jnp.transpose(o, (0, 3, 1, 2, 4)).reshape(B, T, ATTN_HEADS * HEAD_DIM)


def _dsa_sample(q, k, v, qi, ki, wi, cache_k, cache_v, cache_idx_k, page_table):
    B, T = q.shape[:2]
    past = page_table.shape[1] * PAGE_SIZE
    L = past + T
    ktop = min(INDEX_TOPK, L // 4)
    ki_past = cache_idx_k[page_table].reshape(B, past, IDX_DIM).astype(ki.dtype)
    ki_all = jnp.concatenate([ki_past, ki], axis=1)
    tpos = past + jnp.arange(T)
    sc = _index_scores(qi, wi, ki_all)
    sc = jnp.where(jnp.arange(L)[None, None, :] <= tpos[None, :, None], sc, -jnp.inf)
    _, idx = lax.top_k(sc, ktop)
    valid = idx <= tpos[None, :, None]
    bidx = jnp.arange(B)[:, None, None]
    in_past = (idx < past)[..., None, None]
    pidx = jnp.minimum(idx, past - 1)
    phys = page_table[bidx, pidx // PAGE_SIZE]
    off = pidx % PAGE_SIZE
    nidx = jnp.clip(idx - past, 0, T - 1)
    kg = jnp.where(in_past, cache_k[phys, off].astype(k.dtype), k[bidx, nidx])
    vg = jnp.where(in_past, cache_v[phys, off].astype(v.dtype), v[bidx, nidx])
    return _gathered_attention(q, kg, vg, valid)


def _causal_conv(x, buf, conv_w):
    T = x.shape[1]
    xp = jnp.concatenate([buf.astype(x.dtype), x], axis=1)
    y = xp[:, 0:T] * conv_w[0]
    for j in range(1, CONV_W):
        y = y + xp[:, j:j + T] * conv_w[j]
    return jax.nn.silu(y), xp[:, T:]


def _chunk_gated_delta(q, k, v, g, beta, S0):
    B, T, H, DK = q.shape
    DV = v.shape[-1]
    C = min(GDN_CHUNK, T)
    n = -(-T // C)
    pad = n * C - T

    def prep(a):
        a = jnp.pad(a, [(0, 0), (0, pad)] + [(0, 0)] * (a.ndim - 2))
        a = a.reshape((B, n, C) + a.shape[2:])
        return jnp.swapaxes(jnp.moveaxis(a, 1, 0), 2, 3)

    q, k, v, g, beta = prep(q), prep(k), prep(v), prep(g), prep(beta)
    gc = jnp.cumsum(g, axis=-1)
    causal = jnp.tril(jnp.ones((C, C), bool))
    strict = jnp.tril(jnp.ones((C, C), bool), -1)
    decay = jnp.exp(jnp.where(causal, gc[..., :, None] - gc[..., None, :], -jnp.inf))
    kb = k * beta[..., None]
    lower = jnp.where(strict, jnp.einsum('...id,...jd->...ij', kb, k) * decay, 0.0)
    amat = lower + jnp.eye(C, dtype=jnp.float32)
    rhs = jnp.concatenate([v * beta[..., None], kb * jnp.exp(gc)[..., None]], axis=-1)
    sol = lax.linalg.triangular_solve(amat, rhs, left_side=True, lower=True, unit_diagonal=True)
    u, w = sol[..., :DV], sol[..., DV:]
    qk = jnp.einsum('...id,...jd->...ij', q, k) * decay
    qg = q * jnp.exp(gc)[..., None]
    kd = k * jnp.exp(gc[..., -1:] - gc)[..., None]
    g_last = jnp.exp(gc[..., -1])

    def step(S, xs):
        qg_i, kd_i, u_i, w_i, qk_i, gl_i = xs
        v_new = u_i - jnp.einsum('bhck,bhkv->bhcv', w_i, S)
        o = jnp.einsum('bhck,bhkv->bhcv', qg_i, S) + jnp.einsum('bhcj,bhjv->bhcv', qk_i, v_new)
        S = S * gl_i[..., None, None] + jnp.einsum('bhck,bhcv->bhkv', kd_i, v_new)
        return S, o

    S, o = lax.scan(step, S0, (qg, kd, u, w, qk, g_last))
    o = jnp.moveaxis(jnp.swapaxes(o, 2, 3), 0, 1).reshape(B, n * C, H, DV)[:, :T]
    return o, S


def _gated_deltanet(qkv, b_raw, a_raw, gate, conv_buf, S0, conv_w, a_log, dt_bias, gdn_norm_w):
    B, T, _ = qkv.shape
    f32 = jnp.float32
    conv, new_buf = _causal_conv(qkv, conv_buf, conv_w)
    qc, kc, vc = jnp.split(conv, [GDN_HEADS * GDN_DK, 2 * GDN_HEADS * GDN_DK], axis=-1)
    q = _l2norm(qc.reshape(B, T, GDN_HEADS, GDN_DK).astype(f32)) * (GDN_DK ** -0.5)
    k = _l2norm(kc.reshape(B, T, GDN_HEADS, GDN_DK).astype(f32))
    v = vc.reshape(B, T, GDN_HEADS, GDN_DV).astype(f32)
    beta = jax.nn.sigmoid(b_raw.astype(f32))
    g = -jnp.exp(a_log.astype(f32)) * jax.nn.softplus(a_raw.astype(f32) + dt_bias.astype(f32))
    o, S = _chunk_gated_delta(q, k, v, g, beta, S0)
    o = _rmsnorm(o, gdn_norm_w) * jax.nn.silu(gate.reshape(B, T, GDN_HEADS, GDN_DV).astype(f32))
    return o.reshape(B, T, GDN_HEADS * GDN_DV).astype(qkv.dtype), S, new_buf


def _peer(xn, peer_wq, peer_keys, peer_u, peer_v):
    shape = xn.shape
    xf = xn.reshape(-1, D_MODEL)
    N = xf.shape[0]
    nb = -(-N // PEER_BLOCK)
    xb = jnp.pad(xf, ((0, nb * PEER_BLOCK - N), (0, 0))).reshape(nb, PEER_BLOCK, D_MODEL)
    ncand = PEER_TOPK * PEER_TOPK

    def blk(x):
        qh = (x @ peer_wq).reshape(PEER_BLOCK, PEER_HEADS, 2, PEER_QDIM // 2)
        s1 = jnp.einsum('thd,hkd->thk', qh[:, :, 0], peer_keys[0]).astype(jnp.float32)
        s2 = jnp.einsum('thd,hkd->thk', qh[:, :, 1], peer_keys[1]).astype(jnp.float32)
        v1, i1 = lax.top_k(s1, PEER_TOPK)
        v2, i2 = lax.top_k(s2, PEER_TOPK)
        cand = (v1[..., :, None] + v2[..., None, :]).reshape(PEER_BLOCK, PEER_HEADS, ncand)
        cidx = (i1[..., :, None] * PEER_NKEYS + i2[..., None, :]).reshape(PEER_BLOCK, PEER_HEADS, ncand)
        sv, si = lax.top_k(cand, PEER_TOPK)
        eidx = jnp.take_along_axis(cidx, si, axis=-1)
        gsm = jax.nn.softmax(sv, axis=-1)
        act = jax.nn.gelu(jnp.einsum('thkd,td->thk', peer_u[eidx], x).astype(jnp.float32), approximate=False)
        return jnp.einsum('thk,thkd->td', (gsm * act).astype(x.dtype), peer_v[eidx])

    y = lax.map(blk, xb).reshape(nb * PEER_BLOCK, D_MODEL)[:N]
    return y.reshape(shape)


GDN_ROWS = 512
_TN = (((0,), (0,)), ((), ()))


def _mm(a, b):
    return jnp.dot(a.astype(jnp.bfloat16), b.astype(jnp.bfloat16), preferred_element_type=jnp.float32)


def _split2(a):
    hi = a.astype(jnp.bfloat16)
    lo = (a - hi.astype(jnp.float32)).astype(jnp.bfloat16)
    return hi, lo


def _mm3(a, b):
    f32 = jnp.float32
    a1, a2 = _split2(a)
    b1, b2 = _split2(b)
    return (jnp.dot(a1, b1, preferred_element_type=f32) + jnp.dot(a1, b2, preferred_element_type=f32)
            + jnp.dot(a2, b1, preferred_element_type=f32))


def _split3(a):
    f32 = jnp.float32
    p1 = a.astype(jnp.bfloat16)
    r = a - p1.astype(f32)
    p2 = r.astype(jnp.bfloat16)
    p3 = (r - p2.astype(f32)).astype(jnp.bfloat16)
    return p1, p2, p3


def _sigmoid(x):
    return 1.0 / (1.0 + jnp.exp(-x))


def _softplus(x):
    return jnp.maximum(x, 0.0) + jnp.log1p(jnp.exp(-jnp.abs(x)))


def _gdn_body(x_ref, araw_ref, braw_ref, arawT_ref, gate_ref, buf_ref, s0_ref, cw_ref,
              alog_ref, dtb_ref, alogT_ref, dtbT_ref, nw_ref,
              ob_ref, sout_ref, xp_scr, y_scr, s_scr, *, rb, t_valid, t_pad):
    f32, bf16 = jnp.float32, jnp.bfloat16
    C = GDN_CHUNK
    j = pl.program_id(1)

    @pl.when(j == 0)
    def _():
        xp_scr[5:8, :] = buf_ref[...]
        s_scr[...] = s0_ref[...]

    xp_scr[8:8 + rb, :] = x_ref[...]
    y = xp_scr[5:5 + rb, :] * cw_ref[0:1, :]
    for t in range(1, CONV_W):
        y = y + xp_scr[5 + t:5 + t + rb, :] * cw_ref[t:t + 1, :]
    y_scr[...] = y * _sigmoid(y)
    xp_scr[5:8, :] = xp_scr[rb + 5:rb + 8, :]

    ri = lax.broadcasted_iota(jnp.int32, (C, C), 0)
    ci = lax.broadcasted_iota(jnp.int32, (C, C), 1)
    tri_incl = jnp.where(ri >= ci, 1.0, 0.0).astype(bf16)
    tri_inclT = jnp.where(ci >= ri, 1.0, 0.0).astype(bf16)
    causal = ri >= ci
    strict = ri > ci
    eye = jnp.where(ri == ci, 1.0, 0.0)
    neg_a = -jnp.exp(alog_ref[...])
    neg_aT = -jnp.exp(alogT_ref[...])
    q_scale = GDN_DK ** -0.5

    def chunk(c, carry):
        r0 = pl.multiple_of(c * C, C)
        g_c = neg_a * _softplus(araw_ref[pl.ds(r0, C), :] + dtb_ref[...])
        beta_c = _sigmoid(braw_ref[pl.ds(r0, C), :])
        g_r = neg_aT * _softplus(arawT_ref[c] + dtbT_ref[...])
        if t_valid < t_pad:
            base = j * rb + r0
            row_ok = (base + lax.broadcasted_iota(jnp.int32, (C, GDN_HEADS), 0)) < t_valid
            col_ok = (base + lax.broadcasted_iota(jnp.int32, (GDN_HEADS, C), 1)) < t_valid
            g_c = jnp.where(row_ok, g_c, 0.0)
            beta_c = jnp.where(row_ok, beta_c, 0.0)
            g_r = jnp.where(col_ok, g_r, 0.0)
        gc_c = sum(jnp.dot(tri_incl, p, preferred_element_type=f32) for p in _split3(g_c))
        gc_r = sum(jnp.dot(p, tri_inclT, preferred_element_type=f32) for p in _split3(g_r))
        H = range(GDN_HEADS)
        q, k, v, gcol, glast, beta, decay, eg = [], [], [], [], [], [], [], []
        for h in H:
            lo = h * GDN_DK
            qh = y_scr[pl.ds(r0, C), lo:lo + GDN_DK]
            kh = y_scr[pl.ds(r0, C), GDN_HEADS * GDN_DK + lo:GDN_HEADS * GDN_DK + lo + GDN_DK]
            v.append(y_scr[pl.ds(r0, C),
                           2 * GDN_HEADS * GDN_DK + h * GDN_DV:2 * GDN_HEADS * GDN_DK + (h + 1) * GDN_DV])
            q.append(qh * lax.rsqrt(jnp.sum(qh * qh, axis=-1, keepdims=True) + NORM_EPS) * q_scale)
            k.append(kh * lax.rsqrt(jnp.sum(kh * kh, axis=-1, keepdims=True) + NORM_EPS))
            gcol.append(gc_c[:, h:h + 1])
            glast.append(gc_c[C - 1:C, h:h + 1])
            beta.append(beta_c[:, h:h + 1])
            decay.append(jnp.exp(jnp.where(causal, gcol[h] - gc_r[h:h + 1, :], -jnp.inf)))
            eg.append(jnp.exp(gcol[h]))
        kb = [k[h] * beta[h] for h in H]
        kk = [lax.dot_general(kb[h].astype(bf16), k[h].astype(bf16), _NT, preferred_element_type=f32) for h in H]
        qk = [lax.dot_general(q[h].astype(bf16), k[h].astype(bf16), _NT, preferred_element_type=f32) * decay[h]
              for h in H]
        pw = [jnp.where(strict, -(kk[h] * decay[h]), 0.0) for h in H]
        inv = [eye + pw[h] for h in H]
        for _ in range(5):
            pw = [_mm3(pw[h], pw[h]) for h in H]
            inv = [inv[h] + _mm3(inv[h], pw[h]) for h in H]
        sol = [_mm3(inv[h], jnp.concatenate([v[h] * beta[h], kb[h] * eg[h]], axis=1)) for h in H]
        s_old = [s_scr[h] for h in H]
        v_new = [sol[h][:, :GDN_DV] - _mm(sol[h][:, GDN_DV:], s_old[h]) for h in H]
        o = [_mm(q[h] * eg[h], s_old[h]) + _mm(qk[h], v_new[h]) for h in H]
        for h in H:
            kd = k[h] * jnp.exp(glast[h] - gcol[h])
            s_scr[h] = s_old[h] * jnp.exp(glast[h]) + lax.dot_general(
                kd.astype(bf16), v_new[h].astype(bf16), _TN, preferred_element_type=f32)
        for h in H:
            on = o[h] * lax.rsqrt(jnp.mean(o[h] * o[h], axis=-1, keepdims=True) + NORM_EPS) * nw_ref[...]
            gt = gate_ref[pl.ds(r0, C), h * GDN_DV:(h + 1) * GDN_DV]
            ob_ref[pl.ds(r0, C), h * GDN_DV:(h + 1) * GDN_DV] = on * (gt * _sigmoid(gt))
        return carry

    lax.fori_loop(0, rb // C, chunk, 0)

    @pl.when(j == pl.num_programs(1) - 1)
    def _():
        sout_ref[...] = s_scr[...]


def _gdn_pallas(qkv, b_raw, a_raw, gate, conv_buf, S0, conv_w, a_log, dt_bias, gdn_norm_w, interpret=False):
    B, T, _ = qkv.shape
    assert T >= CONV_W - 1
    f32 = jnp.float32
    C = GDN_CHUNK
    t_pad = -(-T // C) * C
    rb = min(GDN_ROWS, t_pad)
    assert t_pad % rb == 0
    pad = lambda a: jnp.pad(a, ((0, 0), (0, t_pad - T), (0, 0)))
    x, a_p, b_p, gate_p = pad(qkv), pad(a_raw), pad(b_raw), pad(gate)
    a_t = jnp.transpose(a_p.reshape(B, t_pad // C, C, GDN_HEADS), (0, 1, 3, 2))
    hd = GDN_HEADS * GDN_DV
    row_blk = lambda b, j: (b, j, 0)
    fix2 = lambda b, j: (0, 0)
    ob, s_out = pl.pallas_call(
        functools.partial(_gdn_body, rb=rb, t_valid=T, t_pad=t_pad),
        grid=(B, t_pad // rb),
        in_specs=[pl.BlockSpec((None, rb, CONV_DIM), row_blk),
                  pl.BlockSpec((None, rb, GDN_HEADS), row_blk),
                  pl.BlockSpec((None, rb, GDN_HEADS), row_blk),
                  pl.BlockSpec((None, rb // C, GDN_HEADS, C), lambda b, j: (b, j, 0, 0)),
                  pl.BlockSpec((None, rb, hd), row_blk),
                  pl.BlockSpec((None, CONV_W - 1, CONV_DIM), lambda b, j: (b, 0, 0)),
                  pl.BlockSpec((None, GDN_HEADS, GDN_DK, GDN_DV), lambda b, j: (b, 0, 0, 0)),
                  pl.BlockSpec((CONV_W, CONV_DIM), fix2),
                  pl.BlockSpec((1, GDN_HEADS), fix2),
                  pl.BlockSpec((1, GDN_HEADS), fix2),
                  pl.BlockSpec((GDN_HEADS, 1), fix2),
                  pl.BlockSpec((GDN_HEADS, 1), fix2),
                  pl.BlockSpec((1, GDN_DV), fix2)],
        out_specs=[pl.BlockSpec((None, rb, hd), row_blk),
                   pl.BlockSpec((None, GDN_HEADS, GDN_DK, GDN_DV), lambda b, j: (b, 0, 0, 0))],
        out_shape=[jax.ShapeDtypeStruct((B, t_pad, hd), f32),
                   jax.ShapeDtypeStruct((B, GDN_HEADS, GDN_DK, GDN_DV), f32)],
        scratch_shapes=[pltpu.VMEM((rb + 8, CONV_DIM), f32),
                        pltpu.VMEM((rb, CONV_DIM), f32),
                        pltpu.VMEM((GDN_HEADS, GDN_DK, GDN_DV), f32)],
        compiler_params=pltpu.CompilerParams(
            dimension_semantics=("arbitrary", "arbitrary"),
            vmem_limit_bytes=56 * 1024 * 1024),
        name="gated_deltanet",
        interpret=interpret,
    )(x, a_p, b_p, a_t, gate_p, conv_buf, S0, conv_w,
      a_log.reshape(1, GDN_HEADS), dt_bias.reshape(1, GDN_HEADS),
      a_log.reshape(GDN_HEADS, 1), dt_bias.reshape(GDN_HEADS, 1), gdn_norm_w.reshape(1, GDN_DV))
    return ob[:, :T], s_out, qkv[:, T - (CONV_W - 1):]


_SQRT_HALF = 0.7071067811865476


def _top_rows(x, k):
    R, n = x.shape
    ri = lax.broadcasted_iota(jnp.int32, (R, n), 0).astype(jnp.float32)
    ki = lax.broadcasted_iota(jnp.int32, (k, n), 0)

    def body(r, c):
        x, out = c
        m = x.max(axis=0, keepdims=True)
        first = jnp.min(jnp.where(x == m, ri, float(R)), axis=0, keepdims=True)
        x = jnp.where(ri == first, -jnp.inf, x)
        out = jnp.where(ki == r, m, out)
        return x, out

    _, out = lax.fori_loop(0, k, body, (x, jnp.zeros((k, n), jnp.float32)))
    return out


def _peer_front_body(x_ref, oa_ref, ob_ref, wo_ref, g_ref, wq_ref, keys_ref,
                     h_ref, xn_ref, s1_ref, s2_ref, e2_ref, aux_ref):
    f32, bf16 = jnp.float32, jnp.bfloat16
    half_w = ATTN_HEADS * HEAD_DIM
    h = (x_ref[...]
         + jnp.dot(oa_ref[...].astype(bf16), wo_ref[:half_w, :], preferred_element_type=f32)
         + jnp.dot(ob_ref[...].astype(bf16), wo_ref[half_w:, :], preferred_element_type=f32))
    h_ref[...] = h
    xn = (h * lax.rsqrt(jnp.mean(h * h, axis=-1, keepdims=True) + NORM_EPS) * g_ref[...]).astype(bf16)
    xn_ref[...] = xn
    qh = jnp.dot(xn, wq_ref[...], preferred_element_type=f32).astype(bf16)
    tq = qh.shape[0]
    hq = PEER_QDIM // 2
    for hh in range(PEER_HEADS):
        tops = []
        for half in range(2):
            col = (hh * 2 + half) * hq
            sT = lax.dot_general(keys_ref[half, hh], qh[:, col:col + hq], _NT,
                                 preferred_element_type=f32)
            (s1_ref if half == 0 else s2_ref)[hh] = sT
            tops.append(_top_rows(sT, PEER_TOPK))
        a16, b16 = tops
        cand = jnp.concatenate([a16[r:r + 1, :] + b16 for r in range(PEER_TOPK)], axis=0)
        tau = _top_rows(cand, PEER_TOPK)[PEER_TOPK - 1:PEER_TOPK, :]
        top_sum = a16[0:1, :] + b16[0:1, :]
        z = jnp.sum(jnp.where(cand >= tau, jnp.exp(cand - top_sum), 0.0), axis=0, keepdims=True)
        e2_ref[hh] = jnp.exp(s2_ref[hh] - b16[0:1, :]) / z
        aux_ref[hh] = jnp.concatenate([tau, a16[0:1, :], jnp.zeros((6, tq), f32)], axis=0)


def _peer_front(x, oa, ob, wo, g, wq, keys, tq, interpret=False):
    n = x.shape[0]
    half_w = ATTN_HEADS * HEAD_DIM
    tok = lambda i: (i, 0)
    fix2 = lambda i: (0, 0)
    colT = lambda i: (0, 0, i)
    f32 = jnp.float32
    plane = jax.ShapeDtypeStruct((PEER_HEADS, PEER_NKEYS, n), f32)
    plane_spec = pl.BlockSpec((PEER_HEADS, PEER_NKEYS, tq), colT)
    return pl.pallas_call(
        _peer_front_body,
        grid=(n // tq,),
        in_specs=[pl.BlockSpec((tq, D_MODEL), tok),
                  pl.BlockSpec((tq, half_w), tok),
                  pl.BlockSpec((tq, half_w), tok),
                  pl.BlockSpec((D_MODEL, D_MODEL), fix2),
                  pl.BlockSpec((1, D_MODEL), fix2),
                  pl.BlockSpec((D_MODEL, PEER_HEADS * PEER_QDIM), fix2),
                  pl.BlockSpec((2, PEER_HEADS, PEER_NKEYS, PEER_QDIM // 2), lambda i: (0, 0, 0, 0))],
        out_specs=[pl.BlockSpec((tq, D_MODEL), tok),
                   pl.BlockSpec((tq, D_MODEL), tok),
                   plane_spec, plane_spec, plane_spec,
                   pl.BlockSpec((PEER_HEADS, 8, tq), colT)],
        out_shape=[jax.ShapeDtypeStruct((n, D_MODEL), f32),
                   jax.ShapeDtypeStruct((n, D_MODEL), jnp.bfloat16),
                   plane, plane, plane,
                   jax.ShapeDtypeStruct((PEER_HEADS, 8, n), f32)],
        compiler_params=pltpu.CompilerParams(
            dimension_semantics=("arbitrary",),
            vmem_limit_bytes=56 * 1024 * 1024),
        name="peer_front",
        interpret=interpret,
    )(x, oa, ob, wo, g.reshape(1, D_MODEL), wq, keys)


def _peer_dense_body(xn_ref, u_ref, vT_ref, s1_ref, s2_ref, e2_ref, aux_ref, yT_ref, *, eblk):
    f32 = jnp.float32
    eb = pl.program_id(1)

    @pl.when(eb == 0)
    def _():
        yT_ref[...] = jnp.zeros(yT_ref.shape, f32)

    a = lax.dot_general(u_ref[...], xn_ref[...], _NT, preferred_element_type=f32)
    act = 0.5 * a * (1.0 + lax.erf(a * _SQRT_HALF))
    sub = eblk // PEER_NKEYS
    pieces = []
    for r in range(sub):
        i1 = eb * sub + r
        gate = jnp.zeros((PEER_NKEYS, a.shape[1]), f32)
        for hh in range(PEER_HEADS):
            s1row = s1_ref[hh, pl.ds(i1, 1), :]
            tau = aux_ref[hh, 0:1, :]
            e1row = jnp.exp(s1row - aux_ref[hh, 1:2, :])
            gate = gate + jnp.where(s1row + s2_ref[hh] >= tau, e1row * e2_ref[hh], 0.0)
        pieces.append((gate * act[r * PEER_NKEYS:(r + 1) * PEER_NKEYS, :]).astype(jnp.bfloat16))
    hT = jnp.concatenate(pieces, axis=0)
    yT_ref[...] += jnp.dot(vT_ref[...], hT, preferred_element_type=f32)


def _peer_dense(xn, u, vT, s1, s2, e2, aux, tq, eblk, interpret=False):
    n = xn.shape[0]
    ne = u.shape[0]
    plane_spec = pl.BlockSpec((PEER_HEADS, PEER_NKEYS, tq), lambda i, e: (0, 0, i))
    return pl.pallas_call(
        functools.partial(_peer_dense_body, eblk=eblk),
        grid=(n // tq, ne // eblk),
        in_specs=[pl.BlockSpec((tq, D_MODEL), lambda i, e: (i, 0)),
                  pl.BlockSpec((eblk, D_MODEL), lambda i, e: (e, 0)),
                  pl.BlockSpec((D_MODEL, eblk), lambda i, e: (0, e)),
                  plane_spec, plane_spec, plane_spec,
                  pl.BlockSpec((PEER_HEADS, 8, tq), lambda i, e: (0, 0, i))],
        out_specs=pl.BlockSpec((D_MODEL, tq), lambda i, e: (0, i)),
        out_shape=jax.ShapeDtypeStruct((D_MODEL, n), jnp.float32),
        compiler_params=pltpu.CompilerParams(
            dimension_semantics=("arbitrary", "arbitrary"),
            vmem_limit_bytes=56 * 1024 * 1024),
        name="peer_dense",
        interpret=interpret,
    )(xn, u, vT, s1, s2, e2, aux)


def _layer_out_pallas(x, oa, ob, wo_b, ffn_norm_w, wq_b, keys_b, u_b, vT_b, interpret=False):
    n = x.shape[0]
    tq1 = min(256, n)
    tq2 = 512 if n % 512 == 0 else min(256, n)
    h, xn, s1, s2, e2, aux = _peer_front(x, oa, ob, wo_b, ffn_norm_w, wq_b, keys_b, tq1, interpret)
    yT = _peer_dense(xn, u_b, vT_b, s1, s2, e2, aux, tq2, 512, interpret)
    return h + yT.T


def _layer_out(x, oa, ob, w_out, ffn_norm_w, peer_wq, peer_keys, peer_u, peer_v):
    h = x + jnp.concatenate([oa, ob], axis=-1) @ w_out
    return h + _peer(_rmsnorm(h, ffn_norm_w), peer_wq, peer_keys, peer_u, peer_v)


def kernel(x_prompt, x_sample, cache_k, cache_v, cache_idx_k, state_ssm, state_conv, page_table,
           attn_norm_w, w_in, q_norm_w, k_norm_w, idx_k_norm_w, conv_w, a_log, dt_bias, gdn_norm_w,
           w_out, ffn_norm_w, peer_wq, peer_keys, peer_u, peer_v):
    l = 0
    n_pad = -(-IN_COLS // 1024) * 1024
    w_in_p = jnp.pad(w_in[l], ((0, 0), (0, n_pad - IN_COLS))).astype(jnp.bfloat16)
    proj_w = (attn_norm_w[l], w_in_p, q_norm_w[l], k_norm_w[l], idx_k_norm_w[l])
    gdn_w = (conv_w[l], a_log[l], dt_bias[l], gdn_norm_w[l])
    bf16 = jnp.bfloat16
    out_w = (w_out[l].astype(bf16), ffn_norm_w[l], peer_wq[l].astype(bf16), peer_keys[l].astype(bf16),
             peer_u[l].astype(bf16), peer_v[l].astype(bf16).T)

    hp, hs = x_prompt, x_sample
    (qa, ka, va, qi, ki, wi), gdn_in = _in_projection(hp, *proj_w)
    oa = _dsa_prompt_pallas(qa[0], ka[0], va[0], qi[0], ki[0], wi[0])[None]
    Bp = hp.shape[0]
    buf0 = jnp.zeros((Bp, CONV_W - 1, CONV_DIM), hp.dtype)
    S0 = jnp.zeros((Bp, GDN_HEADS, GDN_DK, GDN_DV), jnp.float32)
    ob, S_p, buf_p = _gdn_pallas(*gdn_in, buf0, S0, *gdn_w)
    half_w = ATTN_HEADS * HEAD_DIM
    hp = _layer_out_pallas(hp[0], oa[0], ob[0], *out_w)[None]
    kp, vp, ip = ka, va, ki

    (qa, ka, va, qi, ki, wi), gdn_in = _in_projection(hs, *proj_w)
    oa = _dsa_sample_pallas(qa, ka, va, qi, ki, wi, cache_k[l], cache_v[l], cache_idx_k[l], page_table)
    ob, S_s, buf_s = _gdn_pallas(*gdn_in, state_conv[l], state_ssm[l].astype(jnp.float32), *gdn_w)
    ns = hs.shape[0] * hs.shape[1]
    hs = _layer_out_pallas(hs.reshape(ns, D_MODEL), oa.reshape(ns, half_w), ob.reshape(ns, half_w),
                           *out_w).reshape(hs.shape)

    return (hp, hs, kp[None], vp[None], ip[None], S_p[None], buf_p[None],
            ka[None], va[None], ki[None], S_s[None], buf_s[None])
```

```python
import functools
import math

import jax
import jax.numpy as jnp
from jax import lax
from jax.experimental import pallas as pl
from jax.experimental.pallas import tpu as pltpu

D_MODEL = 2048
PAGE_SIZE = 128
HEAD_DIM = 128
ATTN_HEADS = 8
KV_HEADS = 2
GROUP = ATTN_HEADS // KV_HEADS
IDX_HEADS = 16
IDX_DIM = 64
INDEX_TOPK = 256
Q_BLOCK = 128
IDX_SCALE = (IDX_HEADS * IDX_DIM) ** -0.5
GDN_DK = 128
GDN_DV = 128
GDN_HEADS = 8
CONV_W = 4
CONV_DIM = GDN_HEADS * (2 * GDN_DK + GDN_DV)
GDN_CHUNK = 64
IN_SIZES = (ATTN_HEADS * HEAD_DIM, KV_HEADS * HEAD_DIM, KV_HEADS * HEAD_DIM,
            IDX_HEADS * IDX_DIM, IDX_DIM, IDX_HEADS,
            CONV_DIM, GDN_HEADS, GDN_HEADS, GDN_HEADS * GDN_DV)
IN_COLS = sum(IN_SIZES)
PEER_HEADS = 8
PEER_NKEYS = 128
PEER_QDIM = 256
PEER_TOPK = 16
PEER_BLOCK = 128
NORM_EPS = 1e-6

LANES = 128


def _rmsnorm(x, w):
    xf = x.astype(jnp.float32)
    y = xf * lax.rsqrt(jnp.mean(xf * xf, axis=-1, keepdims=True) + NORM_EPS)
    return (y * w.astype(jnp.float32)).astype(x.dtype)


def _l2norm(x):
    return x * lax.rsqrt(jnp.sum(x * x, axis=-1, keepdims=True) + NORM_EPS)


def _norm_matmul_body(x_ref, g_ref, w_ref, o_ref, xn_ref):
    @pl.when(pl.program_id(1) == 0)
    def _():
        x = x_ref[...]
        r = lax.rsqrt(jnp.mean(x * x, axis=-1, keepdims=True) + NORM_EPS)
        xn_ref[...] = (x * r * g_ref[...]).astype(jnp.bfloat16)

    o_ref[...] = jnp.dot(xn_ref[...], w_ref[...], preferred_element_type=jnp.float32)


def _norm_matmul(x, g, w, tm, tn):
    m, k = x.shape
    n = w.shape[1]
    return pl.pallas_call(
        _norm_matmul_body,
        grid=(m // tm, n // tn),
        in_specs=[pl.BlockSpec((tm, k), lambda i, j: (i, 0)),
                  pl.BlockSpec((1, k), lambda i, j: (0, 0)),
                  pl.BlockSpec((k, tn), lambda i, j: (0, j))],
        out_specs=pl.BlockSpec((tm, tn), lambda i, j: (i, j)),
        out_shape=jax.ShapeDtypeStruct((m, n), jnp.float32),
        scratch_shapes=[pltpu.VMEM((tm, k), jnp.bfloat16)],
        compiler_params=pltpu.CompilerParams(
            dimension_semantics=("arbitrary", "arbitrary"),
            vmem_limit_bytes=48 * 1024 * 1024),
        name="norm_matmul",
    )(x, g.reshape(1, k), w)


PROJ_TILE = 1024
_SRC = dict(zip(("qa", "ka", "va", "qi", "ki", "wi", "qkv", "b", "a", "gate"),
                [(sum(IN_SIZES[:i]), IN_SIZES[i]) for i in range(len(IN_SIZES))]))
_DST_ORDER = (("qkv",), ("qa",), ("qi",), ("gate",), ("ka",), ("va",), ("ki",), ("wi", "b", "a"))


def _proj_layout():
    dst, off = {}, 0
    for group in _DST_ORDER:
        for name in group:
            dst[name] = off
            off += _SRC[name][1]
        off = -(-off // LANES) * LANES
    return dst, -(-off // PROJ_TILE) * PROJ_TILE


_DST, PROJ_COLS = _proj_layout()
assert _DST["qkv"] == 0 and _DST["gate"] % (GDN_HEADS * GDN_DV) == 0


def _permute_w_in(w_in):
    pieces, off = [], 0
    for group in _DST_ORDER:
        for name in group:
            if _DST[name] > off:
                pieces.append(jnp.zeros((D_MODEL, _DST[name] - off), w_in.dtype))
            s0, n = _SRC[name]
            pieces.append(w_in[:, s0:s0 + n])
            off = _DST[name] + n
    pieces.append(jnp.zeros((D_MODEL, PROJ_COLS - off), w_in.dtype))
    return jnp.concatenate(pieces, axis=1).astype(jnp.bfloat16)


def _in_projection(x, attn_norm_w, w_in_p, q_norm_w, k_norm_w, idx_k_norm_w):
    B, T, _ = x.shape
    m = B * T
    tm = 512 if m % 512 == 0 else m
    z = _norm_matmul(x.reshape(m, D_MODEL), attn_norm_w, w_in_p, tm, PROJ_TILE).reshape(B, T, PROJ_COLS)
    col = lambda name: z[:, :, _DST[name]:_DST[name] + _SRC[name][1]]
    qa = _rmsnorm(col("qa").reshape(B, T, ATTN_HEADS, HEAD_DIM), q_norm_w)
    ka = _rmsnorm(col("ka").reshape(B, T, KV_HEADS, HEAD_DIM), k_norm_w)
    va = col("va").reshape(B, T, KV_HEADS, HEAD_DIM)
    qi = col("qi").reshape(B, T, IDX_HEADS, IDX_DIM)
    ki = _rmsnorm(col("ki"), idx_k_norm_w)
    return (qa, ka, va, qi, ki, col("wi")), (z, col("b"), col("a"))


def _index_scores(qi, wi, ki):
    s = jax.nn.relu(jnp.einsum('bqhd,bsd->bqhs', qi, ki).astype(jnp.float32))
    return jnp.einsum('bqhs,bqh->bqs', s, wi.astype(jnp.float32)) * IDX_SCALE


def _gathered_attention(q, kg, vg, valid):
    B, Q = q.shape[:2]
    qg = q.reshape(B, Q, KV_HEADS, GROUP, HEAD_DIM)
    s = jnp.einsum('bqngd,bqknd->bqngk', qg, kg).astype(jnp.float32) * (HEAD_DIM ** -0.5)
    s = jnp.where(valid[:, :, None, None, :], s, -jnp.inf)
    p = jax.nn.softmax(s, axis=-1)
    o = jnp.einsum('bqngk,bqknd->bqngd', p.astype(vg.dtype), vg)
    return o.reshape(B, Q, ATTN_HEADS * HEAD_DIM)


def _dsa_prompt(q, k, v, qi, ki, wi):
    B, T = q.shape[:2]
    ktop = min(INDEX_TOPK, T // 4)
    nb = T // Q_BLOCK
    bidx = jnp.arange(B)[:, None, None]
    spos = jnp.arange(T)

    def blocks(a):
        return jnp.moveaxis(a.reshape((B, nb, Q_BLOCK) + a.shape[2:]), 1, 0)

    def blk(xs):
        qb, qib, wib, start = xs
        tpos = start + jnp.arange(Q_BLOCK)
        sc = _index_scores(qib, wib, ki)
        sc = jnp.where(spos[None, None, :] <= tpos[None, :, None], sc, -jnp.inf)
        _, idx = lax.top_k(sc, ktop)
        valid = idx <= tpos[None, :, None]
        return _gathered_attention(qb, k[bidx, idx], v[bidx, idx], valid)

    o = lax.map(blk, (blocks(q), blocks(qi), blocks(wi), jnp.arange(nb) * Q_BLOCK))
    return jnp.moveaxis(o, 0, 1).reshape(B, T, ATTN_HEADS * HEAD_DIM)


_INT_MIN = -2 ** 31
_NEG_BIG = -1e30
_NT = (((1,), (1,)), ((), ()))


def _dsa_prompt_body(qi_ref, wT_ref, q_ref, ki_ref, k_ref, vT_ref, o_ref,
                     keys_ref, m_ref, l_ref, acc_ref, *, ktop, tq):
    f32 = jnp.float32
    i = pl.program_id(0)
    nkb = i + 1
    col_t = i * tq + lax.broadcasted_iota(jnp.int32, (tq, tq), 1)
    row_s = lax.broadcasted_iota(jnp.int32, (tq, tq), 0)

    def score_blk(kb, carry):
        kib = ki_ref[kb]
        acc = jnp.zeros((tq, tq), f32)
        for h in range(IDX_HEADS):
            s = lax.dot_general(kib, qi_ref[h], _NT, preferred_element_type=f32)
            acc = acc + jnp.maximum(s, 0.0) * wT_ref[h:h + 1, :]
        sc = acc * IDX_SCALE
        bits = lax.bitcast_convert_type(sc, jnp.int32)
        key = bits ^ (lax.shift_right_arithmetic(bits, 31) & 0x7FFFFFFF)
        valid = (kb * tq + row_s) <= col_t
        keys_ref[kb] = jnp.where(valid, key, _INT_MIN)
        return carry

    lax.fori_loop(0, nkb, score_blk, 0)

    def bisect(it, ans_u):
        cand_u = ans_u | lax.shift_left(jnp.int32(1), 31 - it)
        cand_s = cand_u ^ _INT_MIN

        def count_blk(kb, cnt):
            hit = jnp.where(keys_ref[kb] >= cand_s, 1.0, 0.0)
            return cnt + hit.reshape(tq // 8, 8, tq).sum(axis=0)

        cnt = lax.fori_loop(0, nkb, count_blk, jnp.zeros((8, tq), f32))
        cnt = cnt.sum(axis=0, keepdims=True)
        return jnp.where(cnt >= ktop, cand_u, ans_u)

    ans_u = lax.fori_loop(0, 32, bisect, jnp.zeros((1, tq), jnp.int32))
    thr = jnp.maximum(ans_u ^ _INT_MIN, _INT_MIN + 1)

    m_ref[...] = jnp.full(m_ref.shape, _NEG_BIG, f32)
    l_ref[...] = jnp.zeros(l_ref.shape, f32)
    acc_ref[...] = jnp.zeros(acc_ref.shape, f32)
    scale = HEAD_DIM ** -0.5

    def attn_blk(kb, carry):
        sel = keys_ref[kb] >= thr
        kblk = k_ref[kb]
        vT = vT_ref[kb]
        s_all = [lax.dot_general(kblk[:, (h // GROUP) * HEAD_DIM:(h // GROUP + 1) * HEAD_DIM], q_ref[h], _NT,
                                 preferred_element_type=f32) for h in range(ATTN_HEADS)]
        for h in range(ATTN_HEADS):
            n = h // GROUP
            s = jnp.where(sel, s_all[h] * scale, _NEG_BIG)
            m_old = m_ref[h]
            m_new = jnp.maximum(m_old, s.max(axis=0, keepdims=True))
            p = jnp.where(sel, jnp.exp(s - m_new), 0.0)
            alpha = jnp.exp(m_old - m_new)
            l_ref[h] = alpha * l_ref[h] + p.sum(axis=0, keepdims=True)
            pv = jnp.dot(vT[n * HEAD_DIM:(n + 1) * HEAD_DIM, :], p.astype(jnp.bfloat16),
                         preferred_element_type=f32)
            acc_ref[h] = alpha * acc_ref[h] + pv
            m_ref[h] = m_new
        return carry

    lax.fori_loop(0, nkb, attn_blk, 0)
    for h in range(ATTN_HEADS):
        o_ref[:, h * HEAD_DIM:(h + 1) * HEAD_DIM] = (acc_ref[h] / l_ref[h]).T


def _dsa_prompt_pallas(q, k, v, qi, ki, wi, interpret=False):
    T = q.shape[0]
    tq = min(256, T)
    nb = T // tq
    ktop = min(INDEX_TOPK, T // 4)
    bf16 = jnp.bfloat16
    qh = jnp.transpose(q.astype(bf16), (1, 0, 2))
    qih = jnp.transpose(qi.astype(bf16), (1, 0, 2))
    wT = wi.astype(jnp.float32).T
    kib = ki.astype(bf16).reshape(nb, tq, IDX_DIM)
    kb = k.astype(bf16).reshape(nb, tq, KV_HEADS * HEAD_DIM)
    vT = jnp.transpose(v.astype(bf16).reshape(nb, tq, KV_HEADS * HEAD_DIM), (0, 2, 1))
    body = functools.partial(_dsa_prompt_body, ktop=ktop, tq=tq)
    return pl.pallas_call(
        body,
        grid=(nb,),
        in_specs=[pl.BlockSpec((IDX_HEADS, tq, IDX_DIM), lambda i: (0, i, 0)),
                  pl.BlockSpec((IDX_HEADS, tq), lambda i: (0, i)),
                  pl.BlockSpec((ATTN_HEADS, tq, HEAD_DIM), lambda i: (0, i, 0)),
                  pl.BlockSpec((nb, tq, IDX_DIM), lambda i: (0, 0, 0)),
                  pl.BlockSpec((nb, tq, KV_HEADS * HEAD_DIM), lambda i: (0, 0, 0)),
                  pl.BlockSpec((nb, KV_HEADS * HEAD_DIM, tq), lambda i: (0, 0, 0))],
        out_specs=pl.BlockSpec((tq, ATTN_HEADS * HEAD_DIM), lambda i: (i, 0)),
        out_shape=jax.ShapeDtypeStruct((T, ATTN_HEADS * HEAD_DIM), jnp.float32),
        scratch_shapes=[pltpu.VMEM((nb, tq, tq), jnp.int32),
                        pltpu.VMEM((ATTN_HEADS, 1, tq), jnp.float32),
                        pltpu.VMEM((ATTN_HEADS, 1, tq), jnp.float32),
                        pltpu.VMEM((ATTN_HEADS, HEAD_DIM, tq), jnp.float32)],
        compiler_params=pltpu.CompilerParams(
            dimension_semantics=("arbitrary",),
            vmem_limit_bytes=56 * 1024 * 1024),
        name="dsa_prompt",
        interpret=interpret,
    )(qih, wT, qh, kib, kb, vT)


PAGES_PER_STEP = 8


def _sortable_key(x):
    bits = lax.bitcast_convert_type(x, jnp.int32)
    return bits ^ (lax.shift_right_arithmetic(bits, 31) & 0x7FFFFFFF)


def _sample_index_body(pt_ref, qi_ref, w_ref, kin_ref, *rest, ktop, nq, n_steps):
    del pt_ref
    pages = rest[:PAGES_PER_STEP]
    keys_out, knew_out, thr_out, keys_scr = rest[PAGES_PER_STEP:]
    f32 = jnp.float32
    j = pl.program_id(1)
    step_w = PAGES_PER_STEP * PAGE_SIZE
    past = n_steps * step_w
    qi = qi_ref[...]
    w = w_ref[...]

    def page_keys(page):
        s = lax.dot_general(qi, page.astype(jnp.bfloat16), _NT, preferred_element_type=f32)
        s = jnp.maximum(s, 0.0) * w
        return s.reshape(IDX_HEADS, nq, PAGE_SIZE).sum(axis=0) * IDX_SCALE

    blk = jnp.concatenate([_sortable_key(page_keys(p[...])) for p in pages], axis=1)
    keys_out[...] = blk
    keys_scr[j] = blk

    @pl.when(j == n_steps - 1)
    def _():
        kn = _sortable_key(page_keys(kin_ref[...]))
        col = lax.broadcasted_iota(jnp.int32, (nq, PAGE_SIZE), 1)
        row = lax.broadcasted_iota(jnp.int32, (nq, PAGE_SIZE), 0)
        kn = jnp.where(col <= row, kn, _INT_MIN)
        knew_out[...] = kn
        keys_scr[n_steps] = jnp.concatenate(
            [kn, jnp.full((nq, step_w - PAGE_SIZE), _INT_MIN, jnp.int32)], axis=1)

        def bisect(it, ans_u):
            cand_u = ans_u | lax.shift_left(jnp.int32(1), 31 - it)
            cand_s = cand_u ^ _INT_MIN
            cnt = jnp.zeros((nq, PAGE_SIZE), f32)
            for st in range(n_steps + 1):
                for c in range(PAGES_PER_STEP):
                    chunk = keys_scr[st, :, c * PAGE_SIZE:(c + 1) * PAGE_SIZE]
                    cnt = cnt + jnp.where(chunk >= cand_s, 1.0, 0.0)
            cnt = cnt.sum(axis=1, keepdims=True)
            return jnp.where(cnt >= ktop, cand_u, ans_u)

        ans_u = lax.fori_loop(0, 32, bisect, jnp.zeros((nq, 1), jnp.int32))
        thr = jnp.maximum(ans_u ^ _INT_MIN, _INT_MIN + 1)
        thr_out[...] = jnp.broadcast_to(thr, (nq, PAGE_SIZE))


def _sample_attn_body(pt_ref, q_ref, keys_ref, knew_ref, thr_ref, kn_ref, vn_ref, *rest, nq, n_steps):
    del pt_ref
    kp = rest[:PAGES_PER_STEP]
    vp = rest[PAGES_PER_STEP:2 * PAGES_PER_STEP]
    o_ref, m_ref, l_ref, acc_ref = rest[2 * PAGES_PER_STEP:]
    f32, bf16 = jnp.float32, jnp.bfloat16
    j = pl.program_id(1)
    scale = HEAD_DIM ** -0.5
    thr = thr_ref[...]

    def update(keys_q, thr_q, k_blocks, v_blocks):
        sel = jnp.concatenate([keys_q] * GROUP, axis=0) >= jnp.concatenate([thr_q] * GROUP, axis=0)
        for n in range(KV_HEADS):
            qn = q_ref[n]
            s = jnp.concatenate(
                [lax.dot_general(qn, kb[pl.ds(n, PAGE_SIZE, stride=KV_HEADS), :].astype(bf16), _NT,
                                 preferred_element_type=f32)
                 for kb in k_blocks], axis=1) * scale
            s = jnp.where(sel, s, _NEG_BIG)
            m_old = m_ref[n]
            m_new = jnp.maximum(m_old, s.max(axis=1, keepdims=True))
            p = jnp.where(sel, jnp.exp(s - m_new), 0.0)
            alpha = jnp.exp(m_old - m_new)
            l_ref[n] = alpha * l_ref[n] + p.sum(axis=1, keepdims=True)
            pb = p.astype(bf16)
            pv = jnp.zeros((GROUP * nq, HEAD_DIM), f32)
            for c, vb in enumerate(v_blocks):
                pv = pv + jnp.dot(pb[:, c * PAGE_SIZE:(c + 1) * PAGE_SIZE],
                                  vb[pl.ds(n, PAGE_SIZE, stride=KV_HEADS), :].astype(bf16),
                                  preferred_element_type=f32)
            acc_ref[n] = alpha * acc_ref[n] + pv
            m_ref[n] = m_new

    @pl.when(j == 0)
    def _():
        m_ref[...] = jnp.full(m_ref.shape, _NEG_BIG, f32)
        l_ref[...] = jnp.zeros(l_ref.shape, f32)
        acc_ref[...] = jnp.zeros(acc_ref.shape, f32)
        update(knew_ref[...], thr, [kn_ref], [vn_ref])

    thr_w = jnp.concatenate([thr] * PAGES_PER_STEP, axis=1)
    update(keys_ref[...], thr_w, kp, vp)

    @pl.when(j == n_steps - 1)
    def _():
        for n in range(KV_HEADS):
            o_ref[n] = acc_ref[n] / l_ref[n]


def _dsa_sample_pallas(q, k, v, qi, ki, wi, cache_k, cache_v, cache_idx_k, page_table, interpret=False):
    B, T = q.shape[:2]
    n_pages = page_table.shape[1]
    past = n_pages * PAGE_SIZE
    ktop = min(INDEX_TOPK, (past + T) // 4)
    n_steps = n_pages // PAGES_PER_STEP
    step_w = PAGES_PER_STEP * PAGE_SIZE
    f32, bf16 = jnp.float32, jnp.bfloat16
    kvw = KV_HEADS * HEAD_DIM
    qi_s = jnp.transpose(qi.astype(bf16), (0, 2, 1, 3)).reshape(B, IDX_HEADS * T, IDX_DIM)
    w_s = jnp.transpose(wi.astype(f32), (0, 2, 1)).reshape(B, IDX_HEADS * T, 1)
    pad_rows = lambda a: jnp.pad(a, ((0, 0), (0, PAGE_SIZE - T)) + ((0, 0),) * (a.ndim - 2))
    ki_new = pad_rows(ki.astype(f32))
    slot_rows = PAGE_SIZE * KV_HEADS
    k_new = pad_rows(k).reshape(B, slot_rows, HEAD_DIM)
    v_new = pad_rows(v).reshape(B, slot_rows, HEAD_DIM)
    q_s = jnp.transpose(q.astype(bf16).reshape(B, T, KV_HEADS, GROUP, HEAD_DIM),
                        (0, 2, 3, 1, 4)).reshape(B, KV_HEADS, GROUP * T, HEAD_DIM)
    ck = cache_k.reshape(cache_k.shape[0], slot_rows, HEAD_DIM)
    cv = cache_v.reshape(cache_v.shape[0], slot_rows, HEAD_DIM)

    def page_map(r):
        return lambda b, j, pt: (pt[b, j * PAGES_PER_STEP + r], 0, 0)

    per_b3 = lambda b, j, pt: (b, 0, 0)
    idx_pages = [pl.BlockSpec((None, PAGE_SIZE, IDX_DIM), page_map(r)) for r in range(PAGES_PER_STEP)]
    keys, knew, thr = pl.pallas_call(
        functools.partial(_sample_index_body, ktop=ktop, nq=T, n_steps=n_steps),
        grid_spec=pltpu.PrefetchScalarGridSpec(
            num_scalar_prefetch=1,
            grid=(B, n_steps),
            in_specs=[pl.BlockSpec((None, IDX_HEADS * T, IDX_DIM), per_b3),
                      pl.BlockSpec((None, IDX_HEADS * T, 1), per_b3),
                      pl.BlockSpec((None, PAGE_SIZE, IDX_DIM), per_b3)] + idx_pages,
            out_specs=[pl.BlockSpec((None, T, step_w), lambda b, j, pt: (b, 0, j)),
                       pl.BlockSpec((None, T, PAGE_SIZE), per_b3),
                       pl.BlockSpec((None, T, PAGE_SIZE), per_b3)],
            scratch_shapes=[pltpu.VMEM((n_steps + 1, T, step_w), jnp.int32)]),
        out_shape=[jax.ShapeDtypeStruct((B, T, past), jnp.int32),
                   jax.ShapeDtypeStruct((B, T, PAGE_SIZE), jnp.int32),
                   jax.ShapeDtypeStruct((B, T, PAGE_SIZE), jnp.int32)],
        compiler_params=pltpu.CompilerParams(dimension_semantics=("arbitrary", "arbitrary")),
        name="sample_index",
        interpret=interpret,
    )(page_table, qi_s, w_s, ki_new, *([cache_idx_k] * PAGES_PER_STEP))

    kv_pages = [pl.BlockSpec((None, slot_rows, HEAD_DIM), page_map(r)) for r in range(PAGES_PER_STEP)]
    o = pl.pallas_call(
        functools.partial(_sample_attn_body, nq=T, n_steps=n_steps),
        grid_spec=pltpu.PrefetchScalarGridSpec(
            num_scalar_prefetch=1,
            grid=(B, n_steps),
            in_specs=[pl.BlockSpec((None, KV_HEADS, GROUP * T, HEAD_DIM), lambda b, j, pt: (b, 0, 0, 0)),
                      pl.BlockSpec((None, T, step_w), lambda b, j, pt: (b, 0, j)),
                      pl.BlockSpec((None, T, PAGE_SIZE), per_b3),
                      pl.BlockSpec((None, T, PAGE_SIZE), per_b3),
                      pl.BlockSpec((None, slot_rows, HEAD_DIM), per_b3),
                      pl.BlockSpec((None, slot_rows, HEAD_DIM), per_b3)] + kv_pages + kv_pages,
            out_specs=pl.BlockSpec((None, KV_HEADS, GROUP * T, HEAD_DIM), lambda b, j, pt: (b, 0, 0, 0)),
            scratch_shapes=[pltpu.VMEM((KV_HEADS, GROUP * T, 1), f32),
                            pltpu.VMEM((KV_HEADS, GROUP * T, 1), f32),
                            pltpu.VMEM((KV_HEADS, GROUP * T, HEAD_DIM), f32)]),
        out_shape=jax.ShapeDtypeStruct((B, KV_HEADS, GROUP * T, HEAD_DIM), f32),
        compiler_params=pltpu.CompilerParams(dimension_semantics=("arbitrary", "arbitrary")),
        name="sample_attn",
        interpret=interpret,
    )(page_table, q_s, keys, knew, thr, k_new, v_new, *([ck] * PAGES_PER_STEP), *([cv] * PAGES_PER_STEP))
    o = o.reshape(B, KV_HEADS, GROUP, T, HEAD_DIM)
    return jnp.transpose(o, (0, 3, 1, 2, 4)).reshape(B, T, ATTN_HEADS * HEAD_DIM)


def _dsa_sample(q, k, v, qi, ki, wi, cache_k, cache_v, cache_idx_k, page_table):
    B, T = q.shape[:2]
    past = page_table.shape[1] * PAGE_SIZE
    L = past + T
    ktop = min(INDEX_TOPK, L // 4)
    ki_past = cache_idx_k[page_table].reshape(B, past, IDX_DIM).astype(ki.dtype)
    ki_all = jnp.concatenate([ki_past, ki], axis=1)
    tpos = past + jnp.arange(T)
    sc = _index_scores(qi, wi, ki_all)
    sc = jnp.where(jnp.arange(L)[None, None, :] <= tpos[None, :, None], sc, -jnp.inf)
    _, idx = lax.top_k(sc, ktop)
    valid = idx <= tpos[None, :, None]
    bidx = jnp.arange(B)[:, None, None]
    in_past = (idx < past)[..., None, None]
    pidx = jnp.minimum(idx, past - 1)
    phys = page_table[bidx, pidx // PAGE_SIZE]
    off = pidx % PAGE_SIZE
    nidx = jnp.clip(idx - past, 0, T - 1)
    kg = jnp.where(in_past, cache_k[phys, off].astype(k.dtype), k[bidx, nidx])
    vg = jnp.where(in_past, cache_v[phys, off].astype(v.dtype), v[bidx, nidx])
    return _gathered_attention(q, kg, vg, valid)


def _causal_conv(x, buf, conv_w):
    T = x.shape[1]
    xp = jnp.concatenate([buf.astype(x.dtype), x], axis=1)
    y = xp[:, 0:T] * conv_w[0]
    for j in range(1, CONV_W):
        y = y + xp[:, j:j + T] * conv_w[j]
    return jax.nn.silu(y), xp[:, T:]


def _chunk_gated_delta(q, k, v, g, beta, S0):
    B, T, H, DK = q.shape
    DV = v.shape[-1]
    C = min(GDN_CHUNK, T)
    n = -(-T // C)
    pad = n * C - T

    def prep(a):
        a = jnp.pad(a, [(0, 0), (0, pad)] + [(0, 0)] * (a.ndim - 2))
        a = a.reshape((B, n, C) + a.shape[2:])
        return jnp.swapaxes(jnp.moveaxis(a, 1, 0), 2, 3)

    q, k, v, g, beta = prep(q), prep(k), prep(v), prep(g), prep(beta)
    gc = jnp.cumsum(g, axis=-1)
    causal = jnp.tril(jnp.ones((C, C), bool))
    strict = jnp.tril(jnp.ones((C, C), bool), -1)
    decay = jnp.exp(jnp.where(causal, gc[..., :, None] - gc[..., None, :], -jnp.inf))
    kb = k * beta[..., None]
    lower = jnp.where(strict, jnp.einsum('...id,...jd->...ij', kb, k) * decay, 0.0)
    amat = lower + jnp.eye(C, dtype=jnp.float32)
    rhs = jnp.concatenate([v * beta[..., None], kb * jnp.exp(gc)[..., None]], axis=-1)
    sol = lax.linalg.triangular_solve(amat, rhs, left_side=True, lower=True, unit_diagonal=True)
    u, w = sol[..., :DV], sol[..., DV:]
    qk = jnp.einsum('...id,...jd->...ij', q, k) * decay
    qg = q * jnp.exp(gc)[..., None]
    kd = k * jnp.exp(gc[..., -1:] - gc)[..., None]
    g_last = jnp.exp(gc[..., -1])

    def step(S, xs):
        qg_i, kd_i, u_i, w_i, qk_i, gl_i = xs
        v_new = u_i - jnp.einsum('bhck,bhkv->bhcv', w_i, S)
        o = jnp.einsum('bhck,bhkv->bhcv', qg_i, S) + jnp.einsum('bhcj,bhjv->bhcv', qk_i, v_new)
        S = S * gl_i[..., None, None] + jnp.einsum('bhck,bhcv->bhkv', kd_i, v_new)
        return S, o

    S, o = lax.scan(step, S0, (qg, kd, u, w, qk, g_last))
    o = jnp.moveaxis(jnp.swapaxes(o, 2, 3), 0, 1).reshape(B, n * C, H, DV)[:, :T]
    return o, S


def _gated_deltanet(qkv, b_raw, a_raw, gate, conv_buf, S0, conv_w, a_log, dt_bias, gdn_norm_w):
    B, T, _ = qkv.shape
    f32 = jnp.float32
    conv, new_buf = _causal_conv(qkv, conv_buf, conv_w)
    qc, kc, vc = jnp.split(conv, [GDN_HEADS * GDN_DK, 2 * GDN_HEADS * GDN_DK], axis=-1)
    q = _l2norm(qc.reshape(B, T, GDN_HEADS, GDN_DK).astype(f32)) * (GDN_DK ** -0.5)
    k = _l2norm(kc.reshape(B, T, GDN_HEADS, GDN_DK).astype(f32))
    v = vc.reshape(B, T, GDN_HEADS, GDN_DV).astype(f32)
    beta = jax.nn.sigmoid(b_raw.astype(f32))
    g = -jnp.exp(a_log.astype(f32)) * jax.nn.softplus(a_raw.astype(f32) + dt_bias.astype(f32))
    o, S = _chunk_gated_delta(q, k, v, g, beta, S0)
    o = _rmsnorm(o, gdn_norm_w) * jax.nn.silu(gate.reshape(B, T, GDN_HEADS, GDN_DV).astype(f32))
    return o.reshape(B, T, GDN_HEADS * GDN_DV).astype(qkv.dtype), S, new_buf


def _peer(xn, peer_wq, peer_keys, peer_u, peer_v):
    shape = xn.shape
    xf = xn.reshape(-1, D_MODEL)
    N = xf.shape[0]
    nb = -(-N // PEER_BLOCK)
    xb = jnp.pad(xf, ((0, nb * PEER_BLOCK - N), (0, 0))).reshape(nb, PEER_BLOCK, D_MODEL)
    ncand = PEER_TOPK * PEER_TOPK

    def blk(x):
        qh = (x @ peer_wq).reshape(PEER_BLOCK, PEER_HEADS, 2, PEER_QDIM // 2)
        s1 = jnp.einsum('thd,hkd->thk', qh[:, :, 0], peer_keys[0]).astype(jnp.float32)
        s2 = jnp.einsum('thd,hkd->thk', qh[:, :, 1], peer_keys[1]).astype(jnp.float32)
        v1, i1 = lax.top_k(s1, PEER_TOPK)
        v2, i2 = lax.top_k(s2, PEER_TOPK)
        cand = (v1[..., :, None] + v2[..., None, :]).reshape(PEER_BLOCK, PEER_HEADS, ncand)
        cidx = (i1[..., :, None] * PEER_NKEYS + i2[..., None, :]).reshape(PEER_BLOCK, PEER_HEADS, ncand)
        sv, si = lax.top_k(cand, PEER_TOPK)
        eidx = jnp.take_along_axis(cidx, si, axis=-1)
        gsm = jax.nn.softmax(sv, axis=-1)
        act = jax.nn.gelu(jnp.einsum('thkd,td->thk', peer_u[eidx], x).astype(jnp.float32), approximate=False)
        return jnp.einsum('thk,thkd->td', (gsm * act).astype(x.dtype), peer_v[eidx])

    y = lax.map(blk, xb).reshape(nb * PEER_BLOCK, D_MODEL)[:N]
    return y.reshape(shape)


GDN_ROWS = 512
_TN = (((0,), (0,)), ((), ()))


def _mm(a, b):
    return jnp.dot(a.astype(jnp.bfloat16), b.astype(jnp.bfloat16), preferred_element_type=jnp.float32)


def _split2(a):
    hi = a.astype(jnp.bfloat16)
    lo = (a - hi.astype(jnp.float32)).astype(jnp.bfloat16)
    return hi, lo


def _mm3(a, b):
    f32 = jnp.float32
    a1, a2 = _split2(a)
    b1, b2 = _split2(b)
    return (jnp.dot(a1, b1, preferred_element_type=f32) + jnp.dot(a1, b2, preferred_element_type=f32)
            + jnp.dot(a2, b1, preferred_element_type=f32))


def _split3(a):
    f32 = jnp.float32
    p1 = a.astype(jnp.bfloat16)
    r = a - p1.astype(f32)
    p2 = r.astype(jnp.bfloat16)
    p3 = (r - p2.astype(f32)).astype(jnp.bfloat16)
    return p1, p2, p3


def _sigmoid(x):
    return 1.0 / (1.0 + jnp.exp(-x))


def _softplus(x):
    return jnp.maximum(x, 0.0) + jnp.log1p(jnp.exp(-jnp.abs(x)))


def _gdn_body(x_ref, araw_ref, braw_ref, arawT_ref, gate_ref, buf_ref, s0_ref, cw_ref,
              alog_ref, dtb_ref, alogT_ref, dtbT_ref, nw_ref,
              ob_ref, sout_ref, xp_scr, y_scr, s_scr, *, rb, t_valid, t_pad):
    f32, bf16 = jnp.float32, jnp.bfloat16
    C = GDN_CHUNK
    j = pl.program_id(1)

    @pl.when(j == 0)
    def _():
        xp_scr[5:8, :] = buf_ref[...]
        s_scr[...] = s0_ref[...]

    xp_scr[8:8 + rb, :] = x_ref[...]
    y = xp_scr[5:5 + rb, :] * cw_ref[0:1, :]
    for t in range(1, CONV_W):
        y = y + xp_scr[5 + t:5 + t + rb, :] * cw_ref[t:t + 1, :]
    y_scr[...] = y * _sigmoid(y)
    xp_scr[5:8, :] = xp_scr[rb + 5:rb + 8, :]

    ri = lax.broadcasted_iota(jnp.int32, (C, C), 0)
    ci = lax.broadcasted_iota(jnp.int32, (C, C), 1)
    tri_incl = jnp.where(ri >= ci, 1.0, 0.0).astype(bf16)
    tri_inclT = jnp.where(ci >= ri, 1.0, 0.0).astype(bf16)
    causal = ri >= ci
    strict = ri > ci
    eye = jnp.where(ri == ci, 1.0, 0.0)
    neg_a = -jnp.exp(alog_ref[...])
    neg_aT = -jnp.exp(alogT_ref[...])
    q_scale = GDN_DK ** -0.5

    def chunk(c, carry):
        r0 = pl.multiple_of(c * C, C)
        g_c = neg_a * _softplus(araw_ref[pl.ds(r0, C), :] + dtb_ref[...])
        beta_c = _sigmoid(braw_ref[pl.ds(r0, C), :])
        g_r = neg_aT * _softplus(arawT_ref[c] + dtbT_ref[...])
        if t_valid < t_pad:
            base = j * rb + r0
            row_ok = (base + lax.broadcasted_iota(jnp.int32, (C, GDN_HEADS), 0)) < t_valid
            col_ok = (base + lax.broadcasted_iota(jnp.int32, (GDN_HEADS, C), 1)) < t_valid
            g_c = jnp.where(row_ok, g_c, 0.0)
            beta_c = jnp.where(row_ok, beta_c, 0.0)
            g_r = jnp.where(col_ok, g_r, 0.0)
        gc_c = sum(jnp.dot(tri_incl, p, preferred_element_type=f32) for p in _split3(g_c))
        gc_r = sum(jnp.dot(p, tri_inclT, preferred_element_type=f32) for p in _split3(g_r))
        H = range(GDN_HEADS)
        q, k, v, gcol, glast, beta, decay, eg = [], [], [], [], [], [], [], []
        for h in H:
            lo = h * GDN_DK
            qh = y_scr[pl.ds(r0, C), lo:lo + GDN_DK]
            kh = y_scr[pl.ds(r0, C), GDN_HEADS * GDN_DK + lo:GDN_HEADS * GDN_DK + lo + GDN_DK]
            v.append(y_scr[pl.ds(r0, C),
                           2 * GDN_HEADS * GDN_DK + h * GDN_DV:2 * GDN_HEADS * GDN_DK + (h + 1) * GDN_DV])
            q.append(qh * lax.rsqrt(jnp.sum(qh * qh, axis=-1, keepdims=True) + NORM_EPS) * q_scale)
            k.append(kh * lax.rsqrt(jnp.sum(kh * kh, axis=-1, keepdims=True) + NORM_EPS))
            gcol.append(gc_c[:, h:h + 1])
            glast.append(gc_c[C - 1:C, h:h + 1])
            beta.append(beta_c[:, h:h + 1])
            decay.append(jnp.exp(jnp.where(causal, gcol[h] - gc_r[h:h + 1, :], -jnp.inf)))
            eg.append(jnp.exp(gcol[h]))
        kb = [k[h] * beta[h] for h in H]
        kk = [lax.dot_general(kb[h].astype(bf16), k[h].astype(bf16), _NT, preferred_element_type=f32) for h in H]
        qk = [lax.dot_general(q[h].astype(bf16), k[h].astype(bf16), _NT, preferred_element_type=f32) * decay[h]
              for h in H]
        pw = [jnp.where(strict, -(kk[h] * decay[h]), 0.0) for h in H]
        inv = [eye + pw[h] for h in H]
        for _ in range(5):
            pw = [_mm3(pw[h], pw[h]) for h in H]
            inv = [inv[h] + _mm3(inv[h], pw[h]) for h in H]
        sol = [_mm3(inv[h], jnp.concatenate([v[h] * beta[h], kb[h] * eg[h]], axis=1)) for h in H]
        s_old = [s_scr[h] for h in H]
        v_new = [sol[h][:, :GDN_DV] - _mm(sol[h][:, GDN_DV:], s_old[h]) for h in H]
        o = [_mm(q[h] * eg[h], s_old[h]) + _mm(qk[h], v_new[h]) for h in H]
        for h in H:
            kd = k[h] * jnp.exp(glast[h] - gcol[h])
            s_scr[h] = s_old[h] * jnp.exp(glast[h]) + lax.dot_general(
                kd.astype(bf16), v_new[h].astype(bf16), _TN, preferred_element_type=f32)
        for h in H:
            on = o[h] * lax.rsqrt(jnp.mean(o[h] * o[h], axis=-1, keepdims=True) + NORM_EPS) * nw_ref[...]
            gt = gate_ref[pl.ds(r0, C), h * GDN_DV:(h + 1) * GDN_DV]
            ob_ref[pl.ds(r0, C), h * GDN_DV:(h + 1) * GDN_DV] = on * (gt * _sigmoid(gt))
        return carry

    lax.fori_loop(0, rb // C, chunk, 0)

    @pl.when(j == pl.num_programs(1) - 1)
    def _():
        sout_ref[...] = s_scr[...]


def _gdn_pallas(qkv, b_raw, a_raw, gate, conv_buf, S0, conv_w, a_log, dt_bias, gdn_norm_w,
                qkv_blk=0, gate_blk=0, interpret=False):
    B, T, _ = qkv.shape
    assert T >= CONV_W - 1
    f32 = jnp.float32
    C = GDN_CHUNK
    t_pad = -(-T // C) * C
    rb = min(GDN_ROWS, t_pad)
    assert t_pad % rb == 0
    pad = lambda a: jnp.pad(a, ((0, 0), (0, t_pad - T), (0, 0)))
    x, a_p, b_p = pad(qkv), pad(a_raw), pad(b_raw)
    gate_p = x if gate is qkv else pad(gate)
    a_t = jnp.transpose(a_p.reshape(B, t_pad // C, C, GDN_HEADS), (0, 1, 3, 2))
    hd = GDN_HEADS * GDN_DV
    row_blk = lambda b, j: (b, j, 0)
    fix2 = lambda b, j: (0, 0)
    ob, s_out = pl.pallas_call(
        functools.partial(_gdn_body, rb=rb, t_valid=T, t_pad=t_pad),
        grid=(B, t_pad // rb),
        in_specs=[pl.BlockSpec((None, rb, CONV_DIM), lambda b, j: (b, j, qkv_blk)),
                  pl.BlockSpec((None, rb, GDN_HEADS), row_blk),
                  pl.BlockSpec((None, rb, GDN_HEADS), row_blk),
                  pl.BlockSpec((None, rb // C, GDN_HEADS, C), lambda b, j: (b, j, 0, 0)),
                  pl.BlockSpec((None, rb, hd), lambda b, j: (b, j, gate_blk)),
                  pl.BlockSpec((None, CONV_W - 1, CONV_DIM), lambda b, j: (b, 0, 0)),
                  pl.BlockSpec((None, GDN_HEADS, GDN_DK, GDN_DV), lambda b, j: (b, 0, 0, 0)),
                  pl.BlockSpec((CONV_W, CONV_DIM), fix2),
                  pl.BlockSpec((1, GDN_HEADS), fix2),
                  pl.BlockSpec((1, GDN_HEADS), fix2),
                  pl.BlockSpec((GDN_HEADS, 1), fix2),
                  pl.BlockSpec((GDN_HEADS, 1), fix2),
                  pl.BlockSpec((1, GDN_DV), fix2)],
        out_specs=[pl.BlockSpec((None, rb, hd), row_blk),
                   pl.BlockSpec((None, GDN_HEADS, GDN_DK, GDN_DV), lambda b, j: (b, 0, 0, 0))],
        out_shape=[jax.ShapeDtypeStruct((B, t_pad, hd), f32),
                   jax.ShapeDtypeStruct((B, GDN_HEADS, GDN_DK, GDN_DV), f32)],
        scratch_shapes=[pltpu.VMEM((rb + 8, CONV_DIM), f32),
                        pltpu.VMEM((rb, CONV_DIM), f32),
                        pltpu.VMEM((GDN_HEADS, GDN_DK, GDN_DV), f32)],
        compiler_params=pltpu.CompilerParams(
            dimension_semantics=("arbitrary", "arbitrary"),
            vmem_limit_bytes=56 * 1024 * 1024),
        name="gated_deltanet",
        interpret=interpret,
    )(x, a_p, b_p, a_t, gate_p, conv_buf, S0, conv_w,
      a_log.reshape(1, GDN_HEADS), dt_bias.reshape(1, GDN_HEADS),
      a_log.reshape(GDN_HEADS, 1), dt_bias.reshape(GDN_HEADS, 1), gdn_norm_w.reshape(1, GDN_DV))
    return ob[:, :T], s_out, qkv[:, T - (CONV_W - 1):, qkv_blk * CONV_DIM:(qkv_blk + 1) * CONV_DIM]


_SQRT_HALF = 0.7071067811865476


def _top_rows(x, k):
    R, n = x.shape
    ri = lax.broadcasted_iota(jnp.int32, (R, n), 0).astype(jnp.float32)
    ki = lax.broadcasted_iota(jnp.int32, (k, n), 0)

    def body(r, c):
        x, out = c
        m = x.max(axis=0, keepdims=True)
        first = jnp.min(jnp.where(x == m, ri, float(R)), axis=0, keepdims=True)
        x = jnp.where(ri == first, -jnp.inf, x)
        out = jnp.where(ki == r, m, out)
        return x, out

    _, out = lax.fori_loop(0, k, body, (x, jnp.zeros((k, n), jnp.float32)))
    return out


def _peer_front_body(x_ref, oa_ref, ob_ref, wo_ref, g_ref, wq_ref, keys_ref,
                     h_ref, xn_ref, s1_ref, s2_ref, e2_ref, aux_ref):
    f32, bf16 = jnp.float32, jnp.bfloat16
    half_w = ATTN_HEADS * HEAD_DIM
    h = (x_ref[...]
         + jnp.dot(oa_ref[...].astype(bf16), wo_ref[:half_w, :], preferred_element_type=f32)
         + jnp.dot(ob_ref[...].astype(bf16), wo_ref[half_w:, :], preferred_element_type=f32))
    h_ref[...] = h
    xn = (h * lax.rsqrt(jnp.mean(h * h, axis=-1, keepdims=True) + NORM_EPS) * g_ref[...]).astype(bf16)
    xn_ref[...] = xn
    qh = jnp.dot(xn, wq_ref[...], preferred_element_type=f32).astype(bf16)
    tq = qh.shape[0]
    hq = PEER_QDIM // 2
    for hh in range(PEER_HEADS):
        tops = []
        for half in range(2):
            col = (hh * 2 + half) * hq
            sT = lax.dot_general(keys_ref[half, hh], qh[:, col:col + hq], _NT,
                                 preferred_element_type=f32)
            (s1_ref if half == 0 else s2_ref)[hh] = sT
            tops.append(_top_rows(sT, PEER_TOPK))
        a16, b16 = tops
        cand = jnp.concatenate(
            [a16[r:r + 1, :] + b16[0:8, :] for r in range(8)]
            + [a16[0:1, :] + b16[8:16, :], a16[8:16, :] + b16[0:1, :]], axis=0)
        tau = _top_rows(cand, PEER_TOPK)[PEER_TOPK - 1:PEER_TOPK, :]
        top_sum = a16[0:1, :] + b16[0:1, :]
        z = jnp.sum(jnp.where(cand >= tau, jnp.exp(cand - top_sum), 0.0), axis=0, keepdims=True)
        e2_ref[hh] = jnp.exp(s2_ref[hh] - b16[0:1, :]) / z
        aux_ref[hh] = jnp.concatenate([tau, a16[0:1, :], jnp.zeros((6, tq), f32)], axis=0)


def _peer_front(x, oa, ob, wo, g, wq, keys, tq, interpret=False):
    n = x.shape[0]
    half_w = ATTN_HEADS * HEAD_DIM
    tok = lambda i: (i, 0)
    fix2 = lambda i: (0, 0)
    colT = lambda i: (0, 0, i)
    f32 = jnp.float32
    plane = jax.ShapeDtypeStruct((PEER_HEADS, PEER_NKEYS, n), f32)
    plane_spec = pl.BlockSpec((PEER_HEADS, PEER_NKEYS, tq), colT)
    return pl.pallas_call(
        _peer_front_body,
        grid=(n // tq,),
        in_specs=[pl.BlockSpec((tq, D_MODEL), tok),
                  pl.BlockSpec((tq, half_w), tok),
                  pl.BlockSpec((tq, half_w), tok),
                  pl.BlockSpec((D_MODEL, D_MODEL), fix2),
                  pl.BlockSpec((1, D_MODEL), fix2),
                  pl.BlockSpec((D_MODEL, PEER_HEADS * PEER_QDIM), fix2),
                  pl.BlockSpec((2, PEER_HEADS, PEER_NKEYS, PEER_QDIM // 2), lambda i: (0, 0, 0, 0))],
        out_specs=[pl.BlockSpec((tq, D_MODEL), tok),
                   pl.BlockSpec((tq, D_MODEL), tok),
                   plane_spec, plane_spec, plane_spec,
                   pl.BlockSpec((PEER_HEADS, 8, tq), colT)],
        out_shape=[jax.ShapeDtypeStruct((n, D_MODEL), f32),
                   jax.ShapeDtypeStruct((n, D_MODEL), jnp.bfloat16),
                   plane, plane, plane,
                   jax.ShapeDtypeStruct((PEER_HEADS, 8, n), f32)],
        compiler_params=pltpu.CompilerParams(
            dimension_semantics=("arbitrary",),
            vmem_limit_bytes=56 * 1024 * 1024),
        name="peer_front",
        interpret=interpret,
    )(x, oa, ob, wo, g.reshape(1, D_MODEL), wq, keys)


def _peer_dense_body(xn_ref, u_ref, vT_ref, s1_ref, s2_ref, e2_ref, aux_ref, yT_ref, *, eblk):
    f32 = jnp.float32
    eb = pl.program_id(1)

    @pl.when(eb == 0)
    def _():
        yT_ref[...] = jnp.zeros(yT_ref.shape, f32)

    a = lax.dot_general(u_ref[...], xn_ref[...], _NT, preferred_element_type=f32)
    act = 0.5 * a * (1.0 + lax.erf(a * _SQRT_HALF))
    sub = eblk // PEER_NKEYS
    pieces = []
    for r in range(sub):
        i1 = eb * sub + r
        gate = jnp.zeros((PEER_NKEYS, a.shape[1]), f32)
        for hh in range(PEER_HEADS):
            s1row = s1_ref[hh, pl.ds(i1, 1), :]
            tau = aux_ref[hh, 0:1, :]
            e1row = jnp.exp(s1row - aux_ref[hh, 1:2, :])
            gate = gate + jnp.where(s1row + s2_ref[hh] >= tau, e1row * e2_ref[hh], 0.0)
        pieces.append((gate * act[r * PEER_NKEYS:(r + 1) * PEER_NKEYS, :]).astype(jnp.bfloat16))
    hT = jnp.concatenate(pieces, axis=0)
    yT_ref[...] += jnp.dot(vT_ref[...], hT, preferred_element_type=f32)


def _peer_dense(xn, u, vT, s1, s2, e2, aux, tq, eblk, interpret=False):
    n = xn.shape[0]
    ne = u.shape[0]
    plane_spec = pl.BlockSpec((PEER_HEADS, PEER_NKEYS, tq), lambda i, e: (0, 0, i))
    return pl.pallas_call(
        functools.partial(_peer_dense_body, eblk=eblk),
        grid=(n // tq, ne // eblk),
        in_specs=[pl.BlockSpec((tq, D_MODEL), lambda i, e: (i, 0)),
                  pl.BlockSpec((eblk, D_MODEL), lambda i, e: (e, 0)),
                  pl.BlockSpec((D_MODEL, eblk), lambda i, e: (0, e)),
                  plane_spec, plane_spec, plane_spec,
                  pl.BlockSpec((PEER_HEADS, 8, tq), lambda i, e: (0, 0, i))],
        out_specs=pl.BlockSpec((D_MODEL, tq), lambda i, e: (0, i)),
        out_shape=jax.ShapeDtypeStruct((D_MODEL, n), jnp.float32),
        compiler_params=pltpu.CompilerParams(
            dimension_semantics=("arbitrary", "arbitrary"),
            vmem_limit_bytes=56 * 1024 * 1024),
        name="peer_dense",
        interpret=interpret,
    )(xn, u, vT, s1, s2, e2, aux)


def _layer_out_pallas(x, oa, ob, wo_b, ffn_norm_w, wq_b, keys_b, u_b, vT_b, interpret=False):
    n = x.shape[0]
    tq1 = min(256, n)
    tq2 = 512 if n % 512 == 0 else min(256, n)
    h, xn, s1, s2, e2, aux = _peer_front(x, oa, ob, wo_b, ffn_norm_w, wq_b, keys_b, tq1, interpret)
    yT = _peer_dense(xn, u_b, vT_b, s1, s2, e2, aux, tq2, 512, interpret)
    return h + yT.T


def _layer_out(x, oa, ob, w_out, ffn_norm_w, peer_wq, peer_keys, peer_u, peer_v):
    h = x + jnp.concatenate([oa, ob], axis=-1) @ w_out
    return h + _peer(_rmsnorm(h, ffn_norm_w), peer_wq, peer_keys, peer_u, peer_v)


def kernel(x_prompt, x_sample, cache_k, cache_v, cache_idx_k, state_ssm, state_conv, page_table,
           attn_norm_w, w_in, q_norm_w, k_norm_w, idx_k_norm_w, conv_w, a_log, dt_bias, gdn_norm_w,
           w_out, ffn_norm_w, peer_wq, peer_keys, peer_u, peer_v):
    l = 0
    proj_w = (attn_norm_w[l], _permute_w_in(w_in[l]), q_norm_w[l], k_norm_w[l], idx_k_norm_w[l])
    gdn_cols = dict(qkv_blk=_DST["qkv"] // CONV_DIM, gate_blk=_DST["gate"] // (GDN_HEADS * GDN_DV))
    gdn_w = (conv_w[l], a_log[l], dt_bias[l], gdn_norm_w[l])
    bf16 = jnp.bfloat16
    out_w = (w_out[l].astype(bf16), ffn_norm_w[l], peer_wq[l].astype(bf16), peer_keys[l].astype(bf16),
             peer_u[l].astype(bf16), peer_v[l].astype(bf16).T)

    hp, hs = x_prompt, x_sample
    (qa, ka, va, qi, ki, wi), (z, b_raw, a_raw) = _in_projection(hp, *proj_w)
    oa = _dsa_prompt_pallas(qa[0], ka[0], va[0], qi[0], ki[0], wi[0])[None]
    Bp = hp.shape[0]
    buf0 = jnp.zeros((Bp, CONV_W - 1, CONV_DIM), hp.dtype)
    S0 = jnp.zeros((Bp, GDN_HEADS, GDN_DK, GDN_DV), jnp.float32)
    ob, S_p, buf_p = _gdn_pallas(z, b_raw, a_raw, z, buf0, S0, *gdn_w, **gdn_cols)
    half_w = ATTN_HEADS * HEAD_DIM
    hp = _layer_out_pallas(hp[0], oa[0], ob[0], *out_w)[None]
    kp, vp, ip = ka, va, ki

    (qa, ka, va, qi, ki, wi), (z, b_raw, a_raw) = _in_projection(hs, *proj_w)
    oa = _dsa_sample_pallas(qa, ka, va, qi, ki, wi, cache_k[l], cache_v[l], cache_idx_k[l], page_table)
    ob, S_s, buf_s = _gdn_pallas(z, b_raw, a_raw, z, state_conv[l], state_ssm[l].astype(jnp.float32),
                                 *gdn_w, **gdn_cols)
    ns = hs.shape[0] * hs.shape[1]
    hs = _layer_out_pallas(hs.reshape(ns, D_MODEL), oa.reshape(ns, half_w), ob.reshape(ns, half_w),
                           *out_w).reshape(hs.shape)

    return (hp, hs, kp[None], vp[None], ip[None], S_p[None], buf_p[None],
            ka[None], va[None], ki[None], S_s[None], buf_s[None])
```

```python
import functools
import math

import jax
import jax.numpy as jnp
from jax import lax
from jax.experimental import pallas as pl
from jax.experimental.pallas import tpu as pltpu

D_MODEL = 2048
PAGE_SIZE = 128
HEAD_DIM = 128
ATTN_HEADS = 8
KV_HEADS = 2
GROUP = ATTN_HEADS // KV_HEADS
IDX_HEADS = 16
IDX_DIM = 64
INDEX_TOPK = 256
Q_BLOCK = 128
IDX_SCALE = (IDX_HEADS * IDX_DIM) ** -0.5
GDN_DK = 128
GDN_DV = 128
GDN_HEADS = 8
CONV_W = 4
CONV_DIM = GDN_HEADS * (2 * GDN_DK + GDN_DV)
GDN_CHUNK = 64
IN_SIZES = (ATTN_HEADS * HEAD_DIM, KV_HEADS * HEAD_DIM, KV_HEADS * HEAD_DIM,
            IDX_HEADS * IDX_DIM, IDX_DIM, IDX_HEADS,
            CONV_DIM, GDN_HEADS, GDN_HEADS, GDN_HEADS * GDN_DV)
IN_COLS = sum(IN_SIZES)
PEER_HEADS = 8
PEER_NKEYS = 128
PEER_QDIM = 256
PEER_TOPK = 16
PEER_BLOCK = 128
NORM_EPS = 1e-6

LANES = 128


def _rmsnorm(x, w):
    xf = x.astype(jnp.float32)
    y = xf * lax.rsqrt(jnp.mean(xf * xf, axis=-1, keepdims=True) + NORM_EPS)
    return (y * w.astype(jnp.float32)).astype(x.dtype)


def _l2norm(x):
    return x * lax.rsqrt(jnp.sum(x * x, axis=-1, keepdims=True) + NORM_EPS)


def _norm_matmul_body(x_ref, g_ref, w_ref, o_ref, xn_ref):
    @pl.when(pl.program_id(1) == 0)
    def _():
        x = x_ref[...]
        r = lax.rsqrt(jnp.mean(x * x, axis=-1, keepdims=True) + NORM_EPS)
        xn_ref[...] = (x * r * g_ref[...]).astype(jnp.bfloat16)

    o_ref[...] = jnp.dot(xn_ref[...], w_ref[...], preferred_element_type=jnp.float32)


def _norm_matmul(x, g, w, tm, tn):
    m, k = x.shape
    n = w.shape[1]
    return pl.pallas_call(
        _norm_matmul_body,
        grid=(m // tm, n // tn),
        in_specs=[pl.BlockSpec((tm, k), lambda i, j: (i, 0)),
                  pl.BlockSpec((1, k), lambda i, j: (0, 0)),
                  pl.BlockSpec((k, tn), lambda i, j: (0, j))],
        out_specs=pl.BlockSpec((tm, tn), lambda i, j: (i, j)),
        out_shape=jax.ShapeDtypeStruct((m, n), jnp.float32),
        scratch_shapes=[pltpu.VMEM((tm, k), jnp.bfloat16)],
        compiler_params=pltpu.CompilerParams(
            dimension_semantics=("arbitrary", "arbitrary"),
            vmem_limit_bytes=48 * 1024 * 1024),
        name="norm_matmul",
    )(x, g.reshape(1, k), w)


PROJ_TILE = 1024
_SRC = dict(zip(("qa", "ka", "va", "qi", "ki", "wi", "qkv", "b", "a", "gate"),
                [(sum(IN_SIZES[:i]), IN_SIZES[i]) for i in range(len(IN_SIZES))]))
_DST_ORDER = (("qkv",), ("qa",), ("qi",), ("gate",), ("ka",), ("va",), ("ki",), ("wi", "b", "a"))


def _proj_layout():
    dst, off = {}, 0
    for group in _DST_ORDER:
        for name in group:
            dst[name] = off
            off += _SRC[name][1]
        off = -(-off // LANES) * LANES
    return dst, -(-off // PROJ_TILE) * PROJ_TILE


_DST, PROJ_COLS = _proj_layout()
assert _DST["qkv"] == 0 and _DST["gate"] % (GDN_HEADS * GDN_DV) == 0


def _permute_w_in(w_in):
    pieces, off = [], 0
    for group in _DST_ORDER:
        for name in group:
            if _DST[name] > off:
                pieces.append(jnp.zeros((D_MODEL, _DST[name] - off), w_in.dtype))
            s0, n = _SRC[name]
            pieces.append(w_in[:, s0:s0 + n])
            off = _DST[name] + n
    pieces.append(jnp.zeros((D_MODEL, PROJ_COLS - off), w_in.dtype))
    return jnp.concatenate(pieces, axis=1).astype(jnp.bfloat16)


def _in_projection(x, attn_norm_w, w_in_p, q_norm_w, k_norm_w, idx_k_norm_w):
    B, T, _ = x.shape
    m = B * T
    tm = 512 if m % 512 == 0 else m
    z = _norm_matmul(x.reshape(m, D_MODEL), attn_norm_w, w_in_p, tm, PROJ_TILE).reshape(B, T, PROJ_COLS)
    col = lambda name: z[:, :, _DST[name]:_DST[name] + _SRC[name][1]]
    qa = _rmsnorm(col("qa").reshape(B, T, ATTN_HEADS, HEAD_DIM), q_norm_w)
    ka = _rmsnorm(col("ka").reshape(B, T, KV_HEADS, HEAD_DIM), k_norm_w)
    va = col("va").reshape(B, T, KV_HEADS, HEAD_DIM)
    qi = col("qi").reshape(B, T, IDX_HEADS, IDX_DIM)
    ki = _rmsnorm(col("ki"), idx_k_norm_w)
    return (qa, ka, va, qi, ki, col("wi")), (z, col("b"), col("a"))


def _index_scores(qi, wi, ki):
    s = jax.nn.relu(jnp.einsum('bqhd,bsd->bqhs', qi, ki).astype(jnp.float32))
    return jnp.einsum('bqhs,bqh->bqs', s, wi.astype(jnp.float32)) * IDX_SCALE


def _gathered_attention(q, kg, vg, valid):
    B, Q = q.shape[:2]
    qg = q.reshape(B, Q, KV_HEADS, GROUP, HEAD_DIM)
    s = jnp.einsum('bqngd,bqknd->bqngk', qg, kg).astype(jnp.float32) * (HEAD_DIM ** -0.5)
    s = jnp.where(valid[:, :, None, None, :], s, -jnp.inf)
    p = jax.nn.softmax(s, axis=-1)
    o = jnp.einsum('bqngk,bqknd->bqngd', p.astype(vg.dtype), vg)
    return o.reshape(B, Q, ATTN_HEADS * HEAD_DIM)


def _dsa_prompt(q, k, v, qi, ki, wi):
    B, T = q.shape[:2]
    ktop = min(INDEX_TOPK, T // 4)
    nb = T // Q_BLOCK
    bidx = jnp.arange(B)[:, None, None]
    spos = jnp.arange(T)

    def blocks(a):
        return jnp.moveaxis(a.reshape((B, nb, Q_BLOCK) + a.shape[2:]), 1, 0)

    def blk(xs):
        qb, qib, wib, start = xs
        tpos = start + jnp.arange(Q_BLOCK)
        sc = _index_scores(qib, wib, ki)
        sc = jnp.where(spos[None, None, :] <= tpos[None, :, None], sc, -jnp.inf)
        _, idx = lax.top_k(sc, ktop)
        valid = idx <= tpos[None, :, None]
        return _gathered_attention(qb, k[bidx, idx], v[bidx, idx], valid)

    o = lax.map(blk, (blocks(q), blocks(qi), blocks(wi), jnp.arange(nb) * Q_BLOCK))
    return jnp.moveaxis(o, 0, 1).reshape(B, T, ATTN_HEADS * HEAD_DIM)


_INT_MIN = -2 ** 31
_NEG_BIG = -1e30
_NT = (((1,), (1,)), ((), ()))


def _dsa_prompt_body(qi_ref, wT_ref, q_ref, ki_ref, k_ref, vT_ref, o_ref,
                     keys_ref, m_ref, l_ref, acc_ref, *, ktop, tq):
    f32 = jnp.float32
    i = pl.program_id(0)
    nkb = i + 1
    col_t = i * tq + lax.broadcasted_iota(jnp.int32, (tq, tq), 1)
    row_s = lax.broadcasted_iota(jnp.int32, (tq, tq), 0)

    def score_blk(kb, carry):
        kib = ki_ref[kb]
        acc = jnp.zeros((tq, tq), f32)
        for h in range(IDX_HEADS):
            s = lax.dot_general(kib, qi_ref[h], _NT, preferred_element_type=f32)
            acc = acc + jnp.maximum(s, 0.0) * wT_ref[h:h + 1, :]
        sc = acc * IDX_SCALE
        bits = lax.bitcast_convert_type(sc, jnp.int32)
        key = bits ^ (lax.shift_right_arithmetic(bits, 31) & 0x7FFFFFFF)
        valid = (kb * tq + row_s) <= col_t
        keys_ref[kb] = jnp.where(valid, key, _INT_MIN)
        return carry

    lax.fori_loop(0, nkb, score_blk, 0)

    def bisect(it, ans_u):
        cand_u = ans_u | lax.shift_left(jnp.int32(1), 31 - it)
        cand_s = cand_u ^ _INT_MIN

        def count_blk(kb, cnt):
            hit = jnp.where(keys_ref[kb] >= cand_s, 1.0, 0.0)
            return cnt + hit.reshape(tq // 8, 8, tq).sum(axis=0)

        cnt = lax.fori_loop(0, nkb, count_blk, jnp.zeros((8, tq), f32))
        cnt = cnt.sum(axis=0, keepdims=True)
        return jnp.where(cnt >= ktop, cand_u, ans_u)

    ans_u = lax.fori_loop(0, 32, bisect, jnp.zeros((1, tq), jnp.int32))
    thr = jnp.maximum(ans_u ^ _INT_MIN, _INT_MIN + 1)

    def tally(kb, c):
        kk = keys_ref[kb]
        ge = jnp.where(kk >= thr, 1.0, 0.0).reshape(tq // 8, 8, tq).sum(axis=0)
        gt = jnp.where(kk > thr, 1.0, 0.0).reshape(tq // 8, 8, tq).sum(axis=0)
        return c[0] + ge, c[1] + gt

    z8 = jnp.zeros((8, tq), f32)
    n_ge, n_gt = lax.fori_loop(0, nkb, tally, (z8, z8))
    n_ge = n_ge.sum(axis=0, keepdims=True)
    n_gt = n_gt.sum(axis=0, keepdims=True)

    @pl.when(jnp.max(n_ge) > ktop)
    def _():
        need = jnp.where(n_ge >= ktop, ktop - n_gt, float(2 ** 30))
        idx_bits = max(1, (keys_ref.shape[0] * tq - 1).bit_length())

        def bisect_idx(it, pos):
            cand = pos | lax.shift_left(jnp.int32(1), idx_bits - 1 - it)

            def count_blk(kb, cnt):
                hit = (keys_ref[kb] == thr) & ((kb * tq + row_s) < cand)
                return cnt + jnp.where(hit, 1.0, 0.0).reshape(tq // 8, 8, tq).sum(axis=0)

            cnt = lax.fori_loop(0, nkb, count_blk, z8).sum(axis=0, keepdims=True)
            return jnp.where(cnt < need, cand, pos)

        last = lax.fori_loop(0, idx_bits, bisect_idx, jnp.zeros((1, tq), jnp.int32))

        def demote(kb, carry):
            kk = keys_ref[kb]
            keys_ref[kb] = jnp.where((kk == thr) & ((kb * tq + row_s) > last), kk - 1, kk)
            return carry

        lax.fori_loop(0, nkb, demote, 0)

    m_ref[...] = jnp.full(m_ref.shape, _NEG_BIG, f32)
    l_ref[...] = jnp.zeros(l_ref.shape, f32)
    acc_ref[...] = jnp.zeros(acc_ref.shape, f32)
    c2 = HEAD_DIM ** -0.5 * math.log2(math.e)

    def attn_blk(kb, carry):
        sel = keys_ref[kb] >= thr
        kblk = k_ref[kb]
        vT = vT_ref[kb]
        s_all = [lax.dot_general(kblk[:, (h // GROUP) * HEAD_DIM:(h // GROUP + 1) * HEAD_DIM], q_ref[h], _NT,
                                 preferred_element_type=f32) for h in range(ATTN_HEADS)]
        for h in range(ATTN_HEADS):
            n = h // GROUP
            s = jnp.where(sel, s_all[h], -jnp.inf)
            m_old = m_ref[h]
            m_new = jnp.maximum(m_old, s.max(axis=0, keepdims=True))
            p = jnp.exp2((s - m_new) * c2)
            alpha = jnp.exp2((m_old - m_new) * c2)
            l_ref[h] = alpha * l_ref[h] + p.sum(axis=0, keepdims=True)
            pv = jnp.dot(vT[n * HEAD_DIM:(n + 1) * HEAD_DIM, :], p.astype(jnp.bfloat16),
                         preferred_element_type=f32)
            acc_ref[h] = alpha * acc_ref[h] + pv
            m_ref[h] = m_new
        return carry

    lax.fori_loop(0, nkb, attn_blk, 0)
    for h in range(ATTN_HEADS):
        o_ref[:, h * HEAD_DIM:(h + 1) * HEAD_DIM] = (acc_ref[h] / l_ref[h]).T


def _dsa_prompt_pallas(q, k, v, qi, ki, wi, interpret=False):
    T = q.shape[0]
    tq = min(256, T)
    nb = T // tq
    ktop = min(INDEX_TOPK, T // 4)
    bf16 = jnp.bfloat16
    qh = jnp.transpose(q.astype(bf16), (1, 0, 2))
    qih = jnp.transpose(qi.astype(bf16), (1, 0, 2))
    wT = wi.astype(jnp.float32).T
    kib = ki.astype(bf16).reshape(nb, tq, IDX_DIM)
    kb = k.astype(bf16).reshape(nb, tq, KV_HEADS * HEAD_DIM)
    vT = jnp.transpose(v.astype(bf16).reshape(nb, tq, KV_HEADS * HEAD_DIM), (0, 2, 1))
    body = functools.partial(_dsa_prompt_body, ktop=ktop, tq=tq)
    return pl.pallas_call(
        body,
        grid=(nb,),
        in_specs=[pl.BlockSpec((IDX_HEADS, tq, IDX_DIM), lambda i: (0, i, 0)),
                  pl.BlockSpec((IDX_HEADS, tq), lambda i: (0, i)),
                  pl.BlockSpec((ATTN_HEADS, tq, HEAD_DIM), lambda i: (0, i, 0)),
                  pl.BlockSpec((nb, tq, IDX_DIM), lambda i: (0, 0, 0)),
                  pl.BlockSpec((nb, tq, KV_HEADS * HEAD_DIM), lambda i: (0, 0, 0)),
                  pl.BlockSpec((nb, KV_HEADS * HEAD_DIM, tq), lambda i: (0, 0, 0))],
        out_specs=pl.BlockSpec((tq, ATTN_HEADS * HEAD_DIM), lambda i: (i, 0)),
        out_shape=jax.ShapeDtypeStruct((T, ATTN_HEADS * HEAD_DIM), jnp.float32),
        scratch_shapes=[pltpu.VMEM((nb, tq, tq), jnp.int32),
                        pltpu.VMEM((ATTN_HEADS, 1, tq), jnp.float32),
                        pltpu.VMEM((ATTN_HEADS, 1, tq), jnp.float32),
                        pltpu.VMEM((ATTN_HEADS, HEAD_DIM, tq), jnp.float32)],
        compiler_params=pltpu.CompilerParams(
            dimension_semantics=("arbitrary",),
            vmem_limit_bytes=56 * 1024 * 1024),
        name="dsa_prompt",
        interpret=interpret,
    )(qih, wT, qh, kib, kb, vT)


PAGES_PER_STEP = 8


def _sortable_key(x):
    bits = lax.bitcast_convert_type(x, jnp.int32)
    return bits ^ (lax.shift_right_arithmetic(bits, 31) & 0x7FFFFFFF)


def _sample_index_body(pt_ref, qi_ref, w_ref, kin_ref, *rest, ktop, nq, n_steps):
    del pt_ref
    pages = rest[:PAGES_PER_STEP]
    keys_out, knew_out, thr_out, keys_scr = rest[PAGES_PER_STEP:]
    f32 = jnp.float32
    j = pl.program_id(1)
    step_w = PAGES_PER_STEP * PAGE_SIZE
    past = n_steps * step_w
    qi = qi_ref[...]
    w = w_ref[...]

    def page_keys(page):
        s = lax.dot_general(qi, page.astype(jnp.bfloat16), _NT, preferred_element_type=f32)
        s = jnp.maximum(s, 0.0) * w
        return s.reshape(IDX_HEADS, nq, PAGE_SIZE).sum(axis=0) * IDX_SCALE

    keys_scr[j] = jnp.concatenate([_sortable_key(page_keys(p[...])) for p in pages], axis=1)

    @pl.when(j == n_steps - 1)
    def _():
        kn = _sortable_key(page_keys(kin_ref[...]))
        col = lax.broadcasted_iota(jnp.int32, (nq, PAGE_SIZE), 1)
        row = lax.broadcasted_iota(jnp.int32, (nq, PAGE_SIZE), 0)
        kn = jnp.where(col <= row, kn, _INT_MIN)
        keys_scr[n_steps] = jnp.concatenate(
            [kn, jnp.full((nq, step_w - PAGE_SIZE), _INT_MIN, jnp.int32)], axis=1)
        chunks = [(st, c) for st in range(n_steps + 1) for c in range(PAGES_PER_STEP)]

        def chunk_of(st, c):
            return (keys_scr[st, :, c * PAGE_SIZE:(c + 1) * PAGE_SIZE],
                    st * step_w + c * PAGE_SIZE + col)

        def bisect(it, ans_u):
            cand_u = ans_u | lax.shift_left(jnp.int32(1), 31 - it)
            cand_s = cand_u ^ _INT_MIN
            cnt = jnp.zeros((nq, PAGE_SIZE), f32)
            for st in range(n_steps + 1):
                for c in range(PAGES_PER_STEP):
                    chunk = keys_scr[st, :, c * PAGE_SIZE:(c + 1) * PAGE_SIZE]
                    cnt = cnt + jnp.where(chunk >= cand_s, 1.0, 0.0)
            cnt = cnt.sum(axis=1, keepdims=True)
            return jnp.where(cnt >= ktop, cand_u, ans_u)

        ans_u = lax.fori_loop(0, 32, bisect, jnp.zeros((nq, 1), jnp.int32))
        thr = jnp.maximum(ans_u ^ _INT_MIN, _INT_MIN + 1)
        thr_out[...] = jnp.broadcast_to(thr, (nq, PAGE_SIZE))

        n_ge = jnp.zeros((nq, PAGE_SIZE), f32)
        n_gt = jnp.zeros((nq, PAGE_SIZE), f32)
        for st, c in chunks:
            kk, _ = chunk_of(st, c)
            n_ge = n_ge + jnp.where(kk >= thr, 1.0, 0.0)
            n_gt = n_gt + jnp.where(kk > thr, 1.0, 0.0)
        n_ge = n_ge.sum(axis=1, keepdims=True)
        n_gt = n_gt.sum(axis=1, keepdims=True)
        need = jnp.where(n_ge >= ktop, ktop - n_gt, float(2 ** 30))
        idx_bits = max(1, (past + PAGE_SIZE - 1).bit_length())

        def bisect_idx(it, pos):
            cand = pos | lax.shift_left(jnp.int32(1), idx_bits - 1 - it)
            cnt = jnp.zeros((nq, PAGE_SIZE), f32)
            for st, c in chunks:
                kk, idx = chunk_of(st, c)
                cnt = cnt + jnp.where(kk == thr, jnp.where(idx < cand, 1.0, 0.0), 0.0)
            cnt = cnt.sum(axis=1, keepdims=True)
            return jnp.where(cnt < need, cand, pos)

        last = lax.fori_loop(0, idx_bits, bisect_idx, jnp.zeros((nq, 1), jnp.int32))

        def demoted(st, c):
            kk, idx = chunk_of(st, c)
            return jnp.where(kk == thr, jnp.where(idx > last, kk - 1, kk), kk)

        for st in range(n_steps):
            for c in range(PAGES_PER_STEP):
                lo = st * step_w + c * PAGE_SIZE
                keys_out[:, lo:lo + PAGE_SIZE] = demoted(st, c)
        knew_out[...] = demoted(n_steps, 0)


def _sample_attn_body(pt_ref, q_ref, keys_ref, knew_ref, thr_ref, kn_ref, vn_ref, *rest, nq, n_steps):
    del pt_ref
    kp = rest[:PAGES_PER_STEP]
    vp = rest[PAGES_PER_STEP:2 * PAGES_PER_STEP]
    o_ref, m_ref, l_ref, acc_ref = rest[2 * PAGES_PER_STEP:]
    f32, bf16 = jnp.float32, jnp.bfloat16
    j = pl.program_id(1)
    scale = HEAD_DIM ** -0.5
    thr = thr_ref[...]

    def update(keys_q, thr_q, k_blocks, v_blocks):
        sel = jnp.concatenate([keys_q] * GROUP, axis=0) >= jnp.concatenate([thr_q] * GROUP, axis=0)
        for n in range(KV_HEADS):
            qn = q_ref[n]
            s = jnp.concatenate(
                [lax.dot_general(qn, kb[pl.ds(n, PAGE_SIZE, stride=KV_HEADS), :].astype(bf16), _NT,
                                 preferred_element_type=f32)
                 for kb in k_blocks], axis=1) * scale
            s = jnp.where(sel, s, _NEG_BIG)
            m_old = m_ref[n]
            m_new = jnp.maximum(m_old, s.max(axis=1, keepdims=True))
            p = jnp.where(sel, jnp.exp(s - m_new), 0.0)
            alpha = jnp.exp(m_old - m_new)
            l_ref[n] = alpha * l_ref[n] + p.sum(axis=1, keepdims=True)
            pb = p.astype(bf16)
            pv = jnp.zeros((GROUP * nq, HEAD_DIM), f32)
            for c, vb in enumerate(v_blocks):
                pv = pv + jnp.dot(pb[:, c * PAGE_SIZE:(c + 1) * PAGE_SIZE],
                                  vb[pl.ds(n, PAGE_SIZE, stride=KV_HEADS), :].astype(bf16),
                                  preferred_element_type=f32)
            acc_ref[n] = alpha * acc_ref[n] + pv
            m_ref[n] = m_new

    @pl.when(j == 0)
    def _():
        m_ref[...] = jnp.full(m_ref.shape, _NEG_BIG, f32)
        l_ref[...] = jnp.zeros(l_ref.shape, f32)
        acc_ref[...] = jnp.zeros(acc_ref.shape, f32)
        update(knew_ref[...], thr, [kn_ref], [vn_ref])

    thr_w = jnp.concatenate([thr] * PAGES_PER_STEP, axis=1)
    update(keys_ref[...], thr_w, kp, vp)

    @pl.when(j == n_steps - 1)
    def _():
        for n in range(KV_HEADS):
            o_ref[n] = acc_ref[n] / l_ref[n]


def _dsa_sample_pallas(q, k, v, qi, ki, wi, cache_k, cache_v, cache_idx_k, page_table, interpret=False):
    B, T = q.shape[:2]
    n_pages = page_table.shape[1]
    past = n_pages * PAGE_SIZE
    ktop = min(INDEX_TOPK, (past + T) // 4)
    n_steps = n_pages // PAGES_PER_STEP
    step_w = PAGES_PER_STEP * PAGE_SIZE
    f32, bf16 = jnp.float32, jnp.bfloat16
    kvw = KV_HEADS * HEAD_DIM
    qi_s = jnp.transpose(qi.astype(bf16), (0, 2, 1, 3)).reshape(B, IDX_HEADS * T, IDX_DIM)
    w_s = jnp.transpose(wi.astype(f32), (0, 2, 1)).reshape(B, IDX_HEADS * T, 1)
    pad_rows = lambda a: jnp.pad(a, ((0, 0), (0, PAGE_SIZE - T)) + ((0, 0),) * (a.ndim - 2))
    ki_new = pad_rows(ki.astype(f32))
    slot_rows = PAGE_SIZE * KV_HEADS
    k_new = pad_rows(k).reshape(B, slot_rows, HEAD_DIM)
    v_new = pad_rows(v).reshape(B, slot_rows, HEAD_DIM)
    q_s = jnp.transpose(q.astype(bf16).reshape(B, T, KV_HEADS, GROUP, HEAD_DIM),
                        (0, 2, 3, 1, 4)).reshape(B, KV_HEADS, GROUP * T, HEAD_DIM)
    ck = cache_k.reshape(cache_k.shape[0], slot_rows, HEAD_DIM)
    cv = cache_v.reshape(cache_v.shape[0], slot_rows, HEAD_DIM)

    def page_map(r):
        return lambda b, j, pt: (pt[b, j * PAGES_PER_STEP + r], 0, 0)

    per_b3 = lambda b, j, pt: (b, 0, 0)
    idx_pages = [pl.BlockSpec((None, PAGE_SIZE, IDX_DIM), page_map(r)) for r in range(PAGES_PER_STEP)]
    keys, knew, thr = pl.pallas_call(
        functools.partial(_sample_index_body, ktop=ktop, nq=T, n_steps=n_steps),
        grid_spec=pltpu.PrefetchScalarGridSpec(
            num_scalar_prefetch=1,
            grid=(B, n_steps),
            in_specs=[pl.BlockSpec((None, IDX_HEADS * T, IDX_DIM), per_b3),
                      pl.BlockSpec((None, IDX_HEADS * T, 1), per_b3),
                      pl.BlockSpec((None, PAGE_SIZE, IDX_DIM), per_b3)] + idx_pages,
            out_specs=[pl.BlockSpec((None, T, past), per_b3),
                       pl.BlockSpec((None, T, PAGE_SIZE), per_b3),
                       pl.BlockSpec((None, T, PAGE_SIZE), per_b3)],
            scratch_shapes=[pltpu.VMEM((n_steps + 1, T, step_w), jnp.int32)]),
        out_shape=[jax.ShapeDtypeStruct((B, T, past), jnp.int32),
                   jax.ShapeDtypeStruct((B, T, PAGE_SIZE), jnp.int32),
                   jax.ShapeDtypeStruct((B, T, PAGE_SIZE), jnp.int32)],
        compiler_params=pltpu.CompilerParams(dimension_semantics=("arbitrary", "arbitrary")),
        name="sample_index",
        interpret=interpret,
    )(page_table, qi_s, w_s, ki_new, *([cache_idx_k] * PAGES_PER_STEP))

    kv_pages = [pl.BlockSpec((None, slot_rows, HEAD_DIM), page_map(r)) for r in range(PAGES_PER_STEP)]
    o = pl.pallas_call(
        functools.partial(_sample_attn_body, nq=T, n_steps=n_steps),
        grid_spec=pltpu.PrefetchScalarGridSpec(
            num_scalar_prefetch=1,
            grid=(B, n_steps),
            in_specs=[pl.BlockSpec((None, KV_HEADS, GROUP * T, HEAD_DIM), lambda b, j, pt: (b, 0, 0, 0)),
                      pl.BlockSpec((None, T, step_w), lambda b, j, pt: (b, 0, j)),
                      pl.BlockSpec((None, T, PAGE_SIZE), per_b3),
                      pl.BlockSpec((None, T, PAGE_SIZE), per_b3),
                      pl.BlockSpec((None, slot_rows, HEAD_DIM), per_b3),
                      pl.BlockSpec((None, slot_rows, HEAD_DIM), per_b3)] + kv_pages + kv_pages,
            out_specs=pl.BlockSpec((None, KV_HEADS, GROUP * T, HEAD_DIM), lambda b, j, pt: (b, 0, 0, 0)),
            scratch_shapes=[pltpu.VMEM((KV_HEADS, GROUP * T, 1), f32),
                            pltpu.VMEM((KV_HEADS, GROUP * T, 1), f32),
                            pltpu.VMEM((KV_HEADS, GROUP * T, HEAD_DIM), f32)]),
        out_shape=jax.ShapeDtypeStruct((B, KV_HEADS, GROUP * T, HEAD_DIM), f32),
        compiler_params=pltpu.CompilerParams(dimension_semantics=("arbitrary", "arbitrary")),
        name="sample_attn",
        interpret=interpret,
    )(page_table, q_s, keys, knew, thr, k_new, v_new, *([ck] * PAGES_PER_STEP), *([cv] * PAGES_PER_STEP))
    o = o.reshape(B, KV_HEADS, GROUP, T, HEAD_DIM)
    return jnp.transpose(o, (0, 3, 1, 2, 4)).reshape(B, T, ATTN_HEADS * HEAD_DIM)


def _dsa_sample(q, k, v, qi, ki, wi, cache_k, cache_v, cache_idx_k, page_table):
    B, T = q.shape[:2]
    past = page_table.shape[1] * PAGE_SIZE
    L = past + T
    ktop = min(INDEX_TOPK, L // 4)
    ki_past = cache_idx_k[page_table].reshape(B, past, IDX_DIM).astype(ki.dtype)
    ki_all = jnp.concatenate([ki_past, ki], axis=1)
    tpos = past + jnp.arange(T)
    sc = _index_scores(qi, wi, ki_all)
    sc = jnp.where(jnp.arange(L)[None, None, :] <= tpos[None, :, None], sc, -jnp.inf)
    _, idx = lax.top_k(sc, ktop)
    valid = idx <= tpos[None, :, None]
    bidx = jnp.arange(B)[:, None, None]
    in_past = (idx < past)[..., None, None]
    pidx = jnp.minimum(idx, past - 1)
    phys = page_table[bidx, pidx // PAGE_SIZE]
    off = pidx % PAGE_SIZE
    nidx = jnp.clip(idx - past, 0, T - 1)
    kg = jnp.where(in_past, cache_k[phys, off].astype(k.dtype), k[bidx, nidx])
    vg = jnp.where(in_past, cache_v[phys, off].astype(v.dtype), v[bidx, nidx])
    return _gathered_attention(q, kg, vg, valid)


def _causal_conv(x, buf, conv_w):
    T = x.shape[1]
    xp = jnp.concatenate([buf.astype(x.dtype), x], axis=1)
    y = xp[:, 0:T] * conv_w[0]
    for j in range(1, CONV_W):
        y = y + xp[:, j:j + T] * conv_w[j]
    return jax.nn.silu(y), xp[:, T:]


def _chunk_gated_delta(q, k, v, g, beta, S0):
    B, T, H, DK = q.shape
    DV = v.shape[-1]
    C = min(GDN_CHUNK, T)
    n = -(-T // C)
    pad = n * C - T

    def prep(a):
        a = jnp.pad(a, [(0, 0), (0, pad)] + [(0, 0)] * (a.ndim - 2))
        a = a.reshape((B, n, C) + a.shape[2:])
        return jnp.swapaxes(jnp.moveaxis(a, 1, 0), 2, 3)

    q, k, v, g, beta = prep(q), prep(k), prep(v), prep(g), prep(beta)
    gc = jnp.cumsum(g, axis=-1)
    causal = jnp.tril(jnp.ones((C, C), bool))
    strict = jnp.tril(jnp.ones((C, C), bool), -1)
    decay = jnp.exp(jnp.where(causal, gc[..., :, None] - gc[..., None, :], -jnp.inf))
    kb = k * beta[..., None]
    lower = jnp.where(strict, jnp.einsum('...id,...jd->...ij', kb, k) * decay, 0.0)
    amat = lower + jnp.eye(C, dtype=jnp.float32)
    rhs = jnp.concatenate([v * beta[..., None], kb * jnp.exp(gc)[..., None]], axis=-1)
    sol = lax.linalg.triangular_solve(amat, rhs, left_side=True, lower=True, unit_diagonal=True)
    u, w = sol[..., :DV], sol[..., DV:]
    qk = jnp.einsum('...id,...jd->...ij', q, k) * decay
    qg = q * jnp.exp(gc)[..., None]
    kd = k * jnp.exp(gc[..., -1:] - gc)[..., None]
    g_last = jnp.exp(gc[..., -1])

    def step(S, xs):
        qg_i, kd_i, u_i, w_i, qk_i, gl_i = xs
        v_new = u_i - jnp.einsum('bhck,bhkv->bhcv', w_i, S)
        o = jnp.einsum('bhck,bhkv->bhcv', qg_i, S) + jnp.einsum('bhcj,bhjv->bhcv', qk_i, v_new)
        S = S * gl_i[..., None, None] + jnp.einsum('bhck,bhcv->bhkv', kd_i, v_new)
        return S, o

    S, o = lax.scan(step, S0, (qg, kd, u, w, qk, g_last))
    o = jnp.moveaxis(jnp.swapaxes(o, 2, 3), 0, 1).reshape(B, n * C, H, DV)[:, :T]
    return o, S


def _gated_deltanet(qkv, b_raw, a_raw, gate, conv_buf, S0, conv_w, a_log, dt_bias, gdn_norm_w):
    B, T, _ = qkv.shape
    f32 = jnp.float32
    conv, new_buf = _causal_conv(qkv, conv_buf, conv_w)
    qc, kc, vc = jnp.split(conv, [GDN_HEADS * GDN_DK, 2 * GDN_HEADS * GDN_DK], axis=-1)
    q = _l2norm(qc.reshape(B, T, GDN_HEADS, GDN_DK).astype(f32)) * (GDN_DK ** -0.5)
    k = _l2norm(kc.reshape(B, T, GDN_HEADS, GDN_DK).astype(f32))
    v = vc.reshape(B, T, GDN_HEADS, GDN_DV).astype(f32)
    beta = jax.nn.sigmoid(b_raw.astype(f32))
    g = -jnp.exp(a_log.astype(f32)) * jax.nn.softplus(a_raw.astype(f32) + dt_bias.astype(f32))
    o, S = _chunk_gated_delta(q, k, v, g, beta, S0)
    o = _rmsnorm(o, gdn_norm_w) * jax.nn.silu(gate.reshape(B, T, GDN_HEADS, GDN_DV).astype(f32))
    return o.reshape(B, T, GDN_HEADS * GDN_DV).astype(qkv.dtype), S, new_buf


def _peer(xn, peer_wq, peer_keys, peer_u, peer_v):
    shape = xn.shape
    xf = xn.reshape(-1, D_MODEL)
    N = xf.shape[0]
    nb = -(-N // PEER_BLOCK)
    xb = jnp.pad(xf, ((0, nb * PEER_BLOCK - N), (0, 0))).reshape(nb, PEER_BLOCK, D_MODEL)
    ncand = PEER_TOPK * PEER_TOPK

    def blk(x):
        qh = (x @ peer_wq).reshape(PEER_BLOCK, PEER_HEADS, 2, PEER_QDIM // 2)
        s1 = jnp.einsum('thd,hkd->thk', qh[:, :, 0], peer_keys[0]).astype(jnp.float32)
        s2 = jnp.einsum('thd,hkd->thk', qh[:, :, 1], peer_keys[1]).astype(jnp.float32)
        v1, i1 = lax.top_k(s1, PEER_TOPK)
        v2, i2 = lax.top_k(s2, PEER_TOPK)
        cand = (v1[..., :, None] + v2[..., None, :]).reshape(PEER_BLOCK, PEER_HEADS, ncand)
        cidx = (i1[..., :, None] * PEER_NKEYS + i2[..., None, :]).reshape(PEER_BLOCK, PEER_HEADS, ncand)
        sv, si = lax.top_k(cand, PEER_TOPK)
        eidx = jnp.take_along_axis(cidx, si, axis=-1)
        gsm = jax.nn.softmax(sv, axis=-1)
        act = jax.nn.gelu(jnp.einsum('thkd,td->thk', peer_u[eidx], x).astype(jnp.float32), approximate=False)
        return jnp.einsum('thk,thkd->td', (gsm * act).astype(x.dtype), peer_v[eidx])

    y = lax.map(blk, xb).reshape(nb * PEER_BLOCK, D_MODEL)[:N]
    return y.reshape(shape)


GDN_ROWS = 512
_TN = (((0,), (0,)), ((), ()))


def _mm(a, b):
    return jnp.dot(a.astype(jnp.bfloat16), b.astype(jnp.bfloat16), preferred_element_type=jnp.float32)


def _split2(a):
    hi = a.astype(jnp.bfloat16)
    lo = (a - hi.astype(jnp.float32)).astype(jnp.bfloat16)
    return hi, lo


def _mm3(a, b):
    f32 = jnp.float32
    a1, a2 = _split2(a)
    b1, b2 = _split2(b)
    return (jnp.dot(a1, b1, preferred_element_type=f32) + jnp.dot(a1, b2, preferred_element_type=f32)
            + jnp.dot(a2, b1, preferred_element_type=f32))


def _split3(a):
    f32 = jnp.float32
    p1 = a.astype(jnp.bfloat16)
    r = a - p1.astype(f32)
    p2 = r.astype(jnp.bfloat16)
    p3 = (r - p2.astype(f32)).astype(jnp.bfloat16)
    return p1, p2, p3


def _sigmoid(x):
    return 1.0 / (1.0 + jnp.exp(-x))


def _softplus(x):
    return jnp.maximum(x, 0.0) + jnp.log1p(jnp.exp(-jnp.abs(x)))


def _gdn_body(x_ref, araw_ref, braw_ref, arawT_ref, gate_ref, buf_ref, s0_ref, cw_ref,
              alog_ref, dtb_ref, alogT_ref, dtbT_ref, nw_ref,
              ob_ref, sout_ref, xp_scr, y_scr, s_scr, *, rb, t_valid, t_pad):
    f32, bf16 = jnp.float32, jnp.bfloat16
    C = GDN_CHUNK
    j = pl.program_id(1)

    @pl.when(j == 0)
    def _():
        xp_scr[5:8, :] = buf_ref[...]
        s_scr[...] = s0_ref[...]

    xp_scr[8:8 + rb, :] = x_ref[...]
    y = xp_scr[5:5 + rb, :] * cw_ref[0:1, :]
    for t in range(1, CONV_W):
        y = y + xp_scr[5 + t:5 + t + rb, :] * cw_ref[t:t + 1, :]
    y_scr[...] = y * _sigmoid(y)
    xp_scr[5:8, :] = xp_scr[rb + 5:rb + 8, :]

    ri = lax.broadcasted_iota(jnp.int32, (C, C), 0)
    ci = lax.broadcasted_iota(jnp.int32, (C, C), 1)
    tri_incl = jnp.where(ri >= ci, 1.0, 0.0).astype(bf16)
    tri_inclT = jnp.where(ci >= ri, 1.0, 0.0).astype(bf16)
    causal = ri >= ci
    strict = ri > ci
    eye = jnp.where(ri == ci, 1.0, 0.0)
    neg_a = -jnp.exp(alog_ref[...])
    neg_aT = -jnp.exp(alogT_ref[...])
    q_scale = GDN_DK ** -0.5

    def chunk(c, carry):
        r0 = pl.multiple_of(c * C, C)
        g_c = neg_a * _softplus(araw_ref[pl.ds(r0, C), :] + dtb_ref[...])
        beta_c = _sigmoid(braw_ref[pl.ds(r0, C), :])
        g_r = neg_aT * _softplus(arawT_ref[c] + dtbT_ref[...])
        if t_valid < t_pad:
            base = j * rb + r0
            row_ok = (base + lax.broadcasted_iota(jnp.int32, (C, GDN_HEADS), 0)) < t_valid
            col_ok = (base + lax.broadcasted_iota(jnp.int32, (GDN_HEADS, C), 1)) < t_valid
            g_c = jnp.where(row_ok, g_c, 0.0)
            beta_c = jnp.where(row_ok, beta_c, 0.0)
            g_r = jnp.where(col_ok, g_r, 0.0)
        gc_c = sum(jnp.dot(tri_incl, p, preferred_element_type=f32) for p in _split3(g_c))
        gc_r = sum(jnp.dot(p, tri_inclT, preferred_element_type=f32) for p in _split3(g_r))
        H = range(GDN_HEADS)
        q, k, v, gcol, glast, beta, decay, eg = [], [], [], [], [], [], [], []
        for h in H:
            lo = h * GDN_DK
            qh = y_scr[pl.ds(r0, C), lo:lo + GDN_DK]
            kh = y_scr[pl.ds(r0, C), GDN_HEADS * GDN_DK + lo:GDN_HEADS * GDN_DK + lo + GDN_DK]
            v.append(y_scr[pl.ds(r0, C),
                           2 * GDN_HEADS * GDN_DK + h * GDN_DV:2 * GDN_HEADS * GDN_DK + (h + 1) * GDN_DV])
            q.append(qh * lax.rsqrt(jnp.sum(qh * qh, axis=-1, keepdims=True) + NORM_EPS) * q_scale)
            k.append(kh * lax.rsqrt(jnp.sum(kh * kh, axis=-1, keepdims=True) + NORM_EPS))
            gcol.append(gc_c[:, h:h + 1])
            glast.append(gc_c[C - 1:C, h:h + 1])
            beta.append(beta_c[:, h:h + 1])
            decay.append(jnp.exp(jnp.where(causal, gcol[h] - gc_r[h:h + 1, :], -jnp.inf)))
            eg.append(jnp.exp(gcol[h]))
        kb = [k[h] * beta[h] for h in H]
        kk = [lax.dot_general(kb[h].astype(bf16), k[h].astype(bf16), _NT, preferred_element_type=f32) for h in H]
        qk = [lax.dot_general(q[h].astype(bf16), k[h].astype(bf16), _NT, preferred_element_type=f32) * decay[h]
              for h in H]
        pw = [jnp.where(strict, -(kk[h] * decay[h]), 0.0) for h in H]
        inv = [eye + pw[h] for h in H]
        for _ in range(5):
            pw = [_mm3(pw[h], pw[h]) for h in H]
            inv = [inv[h] + _mm3(inv[h], pw[h]) for h in H]
        sol = [_mm3(inv[h], jnp.concatenate([v[h] * beta[h], kb[h] * eg[h]], axis=1)) for h in H]
        s_old = [s_scr[h] for h in H]
        v_new = [sol[h][:, :GDN_DV] - _mm(sol[h][:, GDN_DV:], s_old[h]) for h in H]
        o = [_mm(q[h] * eg[h], s_old[h]) + _mm(qk[h], v_new[h]) for h in H]
        for h in H:
            kd = k[h] * jnp.exp(glast[h] - gcol[h])
            s_scr[h] = s_old[h] * jnp.exp(glast[h]) + lax.dot_general(
                kd.astype(bf16), v_new[h].astype(bf16), _TN, preferred_element_type=f32)
        for h in H:
            on = o[h] * lax.rsqrt(jnp.mean(o[h] * o[h], axis=-1, keepdims=True) + NORM_EPS) * nw_ref[...]
            gt = gate_ref[pl.ds(r0, C), h * GDN_DV:(h + 1) * GDN_DV]
            ob_ref[pl.ds(r0, C), h * GDN_DV:(h + 1) * GDN_DV] = on * (gt * _sigmoid(gt))
        return carry

    lax.fori_loop(0, rb // C, chunk, 0)

    @pl.when(j == pl.num_programs(1) - 1)
    def _():
        sout_ref[...] = s_scr[...]


def _gdn_pallas(qkv, b_raw, a_raw, gate, conv_buf, S0, conv_w, a_log, dt_bias, gdn_norm_w,
                qkv_blk=0, gate_blk=0, interpret=False):
    B, T, _ = qkv.shape
    assert T >= CONV_W - 1
    f32 = jnp.float32
    C = GDN_CHUNK
    t_pad = -(-T // C) * C
    rb = min(GDN_ROWS, t_pad)
    assert t_pad % rb == 0
    pad = lambda a: jnp.pad(a, ((0, 0), (0, t_pad - T), (0, 0)))
    x, a_p, b_p = pad(qkv), pad(a_raw), pad(b_raw)
    gate_p = x if gate is qkv else pad(gate)
    a_t = jnp.transpose(a_p.reshape(B, t_pad // C, C, GDN_HEADS), (0, 1, 3, 2))
    hd = GDN_HEADS * GDN_DV
    row_blk = lambda b, j: (b, j, 0)
    fix2 = lambda b, j: (0, 0)
    ob, s_out = pl.pallas_call(
        functools.partial(_gdn_body, rb=rb, t_valid=T, t_pad=t_pad),
        grid=(B, t_pad // rb),
        in_specs=[pl.BlockSpec((None, rb, CONV_DIM), lambda b, j: (b, j, qkv_blk)),
                  pl.BlockSpec((None, rb, GDN_HEADS), row_blk),
                  pl.BlockSpec((None, rb, GDN_HEADS), row_blk),
                  pl.BlockSpec((None, rb // C, GDN_HEADS, C), lambda b, j: (b, j, 0, 0)),
                  pl.BlockSpec((None, rb, hd), lambda b, j: (b, j, gate_blk)),
                  pl.BlockSpec((None, CONV_W - 1, CONV_DIM), lambda b, j: (b, 0, 0)),
                  pl.BlockSpec((None, GDN_HEADS, GDN_DK, GDN_DV), lambda b, j: (b, 0, 0, 0)),
                  pl.BlockSpec((CONV_W, CONV_DIM), fix2),
                  pl.BlockSpec((1, GDN_HEADS), fix2),
                  pl.BlockSpec((1, GDN_HEADS), fix2),
                  pl.BlockSpec((GDN_HEADS, 1), fix2),
                  pl.BlockSpec((GDN_HEADS, 1), fix2),
                  pl.BlockSpec((1, GDN_DV), fix2)],
        out_specs=[pl.BlockSpec((None, rb, hd), row_blk),
                   pl.BlockSpec((None, GDN_HEADS, GDN_DK, GDN_DV), lambda b, j: (b, 0, 0, 0))],
        out_shape=[jax.ShapeDtypeStruct((B, t_pad, hd), f32),
                   jax.ShapeDtypeStruct((B, GDN_HEADS, GDN_DK, GDN_DV), f32)],
        scratch_shapes=[pltpu.VMEM((rb + 8, CONV_DIM), f32),
                        pltpu.VMEM((rb, CONV_DIM), f32),
                        pltpu.VMEM((GDN_HEADS, GDN_DK, GDN_DV), f32)],
        compiler_params=pltpu.CompilerParams(
            dimension_semantics=("arbitrary", "arbitrary"),
            vmem_limit_bytes=56 * 1024 * 1024),
        name="gated_deltanet",
        interpret=interpret,
    )(x, a_p, b_p, a_t, gate_p, conv_buf, S0, conv_w,
      a_log.reshape(1, GDN_HEADS), dt_bias.reshape(1, GDN_HEADS),
      a_log.reshape(GDN_HEADS, 1), dt_bias.reshape(GDN_HEADS, 1), gdn_norm_w.reshape(1, GDN_DV))
    return ob[:, :T], s_out, qkv[:, T - (CONV_W - 1):, qkv_blk * CONV_DIM:(qkv_blk + 1) * CONV_DIM]


_SQRT_HALF = 0.7071067811865476


def _top_rows(x, k):
    R, n = x.shape
    ri = lax.broadcasted_iota(jnp.int32, (R, n), 0).astype(jnp.float32)
    ki = lax.broadcasted_iota(jnp.int32, (k, n), 0)

    def body(r, c):
        x, out = c
        m = x.max(axis=0, keepdims=True)
        first = jnp.min(jnp.where(x == m, ri, float(R)), axis=0, keepdims=True)
        x = jnp.where(ri == first, -jnp.inf, x)
        out = jnp.where(ki == r, m, out)
        return x, out

    _, out = lax.fori_loop(0, k, body, (x, jnp.zeros((k, n), jnp.float32)))
    return out


def _peer_front_body(x_ref, oa_ref, ob_ref, wo_ref, g_ref, wq_ref, keys_ref,
                     h_ref, xn_ref, s1_ref, s2_ref, e2_ref, aux_ref):
    f32, bf16 = jnp.float32, jnp.bfloat16
    half_w = ATTN_HEADS * HEAD_DIM
    h = (x_ref[...]
         + jnp.dot(oa_ref[...].astype(bf16), wo_ref[:half_w, :], preferred_element_type=f32)
         + jnp.dot(ob_ref[...].astype(bf16), wo_ref[half_w:, :], preferred_element_type=f32))
    h_ref[...] = h
    xn = (h * lax.rsqrt(jnp.mean(h * h, axis=-1, keepdims=True) + NORM_EPS) * g_ref[...]).astype(bf16)
    xn_ref[...] = xn
    qh = jnp.dot(xn, wq_ref[...], preferred_element_type=f32).astype(bf16)
    tq = qh.shape[0]
    hq = PEER_QDIM // 2
    for hh in range(PEER_HEADS):
        tops = []
        for half in range(2):
            col = (hh * 2 + half) * hq
            sT = lax.dot_general(keys_ref[half, hh], qh[:, col:col + hq], _NT,
                                 preferred_element_type=f32)
            (s1_ref if half == 0 else s2_ref)[hh] = sT
            tops.append(_top_rows(sT, PEER_TOPK))
        a16, b16 = tops
        cand = jnp.concatenate(
            [a16[r:r + 1, :] + b16[0:8, :] for r in range(8)]
            + [a16[0:1, :] + b16[8:16, :], a16[8:16, :] + b16[0:1, :]], axis=0)
        tau = _top_rows(cand, PEER_TOPK)[PEER_TOPK - 1:PEER_TOPK, :]
        top_sum = a16[0:1, :] + b16[0:1, :]
        z = jnp.sum(jnp.where(cand >= tau, jnp.exp(cand - top_sum), 0.0), axis=0, keepdims=True)
        e2_ref[hh] = jnp.exp(s2_ref[hh] - b16[0:1, :]) / z
        aux_ref[hh] = jnp.concatenate([tau, a16[0:1, :], jnp.zeros((6, tq), f32)], axis=0)


def _peer_front(x, oa, ob, wo, g, wq, keys, tq, interpret=False):
    n = x.shape[0]
    half_w = ATTN_HEADS * HEAD_DIM
    tok = lambda i: (i, 0)
    fix2 = lambda i: (0, 0)
    colT = lambda i: (0, 0, i)
    f32 = jnp.float32
    plane = jax.ShapeDtypeStruct((PEER_HEADS, PEER_NKEYS, n), f32)
    plane_spec = pl.BlockSpec((PEER_HEADS, PEER_NKEYS, tq), colT)
    return pl.pallas_call(
        _peer_front_body,
        grid=(n // tq,),
        in_specs=[pl.BlockSpec((tq, D_MODEL), tok),
                  pl.BlockSpec((tq, half_w), tok),
                  pl.BlockSpec((tq, half_w), tok),
                  pl.BlockSpec((D_MODEL, D_MODEL), fix2),
                  pl.BlockSpec((1, D_MODEL), fix2),
                  pl.BlockSpec((D_MODEL, PEER_HEADS * PEER_QDIM), fix2),
                  pl.BlockSpec((2, PEER_HEADS, PEER_NKEYS, PEER_QDIM // 2), lambda i: (0, 0, 0, 0))],
        out_specs=[pl.BlockSpec((tq, D_MODEL), tok),
                   pl.BlockSpec((tq, D_MODEL), tok),
                   plane_spec, plane_spec, plane_spec,
                   pl.BlockSpec((PEER_HEADS, 8, tq), colT)],
        out_shape=[jax.ShapeDtypeStruct((n, D_MODEL), f32),
                   jax.ShapeDtypeStruct((n, D_MODEL), jnp.bfloat16),
                   plane, plane, plane,
                   jax.ShapeDtypeStruct((PEER_HEADS, 8, n), f32)],
        compiler_params=pltpu.CompilerParams(
            dimension_semantics=("arbitrary",),
            vmem_limit_bytes=56 * 1024 * 1024),
        name="peer_front",
        interpret=interpret,
    )(x, oa, ob, wo, g.reshape(1, D_MODEL), wq, keys)


def _peer_dense_body(xn_ref, u_ref, vT_ref, s1_ref, s2_ref, e2_ref, aux_ref, yT_ref, *, eblk):
    f32 = jnp.float32
    eb = pl.program_id(1)

    @pl.when(eb == 0)
    def _():
        yT_ref[...] = jnp.zeros(yT_ref.shape, f32)

    a = lax.dot_general(u_ref[...], xn_ref[...], _NT, preferred_element_type=f32)
    act = 0.5 * a * (1.0 + lax.erf(a * _SQRT_HALF))
    sub = eblk // PEER_NKEYS
    pieces = []
    for r in range(sub):
        i1 = eb * sub + r
        gate = jnp.zeros((PEER_NKEYS, a.shape[1]), f32)
        for hh in range(PEER_HEADS):
            s1row = s1_ref[hh, pl.ds(i1, 1), :]
            tau = aux_ref[hh, 0:1, :]
            e1row = jnp.exp(s1row - aux_ref[hh, 1:2, :])
            gate = gate + jnp.where(s1row + s2_ref[hh] >= tau, e1row * e2_ref[hh], 0.0)
        pieces.append((gate * act[r * PEER_NKEYS:(r + 1) * PEER_NKEYS, :]).astype(jnp.bfloat16))
    hT = jnp.concatenate(pieces, axis=0)
    yT_ref[...] += jnp.dot(vT_ref[...], hT, preferred_element_type=f32)


def _peer_dense(xn, u, vT, s1, s2, e2, aux, tq, eblk, interpret=False):
    n = xn.shape[0]
    ne = u.shape[0]
    plane_spec = pl.BlockSpec((PEER_HEADS, PEER_NKEYS, tq), lambda i, e: (0, 0, i))
    return pl.pallas_call(
        functools.partial(_peer_dense_body, eblk=eblk),
        grid=(n // tq, ne // eblk),
        in_specs=[pl.BlockSpec((tq, D_MODEL), lambda i, e: (i, 0)),
                  pl.BlockSpec((eblk, D_MODEL), lambda i, e: (e, 0)),
                  pl.BlockSpec((D_MODEL, eblk), lambda i, e: (0, e)),
                  plane_spec, plane_spec, plane_spec,
                  pl.BlockSpec((PEER_HEADS, 8, tq), lambda i, e: (0, 0, i))],
        out_specs=pl.BlockSpec((D_MODEL, tq), lambda i, e: (0, i)),
        out_shape=jax.ShapeDtypeStruct((D_MODEL, n), jnp.float32),
        compiler_params=pltpu.CompilerParams(
            dimension_semantics=("arbitrary", "arbitrary"),
            vmem_limit_bytes=56 * 1024 * 1024),
        name="peer_dense",
        interpret=interpret,
    )(xn, u, vT, s1, s2, e2, aux)


def _layer_out_pallas(x, oa, ob, wo_b, ffn_norm_w, wq_b, keys_b, u_b, vT_b, interpret=False):
    n = x.shape[0]
    tq1 = min(256, n)
    tq2 = 512 if n % 512 == 0 else min(256, n)
    h, xn, s1, s2, e2, aux = _peer_front(x, oa, ob, wo_b, ffn_norm_w, wq_b, keys_b, tq1, interpret)
    yT = _peer_dense(xn, u_b, vT_b, s1, s2, e2, aux, tq2, 512, interpret)
    return h + yT.T


def _layer_out(x, oa, ob, w_out, ffn_norm_w, peer_wq, peer_keys, peer_u, peer_v):
    h = x + jnp.concatenate([oa, ob], axis=-1) @ w_out
    return h + _peer(_rmsnorm(h, ffn_norm_w), peer_wq, peer_keys, peer_u, peer_v)


def kernel(x_prompt, x_sample, cache_k, cache_v, cache_idx_k, state_ssm, state_conv, page_table,
           attn_norm_w, w_in, q_norm_w, k_norm_w, idx_k_norm_w, conv_w, a_log, dt_bias, gdn_norm_w,
           w_out, ffn_norm_w, peer_wq, peer_keys, peer_u, peer_v):
    l = 0
    proj_w = (attn_norm_w[l], _permute_w_in(w_in[l]), q_norm_w[l], k_norm_w[l], idx_k_norm_w[l])
    gdn_cols = dict(qkv_blk=_DST["qkv"] // CONV_DIM, gate_blk=_DST["gate"] // (GDN_HEADS * GDN_DV))
    gdn_w = (conv_w[l], a_log[l], dt_bias[l], gdn_norm_w[l])
    bf16 = jnp.bfloat16
    out_w = (w_out[l].astype(bf16), ffn_norm_w[l], peer_wq[l].astype(bf16), peer_keys[l].astype(bf16),
             peer_u[l].astype(bf16), peer_v[l].astype(bf16).T)

    hp, hs = x_prompt, x_sample
    (qa, ka, va, qi, ki, wi), (z, b_raw, a_raw) = _in_projection(hp, *proj_w)
    oa = _dsa_prompt_pallas(qa[0], ka[0], va[0], qi[0], ki[0], wi[0])[None]
    Bp = hp.shape[0]
    buf0 = jnp.zeros((Bp, CONV_W - 1, CONV_DIM), hp.dtype)
    S0 = jnp.zeros((Bp, GDN_HEADS, GDN_DK, GDN_DV), jnp.float32)
    ob, S_p, buf_p = _gdn_pallas(z, b_raw, a_raw, z, buf0, S0, *gdn_w, **gdn_cols)
    half_w = ATTN_HEADS * HEAD_DIM
    hp = _layer_out_pallas(hp[0], oa[0], ob[0], *out_w)[None]
    kp, vp, ip = ka, va, ki

    (qa, ka, va, qi, ki, wi), (z, b_raw, a_raw) = _in_projection(hs, *proj_w)
    oa = _dsa_sample_pallas(qa, ka, va, qi, ki, wi, cache_k[l], cache_v[l], cache_idx_k[l], page_table)
    ob, S_s, buf_s = _gdn_pallas(z, b_raw, a_raw, z, state_conv[l], state_ssm[l].astype(jnp.float32),
                                 *gdn_w, **gdn_cols)
    ns = hs.shape[0] * hs.shape[1]
    hs = _layer_out_pallas(hs.reshape(ns, D_MODEL), oa.reshape(ns, half_w), ob.reshape(ns, half_w),
                           *out_w).reshape(hs.shape)

    return (hp, hs, kp[None], vp[None], ip[None], S_p[None], buf_p[None],
            ka[None], va[None], ki[None], S_s[None], buf_s[None])
```

```python
import functools
import math

import jax
import jax.numpy as jnp
from jax import lax
from jax.experimental import pallas as pl
from jax.experimental.pallas import tpu as pltpu

D_MODEL = 2048
PAGE_SIZE = 128
HEAD_DIM = 128
ATTN_HEADS = 8
KV_HEADS = 2
GROUP = ATTN_HEADS // KV_HEADS
IDX_HEADS = 16
IDX_DIM = 64
INDEX_TOPK = 256
IDX_SCALE = (IDX_HEADS * IDX_DIM) ** -0.5
GDN_DK = 128
GDN_DV = 128
GDN_HEADS = 8
CONV_W = 4
CONV_DIM = GDN_HEADS * (2 * GDN_DK + GDN_DV)
GDN_CHUNK = 64
IN_SIZES = (ATTN_HEADS * HEAD_DIM, KV_HEADS * HEAD_DIM, KV_HEADS * HEAD_DIM,
            IDX_HEADS * IDX_DIM, IDX_DIM, IDX_HEADS,
            CONV_DIM, GDN_HEADS, GDN_HEADS, GDN_HEADS * GDN_DV)
IN_COLS = sum(IN_SIZES)
PEER_HEADS = 8
PEER_NKEYS = 128
PEER_QDIM = 256
PEER_TOPK = 16
NORM_EPS = 1e-6

LANES = 128


def _rmsnorm(x, w):
    xf = x.astype(jnp.float32)
    y = xf * lax.rsqrt(jnp.mean(xf * xf, axis=-1, keepdims=True) + NORM_EPS)
    return (y * w.astype(jnp.float32)).astype(x.dtype)


def _norm_matmul_body(x_ref, g_ref, w_ref, o_ref, xn_ref):
    @pl.when(pl.program_id(1) == 0)
    def _():
        x = x_ref[...]
        r = lax.rsqrt(jnp.mean(x * x, axis=-1, keepdims=True) + NORM_EPS)
        xn_ref[...] = (x * r * g_ref[...]).astype(jnp.bfloat16)

    o_ref[...] = jnp.dot(xn_ref[...], w_ref[...], preferred_element_type=jnp.float32)


def _norm_matmul(x, g, w, tm, tn):
    m, k = x.shape
    n = w.shape[1]
    return pl.pallas_call(
        _norm_matmul_body,
        grid=(m // tm, n // tn),
        in_specs=[pl.BlockSpec((tm, k), lambda i, j: (i, 0)),
                  pl.BlockSpec((1, k), lambda i, j: (0, 0)),
                  pl.BlockSpec((k, tn), lambda i, j: (0, j))],
        out_specs=pl.BlockSpec((tm, tn), lambda i, j: (i, j)),
        out_shape=jax.ShapeDtypeStruct((m, n), jnp.float32),
        scratch_shapes=[pltpu.VMEM((tm, k), jnp.bfloat16)],
        compiler_params=pltpu.CompilerParams(
            dimension_semantics=("arbitrary", "arbitrary"),
            vmem_limit_bytes=48 * 1024 * 1024),
        name="norm_matmul",
    )(x, g.reshape(1, k), w)


PROJ_TILE = 1024
_SRC = dict(zip(("qa", "ka", "va", "qi", "ki", "wi", "qkv", "b", "a", "gate"),
                [(sum(IN_SIZES[:i]), IN_SIZES[i]) for i in range(len(IN_SIZES))]))
_DST_ORDER = (("qkv",), ("qa",), ("qi",), ("gate",), ("ka",), ("va",), ("ki",), ("wi", "b", "a"))


def _proj_layout():
    dst, off = {}, 0
    for group in _DST_ORDER:
        for name in group:
            dst[name] = off
            off += _SRC[name][1]
        off = -(-off // LANES) * LANES
    return dst, -(-off // PROJ_TILE) * PROJ_TILE


_DST, PROJ_COLS = _proj_layout()
assert _DST["qkv"] == 0 and _DST["gate"] % (GDN_HEADS * GDN_DV) == 0


def _permute_w_in(w_in):
    pieces, off = [], 0
    for group in _DST_ORDER:
        for name in group:
            if _DST[name] > off:
                pieces.append(jnp.zeros((D_MODEL, _DST[name] - off), w_in.dtype))
            s0, n = _SRC[name]
            pieces.append(w_in[:, s0:s0 + n])
            off = _DST[name] + n
    pieces.append(jnp.zeros((D_MODEL, PROJ_COLS - off), w_in.dtype))
    return jnp.concatenate(pieces, axis=1).astype(jnp.bfloat16)


def _in_projection(x, attn_norm_w, w_in_p, q_norm_w, k_norm_w, idx_k_norm_w):
    B, T, _ = x.shape
    m = B * T
    tm = 512 if m % 512 == 0 else m
    z = _norm_matmul(x.reshape(m, D_MODEL), attn_norm_w, w_in_p, tm, PROJ_TILE).reshape(B, T, PROJ_COLS)
    col = lambda name: z[:, :, _DST[name]:_DST[name] + _SRC[name][1]]
    qa = _rmsnorm(col("qa").reshape(B, T, ATTN_HEADS, HEAD_DIM), q_norm_w)
    ka = _rmsnorm(col("ka").reshape(B, T, KV_HEADS, HEAD_DIM), k_norm_w)
    va = col("va").reshape(B, T, KV_HEADS, HEAD_DIM)
    qi = col("qi").reshape(B, T, IDX_HEADS, IDX_DIM)
    ki = _rmsnorm(col("ki"), idx_k_norm_w)
    return (qa, ka, va, qi, ki, col("wi")), (z, col("b"), col("a"))


_INT_MIN = -2 ** 31
_NEG_BIG = -1e30
_NT = (((1,), (1,)), ((), ()))


def _dsa_prompt_body(qi_ref, wT_ref, q_ref, ki_ref, k_ref, vT_ref, o_ref,
                     keys_ref, m_ref, l_ref, acc_ref, *, ktop, tq):
    f32 = jnp.float32
    i = pl.program_id(0)
    nkb = i + 1
    col_t = i * tq + lax.broadcasted_iota(jnp.int32, (tq, tq), 1)
    row_s = lax.broadcasted_iota(jnp.int32, (tq, tq), 0)

    def score_blk(kb, carry):
        kib = ki_ref[kb]
        acc = jnp.zeros((tq, tq), f32)
        for h in range(IDX_HEADS):
            s = lax.dot_general(kib, qi_ref[h], _NT, preferred_element_type=f32)
            acc = acc + jnp.maximum(s, 0.0) * wT_ref[h:h + 1, :]
        sc = acc * IDX_SCALE
        bits = lax.bitcast_convert_type(sc, jnp.int32)
        key = bits ^ (lax.shift_right_arithmetic(bits, 31) & 0x7FFFFFFF)
        valid = (kb * tq + row_s) <= col_t
        keys_ref[kb] = jnp.where(valid, key, _INT_MIN)
        return carry

    lax.fori_loop(0, nkb, score_blk, 0)

    def bisect(it, ans_u):
        cand_u = ans_u | lax.shift_left(jnp.int32(1), 31 - it)
        cand_s = cand_u ^ _INT_MIN

        def count_blk(kb, cnt):
            hit = jnp.where(keys_ref[kb] >= cand_s, 1.0, 0.0)
            return cnt + hit.reshape(tq // 8, 8, tq).sum(axis=0)

        cnt = lax.fori_loop(0, nkb, count_blk, jnp.zeros((8, tq), f32))
        cnt = cnt.sum(axis=0, keepdims=True)
        return jnp.where(cnt >= ktop, cand_u, ans_u)

    ans_u = lax.fori_loop(0, 32, bisect, jnp.zeros((1, tq), jnp.int32))
    thr = jnp.maximum(ans_u ^ _INT_MIN, _INT_MIN + 1)

    def tally(kb, c):
        kk = keys_ref[kb]
        ge = jnp.where(kk >= thr, 1.0, 0.0).reshape(tq // 8, 8, tq).sum(axis=0)
        gt = jnp.where(kk > thr, 1.0, 0.0).reshape(tq // 8, 8, tq).sum(axis=0)
        return c[0] + ge, c[1] + gt

    z8 = jnp.zeros((8, tq), f32)
    n_ge, n_gt = lax.fori_loop(0, nkb, tally, (z8, z8))
    n_ge = n_ge.sum(axis=0, keepdims=True)
    n_gt = n_gt.sum(axis=0, keepdims=True)

    @pl.when(jnp.max(n_ge) > ktop)
    def _():
        need = jnp.where(n_ge >= ktop, ktop - n_gt, float(2 ** 30))
        idx_bits = max(1, (keys_ref.shape[0] * tq - 1).bit_length())

        def bisect_idx(it, pos):
            cand = pos | lax.shift_left(jnp.int32(1), idx_bits - 1 - it)

            def count_blk(kb, cnt):
                hit = (keys_ref[kb] == thr) & ((kb * tq + row_s) < cand)
                return cnt + jnp.where(hit, 1.0, 0.0).reshape(tq // 8, 8, tq).sum(axis=0)

            cnt = lax.fori_loop(0, nkb, count_blk, z8).sum(axis=0, keepdims=True)
            return jnp.where(cnt < need, cand, pos)

        last = lax.fori_loop(0, idx_bits, bisect_idx, jnp.zeros((1, tq), jnp.int32))

        def demote(kb, carry):
            kk = keys_ref[kb]
            keys_ref[kb] = jnp.where((kk == thr) & ((kb * tq + row_s) > last), kk - 1, kk)
            return carry

        lax.fori_loop(0, nkb, demote, 0)

    m_ref[...] = jnp.full(m_ref.shape, _NEG_BIG, f32)
    l_ref[...] = jnp.zeros(l_ref.shape, f32)
    acc_ref[...] = jnp.zeros(acc_ref.shape, f32)
    c2 = HEAD_DIM ** -0.5 * math.log2(math.e)

    def attn_blk(kb, carry):
        sel = keys_ref[kb] >= thr
        kblk = k_ref[kb]
        vT = vT_ref[kb]
        s_all = [lax.dot_general(kblk[:, (h // GROUP) * HEAD_DIM:(h // GROUP + 1) * HEAD_DIM], q_ref[h], _NT,
                                 preferred_element_type=f32) for h in range(ATTN_HEADS)]
        for h in range(ATTN_HEADS):
            n = h // GROUP
            s = jnp.where(sel, s_all[h], -jnp.inf)
            m_old = m_ref[h]
            m_new = jnp.maximum(m_old, s.max(axis=0, keepdims=True))
            p = jnp.exp2((s - m_new) * c2)
            alpha = jnp.exp2((m_old - m_new) * c2)
            l_ref[h] = alpha * l_ref[h] + p.sum(axis=0, keepdims=True)
            pv = jnp.dot(vT[n * HEAD_DIM:(n + 1) * HEAD_DIM, :], p.astype(jnp.bfloat16),
                         preferred_element_type=f32)
            acc_ref[h] = alpha * acc_ref[h] + pv
            m_ref[h] = m_new
        return carry

    lax.fori_loop(0, nkb, attn_blk, 0)
    for h in range(ATTN_HEADS):
        o_ref[:, h * HEAD_DIM:(h + 1) * HEAD_DIM] = (acc_ref[h] / l_ref[h]).T


def _dsa_prompt_pallas(q, k, v, qi, ki, wi, interpret=False):
    T = q.shape[0]
    tq = min(256, T)
    nb = T // tq
    ktop = min(INDEX_TOPK, T // 4)
    bf16 = jnp.bfloat16
    qh = jnp.transpose(q.astype(bf16), (1, 0, 2))
    qih = jnp.transpose(qi.astype(bf16), (1, 0, 2))
    wT = wi.astype(jnp.float32).T
    kib = ki.astype(bf16).reshape(nb, tq, IDX_DIM)
    kb = k.astype(bf16).reshape(nb, tq, KV_HEADS * HEAD_DIM)
    vT = jnp.transpose(v.astype(bf16).reshape(nb, tq, KV_HEADS * HEAD_DIM), (0, 2, 1))
    body = functools.partial(_dsa_prompt_body, ktop=ktop, tq=tq)
    return pl.pallas_call(
        body,
        grid=(nb,),
        in_specs=[pl.BlockSpec((IDX_HEADS, tq, IDX_DIM), lambda i: (0, i, 0)),
                  pl.BlockSpec((IDX_HEADS, tq), lambda i: (0, i)),
                  pl.BlockSpec((ATTN_HEADS, tq, HEAD_DIM), lambda i: (0, i, 0)),
                  pl.BlockSpec((nb, tq, IDX_DIM), lambda i: (0, 0, 0)),
                  pl.BlockSpec((nb, tq, KV_HEADS * HEAD_DIM), lambda i: (0, 0, 0)),
                  pl.BlockSpec((nb, KV_HEADS * HEAD_DIM, tq), lambda i: (0, 0, 0))],
        out_specs=pl.BlockSpec((tq, ATTN_HEADS * HEAD_DIM), lambda i: (i, 0)),
        out_shape=jax.ShapeDtypeStruct((T, ATTN_HEADS * HEAD_DIM), jnp.float32),
        scratch_shapes=[pltpu.VMEM((nb, tq, tq), jnp.int32),
                        pltpu.VMEM((ATTN_HEADS, 1, tq), jnp.float32),
                        pltpu.VMEM((ATTN_HEADS, 1, tq), jnp.float32),
                        pltpu.VMEM((ATTN_HEADS, HEAD_DIM, tq), jnp.float32)],
        compiler_params=pltpu.CompilerParams(
            dimension_semantics=("arbitrary",),
            vmem_limit_bytes=56 * 1024 * 1024),
        name="dsa_prompt",
        interpret=interpret,
    )(qih, wT, qh, kib, kb, vT)


PAGES_PER_STEP = 8


def _sortable_key(x):
    bits = lax.bitcast_convert_type(x, jnp.int32)
    return bits ^ (lax.shift_right_arithmetic(bits, 31) & 0x7FFFFFFF)


def _sample_index_body(pt_ref, qi_ref, w_ref, kin_ref, *rest, ktop, nq, n_steps):
    del pt_ref
    pages = rest[:PAGES_PER_STEP]
    keys_out, knew_out, thr_out, keys_scr = rest[PAGES_PER_STEP:]
    f32 = jnp.float32
    j = pl.program_id(1)
    step_w = PAGES_PER_STEP * PAGE_SIZE
    past = n_steps * step_w
    qi = qi_ref[...]
    w = w_ref[...]

    def page_keys(page):
        s = lax.dot_general(qi, page.astype(jnp.bfloat16), _NT, preferred_element_type=f32)
        s = jnp.maximum(s, 0.0) * w
        return s.reshape(IDX_HEADS, nq, PAGE_SIZE).sum(axis=0) * IDX_SCALE

    keys_scr[j] = jnp.concatenate([_sortable_key(page_keys(p[...])) for p in pages], axis=1)

    @pl.when(j == n_steps - 1)
    def _():
        kn = _sortable_key(page_keys(kin_ref[...]))
        col = lax.broadcasted_iota(jnp.int32, (nq, PAGE_SIZE), 1)
        row = lax.broadcasted_iota(jnp.int32, (nq, PAGE_SIZE), 0)
        kn = jnp.where(col <= row, kn, _INT_MIN)
        keys_scr[n_steps] = jnp.concatenate(
            [kn, jnp.full((nq, step_w - PAGE_SIZE), _INT_MIN, jnp.int32)], axis=1)
        chunks = [(st, c) for st in range(n_steps + 1) for c in range(PAGES_PER_STEP)]

        def chunk_of(st, c):
            return (keys_scr[st, :, c * PAGE_SIZE:(c + 1) * PAGE_SIZE],
                    st * step_w + c * PAGE_SIZE + col)

        def bisect(it, ans_u):
            cand_u = ans_u | lax.shift_left(jnp.int32(1), 31 - it)
            cand_s = cand_u ^ _INT_MIN
            cnt = jnp.zeros((nq, PAGE_SIZE), f32)
            for st in range(n_steps + 1):
                for c in range(PAGES_PER_STEP):
                    chunk = keys_scr[st, :, c * PAGE_SIZE:(c + 1) * PAGE_SIZE]
                    cnt = cnt + jnp.where(chunk >= cand_s, 1.0, 0.0)
            cnt = cnt.sum(axis=1, keepdims=True)
            return jnp.where(cnt >= ktop, cand_u, ans_u)

        ans_u = lax.fori_loop(0, 32, bisect, jnp.zeros((nq, 1), jnp.int32))
        thr = jnp.maximum(ans_u ^ _INT_MIN, _INT_MIN + 1)
        thr_out[...] = jnp.broadcast_to(thr, (nq, PAGE_SIZE))

        n_ge = jnp.zeros((nq, PAGE_SIZE), f32)
        n_gt = jnp.zeros((nq, PAGE_SIZE), f32)
        for st, c in chunks:
            kk, _ = chunk_of(st, c)
            n_ge = n_ge + jnp.where(kk >= thr, 1.0, 0.0)
            n_gt = n_gt + jnp.where(kk > thr, 1.0, 0.0)
        n_ge = n_ge.sum(axis=1, keepdims=True)
        n_gt = n_gt.sum(axis=1, keepdims=True)
        need = jnp.where(n_ge >= ktop, ktop - n_gt, float(2 ** 30))
        idx_bits = max(1, (past + PAGE_SIZE - 1).bit_length())

        def bisect_idx(it, pos):
            cand = pos | lax.shift_left(jnp.int32(1), idx_bits - 1 - it)
            cnt = jnp.zeros((nq, PAGE_SIZE), f32)
            for st, c in chunks:
                kk, idx = chunk_of(st, c)
                cnt = cnt + jnp.where(kk == thr, jnp.where(idx < cand, 1.0, 0.0), 0.0)
            cnt = cnt.sum(axis=1, keepdims=True)
            return jnp.where(cnt < need, cand, pos)

        last = lax.fori_loop(0, idx_bits, bisect_idx, jnp.zeros((nq, 1), jnp.int32))

        def demoted(st, c):
            kk, idx = chunk_of(st, c)
            return jnp.where(kk == thr, jnp.where(idx > last, kk - 1, kk), kk)

        for st in range(n_steps):
            for c in range(PAGES_PER_STEP):
                lo = st * step_w + c * PAGE_SIZE
                keys_out[:, lo:lo + PAGE_SIZE] = demoted(st, c)
        knew_out[...] = demoted(n_steps, 0)


def _sample_attn_body(pt_ref, q_ref, keys_ref, knew_ref, thr_ref, kn_ref, vn_ref, *rest, nq, n_steps):
    del pt_ref
    kp = rest[:PAGES_PER_STEP]
    vp = rest[PAGES_PER_STEP:2 * PAGES_PER_STEP]
    o_ref, m_ref, l_ref, acc_ref = rest[2 * PAGES_PER_STEP:]
    f32, bf16 = jnp.float32, jnp.bfloat16
    j = pl.program_id(1)
    scale = HEAD_DIM ** -0.5
    thr = thr_ref[...]

    def update(keys_q, thr_q, k_blocks, v_blocks):
        sel = jnp.concatenate([keys_q] * GROUP, axis=0) >= jnp.concatenate([thr_q] * GROUP, axis=0)
        for n in range(KV_HEADS):
            qn = q_ref[n]
            s = jnp.concatenate(
                [lax.dot_general(qn, kb[pl.ds(n, PAGE_SIZE, stride=KV_HEADS), :].astype(bf16), _NT,
                                 preferred_element_type=f32)
                 for kb in k_blocks], axis=1) * scale
            s = jnp.where(sel, s, _NEG_BIG)
            m_old = m_ref[n]
            m_new = jnp.maximum(m_old, s.max(axis=1, keepdims=True))
            p = jnp.where(sel, jnp.exp(s - m_new), 0.0)
            alpha = jnp.exp(m_old - m_new)
            l_ref[n] = alpha * l_ref[n] + p.sum(axis=1, keepdims=True)
            pb = p.astype(bf16)
            pv = jnp.zeros((GROUP * nq, HEAD_DIM), f32)
            for c, vb in enumerate(v_blocks):
                pv = pv + jnp.dot(pb[:, c * PAGE_SIZE:(c + 1) * PAGE_SIZE],
                                  vb[pl.ds(n, PAGE_SIZE, stride=KV_HEADS), :].astype(bf16),
                                  preferred_element_type=f32)
            acc_ref[n] = alpha * acc_ref[n] + pv
            m_ref[n] = m_new

    @pl.when(j == 0)
    def _():
        m_ref[...] = jnp.full(m_ref.shape, _NEG_BIG, f32)
        l_ref[...] = jnp.zeros(l_ref.shape, f32)
        acc_ref[...] = jnp.zeros(acc_ref.shape, f32)
        update(knew_ref[...], thr, [kn_ref], [vn_ref])

    thr_w = jnp.concatenate([thr] * PAGES_PER_STEP, axis=1)
    update(keys_ref[...], thr_w, kp, vp)

    @pl.when(j == n_steps - 1)
    def _():
        for n in range(KV_HEADS):
            o_ref[n] = acc_ref[n] / l_ref[n]


def _dsa_sample_pallas(q, k, v, qi, ki, wi, cache_k, cache_v, cache_idx_k, page_table, interpret=False):
    B, T = q.shape[:2]
    n_pages = page_table.shape[1]
    past = n_pages * PAGE_SIZE
    ktop = min(INDEX_TOPK, (past + T) // 4)
    n_steps = n_pages // PAGES_PER_STEP
    step_w = PAGES_PER_STEP * PAGE_SIZE
    f32, bf16 = jnp.float32, jnp.bfloat16
    kvw = KV_HEADS * HEAD_DIM
    qi_s = jnp.transpose(qi.astype(bf16), (0, 2, 1, 3)).reshape(B, IDX_HEADS * T, IDX_DIM)
    w_s = jnp.transpose(wi.astype(f32), (0, 2, 1)).reshape(B, IDX_HEADS * T, 1)
    pad_rows = lambda a: jnp.pad(a, ((0, 0), (0, PAGE_SIZE - T)) + ((0, 0),) * (a.ndim - 2))
    ki_new = pad_rows(ki.astype(f32))
    slot_rows = PAGE_SIZE * KV_HEADS
    k_new = pad_rows(k).reshape(B, slot_rows, HEAD_DIM)
    v_new = pad_rows(v).reshape(B, slot_rows, HEAD_DIM)
    q_s = jnp.transpose(q.astype(bf16).reshape(B, T, KV_HEADS, GROUP, HEAD_DIM),
                        (0, 2, 3, 1, 4)).reshape(B, KV_HEADS, GROUP * T, HEAD_DIM)
    ck = cache_k.reshape(cache_k.shape[0], slot_rows, HEAD_DIM)
    cv = cache_v.reshape(cache_v.shape[0], slot_rows, HEAD_DIM)

    def page_map(r):
        return lambda b, j, pt: (pt[b, j * PAGES_PER_STEP + r], 0, 0)

    per_b3 = lambda b, j, pt: (b, 0, 0)
    idx_pages = [pl.BlockSpec((None, PAGE_SIZE, IDX_DIM), page_map(r)) for r in range(PAGES_PER_STEP)]
    keys, knew, thr = pl.pallas_call(
        functools.partial(_sample_index_body, ktop=ktop, nq=T, n_steps=n_steps),
        grid_spec=pltpu.PrefetchScalarGridSpec(
            num_scalar_prefetch=1,
            grid=(B, n_steps),
            in_specs=[pl.BlockSpec((None, IDX_HEADS * T, IDX_DIM), per_b3),
                      pl.BlockSpec((None, IDX_HEADS * T, 1), per_b3),
                      pl.BlockSpec((None, PAGE_SIZE, IDX_DIM), per_b3)] + idx_pages,
            out_specs=[pl.BlockSpec((None, T, past), per_b3),
                       pl.BlockSpec((None, T, PAGE_SIZE), per_b3),
                       pl.BlockSpec((None, T, PAGE_SIZE), per_b3)],
            scratch_shapes=[pltpu.VMEM((n_steps + 1, T, step_w), jnp.int32)]),
        out_shape=[jax.ShapeDtypeStruct((B, T, past), jnp.int32),
                   jax.ShapeDtypeStruct((B, T, PAGE_SIZE), jnp.int32),
                   jax.ShapeDtypeStruct((B, T, PAGE_SIZE), jnp.int32)],
        compiler_params=pltpu.CompilerParams(dimension_semantics=("arbitrary", "arbitrary")),
        name="sample_index",
        interpret=interpret,
    )(page_table, qi_s, w_s, ki_new, *([cache_idx_k] * PAGES_PER_STEP))

    kv_pages = [pl.BlockSpec((None, slot_rows, HEAD_DIM), page_map(r)) for r in range(PAGES_PER_STEP)]
    o = pl.pallas_call(
        functools.partial(_sample_attn_body, nq=T, n_steps=n_steps),
        grid_spec=pltpu.PrefetchScalarGridSpec(
            num_scalar_prefetch=1,
            grid=(B, n_steps),
            in_specs=[pl.BlockSpec((None, KV_HEADS, GROUP * T, HEAD_DIM), lambda b, j, pt: (b, 0, 0, 0)),
                      pl.BlockSpec((None, T, step_w), lambda b, j, pt: (b, 0, j)),
                      pl.BlockSpec((None, T, PAGE_SIZE), per_b3),
                      pl.BlockSpec((None, T, PAGE_SIZE), per_b3),
                      pl.BlockSpec((None, slot_rows, HEAD_DIM), per_b3),
                      pl.BlockSpec((None, slot_rows, HEAD_DIM), per_b3)] + kv_pages + kv_pages,
            out_specs=pl.BlockSpec((None, KV_HEADS, GROUP * T, HEAD_DIM), lambda b, j, pt: (b, 0, 0, 0)),
            scratch_shapes=[pltpu.VMEM((KV_HEADS, GROUP * T, 1), f32),
                            pltpu.VMEM((KV_HEADS, GROUP * T, 1), f32),
                            pltpu.VMEM((KV_HEADS, GROUP * T, HEAD_DIM), f32)]),
        out_shape=jax.ShapeDtypeStruct((B, KV_HEADS, GROUP * T, HEAD_DIM), f32),
        compiler_params=pltpu.CompilerParams(dimension_semantics=("arbitrary", "arbitrary")),
        name="sample_attn",
        interpret=interpret,
    )(page_table, q_s, keys, knew, thr, k_new, v_new, *([ck] * PAGES_PER_STEP), *([cv] * PAGES_PER_STEP))
    o = o.reshape(B, KV_HEADS, GROUP, T, HEAD_DIM)
    return jnp.transpose(o, (0, 3, 1, 2, 4)).reshape(B, T, ATTN_HEADS * HEAD_DIM)


GDN_ROWS = 512
_TN = (((0,), (0,)), ((), ()))


def _mm(a, b):
    return jnp.dot(a.astype(jnp.bfloat16), b.astype(jnp.bfloat16), preferred_element_type=jnp.float32)


def _split2(a):
    hi = a.astype(jnp.bfloat16)
    lo = (a - hi.astype(jnp.float32)).astype(jnp.bfloat16)
    return hi, lo


def _mm3(a, b):
    f32 = jnp.float32
    a1, a2 = _split2(a)
    b1, b2 = _split2(b)
    return (jnp.dot(a1, b1, preferred_element_type=f32) + jnp.dot(a1, b2, preferred_element_type=f32)
            + jnp.dot(a2, b1, preferred_element_type=f32))


def _split3(a):
    f32 = jnp.float32
    p1 = a.astype(jnp.bfloat16)
    r = a - p1.astype(f32)
    p2 = r.astype(jnp.bfloat16)
    p3 = (r - p2.astype(f32)).astype(jnp.bfloat16)
    return p1, p2, p3


def _sigmoid(x):
    return 1.0 / (1.0 + jnp.exp(-x))


def _softplus(x):
    return jnp.maximum(x, 0.0) + jnp.log1p(jnp.exp(-jnp.abs(x)))


def _gdn_body(x_ref, araw_ref, braw_ref, arawT_ref, gate_ref, buf_ref, s0_ref, cw_ref,
              alog_ref, dtb_ref, alogT_ref, dtbT_ref, nw_ref,
              ob_ref, sout_ref, xp_scr, y_scr, s_scr, *, rb, t_valid, t_pad):
    f32, bf16 = jnp.float32, jnp.bfloat16
    C = GDN_CHUNK
    j = pl.program_id(1)

    @pl.when(j == 0)
    def _():
        xp_scr[5:8, :] = buf_ref[...]
        s_scr[...] = s0_ref[...]

    xp_scr[8:8 + rb, :] = x_ref[...]
    y = xp_scr[5:5 + rb, :] * cw_ref[0:1, :]
    for t in range(1, CONV_W):
        y = y + xp_scr[5 + t:5 + t + rb, :] * cw_ref[t:t + 1, :]
    y_scr[...] = y * _sigmoid(y)
    xp_scr[5:8, :] = xp_scr[rb + 5:rb + 8, :]

    ri = lax.broadcasted_iota(jnp.int32, (C, C), 0)
    ci = lax.broadcasted_iota(jnp.int32, (C, C), 1)
    tri_incl = jnp.where(ri >= ci, 1.0, 0.0).astype(bf16)
    tri_inclT = jnp.where(ci >= ri, 1.0, 0.0).astype(bf16)
    causal = ri >= ci
    strict = ri > ci
    eye = jnp.where(ri == ci, 1.0, 0.0)
    neg_a = -jnp.exp(alog_ref[...])
    neg_aT = -jnp.exp(alogT_ref[...])
    q_scale = GDN_DK ** -0.5

    def chunk(c, carry):
        r0 = pl.multiple_of(c * C, C)
        g_c = neg_a * _softplus(araw_ref[pl.ds(r0, C), :] + dtb_ref[...])
        beta_c = _sigmoid(braw_ref[pl.ds(r0, C), :])
        g_r = neg_aT * _softplus(arawT_ref[c] + dtbT_ref[...])
        if t_valid < t_pad:
            base = j * rb + r0
            row_ok = (base + lax.broadcasted_iota(jnp.int32, (C, GDN_HEADS), 0)) < t_valid
            col_ok = (base + lax.broadcasted_iota(jnp.int32, (GDN_HEADS, C), 1)) < t_valid
            g_c = jnp.where(row_ok, g_c, 0.0)
            beta_c = jnp.where(row_ok, beta_c, 0.0)
            g_r = jnp.where(col_ok, g_r, 0.0)
        gc_c = sum(jnp.dot(tri_incl, p, preferred_element_type=f32) for p in _split3(g_c))
        gc_r = sum(jnp.dot(p, tri_inclT, preferred_element_type=f32) for p in _split3(g_r))
        H = range(GDN_HEADS)
        q, k, v, gcol, glast, beta, decay, eg = [], [], [], [], [], [], [], []
        for h in H:
            lo = h * GDN_DK
            qh = y_scr[pl.ds(r0, C), lo:lo + GDN_DK]
            kh = y_scr[pl.ds(r0, C), GDN_HEADS * GDN_DK + lo:GDN_HEADS * GDN_DK + lo + GDN_DK]
            v.append(y_scr[pl.ds(r0, C),
                           2 * GDN_HEADS * GDN_DK + h * GDN_DV:2 * GDN_HEADS * GDN_DK + (h + 1) * GDN_DV])
            q.append(qh * lax.rsqrt(jnp.sum(qh * qh, axis=-1, keepdims=True) + NORM_EPS) * q_scale)
            k.append(kh * lax.rsqrt(jnp.sum(kh * kh, axis=-1, keepdims=True) + NORM_EPS))
            gcol.append(gc_c[:, h:h + 1])
            glast.append(gc_c[C - 1:C, h:h + 1])
            beta.append(beta_c[:, h:h + 1])
            decay.append(jnp.exp(jnp.where(causal, gcol[h] - gc_r[h:h + 1, :], -jnp.inf)))
            eg.append(jnp.exp(gcol[h]))
        kb = [k[h] * beta[h] for h in H]
        kk = [lax.dot_general(kb[h].astype(bf16), k[h].astype(bf16), _NT, preferred_element_type=f32) for h in H]
        qk = [lax.dot_general(q[h].astype(bf16), k[h].astype(bf16), _NT, preferred_element_type=f32) * decay[h]
              for h in H]
        pw = [jnp.where(strict, -(kk[h] * decay[h]), 0.0) for h in H]
        inv = [eye + pw[h] for h in H]
        for _ in range(5):
            pw = [_mm3(pw[h], pw[h]) for h in H]
            inv = [inv[h] + _mm3(inv[h], pw[h]) for h in H]
        sol = [_mm3(inv[h], jnp.concatenate([v[h] * beta[h], kb[h] * eg[h]], axis=1)) for h in H]
        s_old = [s_scr[h] for h in H]
        v_new = [sol[h][:, :GDN_DV] - _mm(sol[h][:, GDN_DV:], s_old[h]) for h in H]
        o = [_mm(q[h] * eg[h], s_old[h]) + _mm(qk[h], v_new[h]) for h in H]
        for h in H:
            kd = k[h] * jnp.exp(glast[h] - gcol[h])
            s_scr[h] = s_old[h] * jnp.exp(glast[h]) + lax.dot_general(
                kd.astype(bf16), v_new[h].astype(bf16), _TN, preferred_element_type=f32)
        for h in H:
            on = o[h] * lax.rsqrt(jnp.mean(o[h] * o[h], axis=-1, keepdims=True) + NORM_EPS) * nw_ref[...]
            gt = gate_ref[pl.ds(r0, C), h * GDN_DV:(h + 1) * GDN_DV]
            ob_ref[pl.ds(r0, C), h * GDN_DV:(h + 1) * GDN_DV] = on * (gt * _sigmoid(gt))
        return carry

    lax.fori_loop(0, rb // C, chunk, 0)

    @pl.when(j == pl.num_programs(1) - 1)
    def _():
        sout_ref[...] = s_scr[...]


def _gdn_pallas(qkv, b_raw, a_raw, gate, conv_buf, S0, conv_w, a_log, dt_bias, gdn_norm_w,
                qkv_blk=0, gate_blk=0, interpret=False):
    B, T, _ = qkv.shape
    assert T >= CONV_W - 1
    f32 = jnp.float32
    C = GDN_CHUNK
    t_pad = -(-T // C) * C
    rb = min(GDN_ROWS, t_pad)
    assert t_pad % rb == 0
    pad = lambda a: jnp.pad(a, ((0, 0), (0, t_pad - T), (0, 0)))
    x, a_p, b_p = pad(qkv), pad(a_raw), pad(b_raw)
    gate_p = x if gate is qkv else pad(gate)
    a_t = jnp.transpose(a_p.reshape(B, t_pad // C, C, GDN_HEADS), (0, 1, 3, 2))
    hd = GDN_HEADS * GDN_DV
    row_blk = lambda b, j: (b, j, 0)
    fix2 = lambda b, j: (0, 0)
    ob, s_out = pl.pallas_call(
        functools.partial(_gdn_body, rb=rb, t_valid=T, t_pad=t_pad),
        grid=(B, t_pad // rb),
        in_specs=[pl.BlockSpec((None, rb, CONV_DIM), lambda b, j: (b, j, qkv_blk)),
                  pl.BlockSpec((None, rb, GDN_HEADS), row_blk),
                  pl.BlockSpec((None, rb, GDN_HEADS), row_blk),
                  pl.BlockSpec((None, rb // C, GDN_HEADS, C), lambda b, j: (b, j, 0, 0)),
                  pl.BlockSpec((None, rb, hd), lambda b, j: (b, j, gate_blk)),
                  pl.BlockSpec((None, CONV_W - 1, CONV_DIM), lambda b, j: (b, 0, 0)),
                  pl.BlockSpec((None, GDN_HEADS, GDN_DK, GDN_DV), lambda b, j: (b, 0, 0, 0)),
                  pl.BlockSpec((CONV_W, CONV_DIM), fix2),
                  pl.BlockSpec((1, GDN_HEADS), fix2),
                  pl.BlockSpec((1, GDN_HEADS), fix2),
                  pl.BlockSpec((GDN_HEADS, 1), fix2),
                  pl.BlockSpec((GDN_HEADS, 1), fix2),
                  pl.BlockSpec((1, GDN_DV), fix2)],
        out_specs=[pl.BlockSpec((None, rb, hd), row_blk),
                   pl.BlockSpec((None, GDN_HEADS, GDN_DK, GDN_DV), lambda b, j: (b, 0, 0, 0))],
        out_shape=[jax.ShapeDtypeStruct((B, t_pad, hd), f32),
                   jax.ShapeDtypeStruct((B, GDN_HEADS, GDN_DK, GDN_DV), f32)],
        scratch_shapes=[pltpu.VMEM((rb + 8, CONV_DIM), f32),
                        pltpu.VMEM((rb, CONV_DIM), f32),
                        pltpu.VMEM((GDN_HEADS, GDN_DK, GDN_DV), f32)],
        compiler_params=pltpu.CompilerParams(
            dimension_semantics=("arbitrary", "arbitrary"),
            vmem_limit_bytes=56 * 1024 * 1024),
        name="gated_deltanet",
        interpret=interpret,
    )(x, a_p, b_p, a_t, gate_p, conv_buf, S0, conv_w,
      a_log.reshape(1, GDN_HEADS), dt_bias.reshape(1, GDN_HEADS),
      a_log.reshape(GDN_HEADS, 1), dt_bias.reshape(GDN_HEADS, 1), gdn_norm_w.reshape(1, GDN_DV))
    return ob[:, :T], s_out, qkv[:, T - (CONV_W - 1):, qkv_blk * CONV_DIM:(qkv_blk + 1) * CONV_DIM]


_SQRT_HALF = 0.7071067811865476


def _top_rows(x, k):
    R, n = x.shape
    ri = lax.broadcasted_iota(jnp.int32, (R, n), 0).astype(jnp.float32)
    ki = lax.broadcasted_iota(jnp.int32, (k, n), 0)

    def body(r, c):
        x, out, idx = c
        m = x.max(axis=0, keepdims=True)
        first = jnp.min(jnp.where(x == m, ri, float(R)), axis=0, keepdims=True)
        x = jnp.where(ri == first, -jnp.inf, x)
        return x, jnp.where(ki == r, m, out), jnp.where(ki == r, first, idx)

    zk = jnp.zeros((k, n), jnp.float32)
    rest, out, idx = lax.fori_loop(0, k, body, (x, zk, zk))
    return out, idx, rest.max(axis=0, keepdims=True)


def _peer_front_body(x_ref, oa_ref, ob_ref, wo_ref, g_ref, wq_ref, keys_ref,
                     h_ref, xn_ref, s1_ref, s2_ref, e2_ref, aux_ref, eidx_ref, gsm_ref, flag_ref, top_scr):
    f32, bf16 = jnp.float32, jnp.bfloat16
    half_w = ATTN_HEADS * HEAD_DIM
    h = (x_ref[...]
         + jnp.dot(oa_ref[...].astype(bf16), wo_ref[:half_w, :], preferred_element_type=f32)
         + jnp.dot(ob_ref[...].astype(bf16), wo_ref[half_w:, :], preferred_element_type=f32))
    h_ref[...] = h
    xn = (h * lax.rsqrt(jnp.mean(h * h, axis=-1, keepdims=True) + NORM_EPS) * g_ref[...]).astype(bf16)
    xn_ref[...] = xn
    qh = jnp.dot(xn, wq_ref[...], preferred_element_type=f32).astype(bf16)
    tq = qh.shape[0]
    hq = PEER_QDIM // 2
    K = PEER_TOPK
    tied = jnp.zeros((1, tq), f32)
    for hh in range(PEER_HEADS):
        tops = []
        for half in range(2):
            col = (hh * 2 + half) * hq
            sT = lax.dot_general(keys_ref[half, hh], qh[:, col:col + hq], _NT,
                                 preferred_element_type=f32)
            (s1_ref if half == 0 else s2_ref)[hh] = sT
            vals, idx, nxt = _top_rows(sT, K)
            top_scr[half, hh] = jnp.concatenate([vals, idx], axis=0)
            tied = jnp.where(nxt == vals[K - 1:K, :], 1.0, tied)
            tops.append(vals)
        a16, b16 = tops
        cand = jnp.concatenate(
            [a16[r:r + 1, :] + b16[0:8, :] for r in range(8)]
            + [a16[0:1, :] + b16[8:16, :], a16[8:16, :] + b16[0:1, :]], axis=0)
        tau = _top_rows(cand, K)[0][K - 1:K, :]
        top_sum = a16[0:1, :] + b16[0:1, :]
        keep = cand >= tau
        tied = jnp.where(jnp.sum(jnp.where(keep, 1.0, 0.0), axis=0, keepdims=True) > K, 1.0, tied)
        z = jnp.sum(jnp.where(keep, jnp.exp(cand - top_sum), 0.0), axis=0, keepdims=True)
        e2_ref[hh] = jnp.exp(s2_ref[hh] - b16[0:1, :]) / z
        aux_ref[hh] = jnp.concatenate([tau, a16[0:1, :], jnp.zeros((6, tq), f32)], axis=0)

    any_tied = jnp.max(tied) > 0.0
    flag_ref[...] = jnp.full(flag_ref.shape, jnp.where(any_tied, 1.0, 0.0), f32)

    @pl.when(jnp.logical_not(any_tied))
    def _():
        eidx_ref[...] = jnp.zeros(eidx_ref.shape, jnp.int32)
        gsm_ref[...] = jnp.zeros(gsm_ref.shape, f32)

    @pl.when(any_tied)
    def _():
        for hh in range(PEER_HEADS):
            a16, ia = top_scr[0, hh, 0:K, :], top_scr[0, hh, K:2 * K, :]
            b16, ib = top_scr[1, hh, 0:K, :], top_scr[1, hh, K:2 * K, :]
            full = jnp.concatenate([a16[r:r + 1, :] + b16 for r in range(K)], axis=0)
            sv, flat, _ = _top_rows(full, K)
            ra = jnp.floor(flat * (1.0 / K))
            rb = flat - ra * K
            i1 = jnp.zeros((K, tq), f32)
            i2 = jnp.zeros((K, tq), f32)
            for r in range(K):
                i1 = jnp.where(ra == r, ia[r:r + 1, :], i1)
                i2 = jnp.where(rb == r, ib[r:r + 1, :], i2)
            eidx_ref[hh] = (i1 * PEER_NKEYS + i2).astype(jnp.int32)
            ex = jnp.exp(sv - sv[0:1, :])
            gsm_ref[hh] = ex / jnp.sum(ex, axis=0, keepdims=True)


def _peer_front(x, oa, ob, wo, g, wq, keys, tq, interpret=False):
    n = x.shape[0]
    half_w = ATTN_HEADS * HEAD_DIM
    tok = lambda i: (i, 0)
    fix2 = lambda i: (0, 0)
    colT = lambda i: (0, 0, i)
    f32 = jnp.float32
    plane = jax.ShapeDtypeStruct((PEER_HEADS, PEER_NKEYS, n), f32)
    plane_spec = pl.BlockSpec((PEER_HEADS, PEER_NKEYS, tq), colT)
    return pl.pallas_call(
        _peer_front_body,
        grid=(n // tq,),
        in_specs=[pl.BlockSpec((tq, D_MODEL), tok),
                  pl.BlockSpec((tq, half_w), tok),
                  pl.BlockSpec((tq, half_w), tok),
                  pl.BlockSpec((D_MODEL, D_MODEL), fix2),
                  pl.BlockSpec((1, D_MODEL), fix2),
                  pl.BlockSpec((D_MODEL, PEER_HEADS * PEER_QDIM), fix2),
                  pl.BlockSpec((2, PEER_HEADS, PEER_NKEYS, PEER_QDIM // 2), lambda i: (0, 0, 0, 0))],
        out_specs=[pl.BlockSpec((tq, D_MODEL), tok),
                   pl.BlockSpec((tq, D_MODEL), tok),
                   plane_spec, plane_spec, plane_spec,
                   pl.BlockSpec((PEER_HEADS, 8, tq), colT),
                   pl.BlockSpec((PEER_HEADS, PEER_TOPK, tq), colT),
                   pl.BlockSpec((PEER_HEADS, PEER_TOPK, tq), colT),
                   pl.BlockSpec((1, 8, LANES), lambda i: (i, 0, 0))],
        out_shape=[jax.ShapeDtypeStruct((n, D_MODEL), f32),
                   jax.ShapeDtypeStruct((n, D_MODEL), jnp.bfloat16),
                   plane, plane, plane,
                   jax.ShapeDtypeStruct((PEER_HEADS, 8, n), f32),
                   jax.ShapeDtypeStruct((PEER_HEADS, PEER_TOPK, n), jnp.int32),
                   jax.ShapeDtypeStruct((PEER_HEADS, PEER_TOPK, n), f32),
                   jax.ShapeDtypeStruct((n // tq, 8, LANES), f32)],
        scratch_shapes=[pltpu.VMEM((2, PEER_HEADS, 2 * PEER_TOPK, tq), f32)],
        compiler_params=pltpu.CompilerParams(
            dimension_semantics=("arbitrary",),
            vmem_limit_bytes=56 * 1024 * 1024),
        name="peer_front",
        interpret=interpret,
    )(x, oa, ob, wo, g.reshape(1, D_MODEL), wq, keys)


def _peer_dense_body(xn_ref, u_ref, vT_ref, s1_ref, s2_ref, e2_ref, aux_ref, eidx_ref, gsm_ref, flag_ref,
                     yT_ref, g_scr, *, eblk):
    f32 = jnp.float32
    eb = pl.program_id(1)
    tq = xn_ref.shape[0]
    sub = eblk // PEER_NKEYS

    @pl.when(eb == 0)
    def _():
        yT_ref[...] = jnp.zeros(yT_ref.shape, f32)

    def finish(gate_of):
        a = lax.dot_general(u_ref[...], xn_ref[...], _NT, preferred_element_type=f32)
        act = 0.5 * a * (1.0 + lax.erf(a * _SQRT_HALF))
        hT = jnp.concatenate(
            [(gate_of(r) * act[r * PEER_NKEYS:(r + 1) * PEER_NKEYS, :]).astype(jnp.bfloat16) for r in range(sub)],
            axis=0)
        yT_ref[...] += jnp.dot(vT_ref[...], hT, preferred_element_type=f32)

    exact_lists = jnp.max(flag_ref[...]) > 0.0

    def threshold_gate(r):
        i1 = eb * sub + r
        gate = jnp.zeros((PEER_NKEYS, tq), f32)
        for hh in range(PEER_HEADS):
            s1row = s1_ref[hh, pl.ds(i1, 1), :]
            tau = aux_ref[hh, 0:1, :]
            e1row = jnp.exp(s1row - aux_ref[hh, 1:2, :])
            gate = gate + jnp.where(s1row + s2_ref[hh] >= tau, e1row * e2_ref[hh], 0.0)
        return gate

    @pl.when(jnp.logical_not(exact_lists))
    def _():
        finish(threshold_gate)

    @pl.when(exact_lists)
    def _():
        n_flags = flag_ref.shape[0]
        use_list = jnp.concatenate(
            [jnp.broadcast_to(flag_ref[f, 0:1, 0:1], (1, tq // n_flags)) for f in range(n_flags)], axis=1) > 0.0
        row = lax.broadcasted_iota(jnp.int32, (PEER_NKEYS, tq), 0)
        for r in range(sub):
            base = (eb * sub + r) * PEER_NKEYS

            def add_head(hh, gate):
                ids = eidx_ref[hh] - base
                wts = gsm_ref[hh]
                for k in range(PEER_TOPK):
                    gate = gate + jnp.where(row == ids[k:k + 1, :], wts[k:k + 1, :], 0.0)
                return gate

            listed = lax.fori_loop(0, PEER_HEADS, add_head, jnp.zeros((PEER_NKEYS, tq), f32))
            g_scr[r * PEER_NKEYS:(r + 1) * PEER_NKEYS, :] = jnp.where(use_list, listed, threshold_gate(r))
        finish(lambda r: g_scr[r * PEER_NKEYS:(r + 1) * PEER_NKEYS, :])


def _peer_dense(xn, u, vT, s1, s2, e2, aux, eidx, gsm, flag, tq, eblk, interpret=False):
    n = xn.shape[0]
    ne = u.shape[0]
    flags_per_tile = flag.shape[0] * tq // n
    colT = lambda i, e: (0, 0, i)
    plane_spec = pl.BlockSpec((PEER_HEADS, PEER_NKEYS, tq), colT)
    list_spec = pl.BlockSpec((PEER_HEADS, PEER_TOPK, tq), colT)
    return pl.pallas_call(
        functools.partial(_peer_dense_body, eblk=eblk),
        grid=(n // tq, ne // eblk),
        in_specs=[pl.BlockSpec((tq, D_MODEL), lambda i, e: (i, 0)),
                  pl.BlockSpec((eblk, D_MODEL), lambda i, e: (e, 0)),
                  pl.BlockSpec((D_MODEL, eblk), lambda i, e: (0, e)),
                  plane_spec, plane_spec, plane_spec,
                  pl.BlockSpec((PEER_HEADS, 8, tq), colT),
                  list_spec, list_spec,
                  pl.BlockSpec((flags_per_tile, 8, LANES), lambda i, e: (i, 0, 0))],
        out_specs=pl.BlockSpec((D_MODEL, tq), lambda i, e: (0, i)),
        out_shape=jax.ShapeDtypeStruct((D_MODEL, n), jnp.float32),
        scratch_shapes=[pltpu.VMEM((eblk, tq), jnp.float32)],
        compiler_params=pltpu.CompilerParams(
            dimension_semantics=("arbitrary", "arbitrary"),
            vmem_limit_bytes=56 * 1024 * 1024),
        name="peer_dense",
        interpret=interpret,
    )(xn, u, vT, s1, s2, e2, aux, eidx, gsm, flag)


def _layer_out_pallas(x, oa, ob, wo_b, ffn_norm_w, wq_b, keys_b, u_b, vT_b, interpret=False):
    n = x.shape[0]
    tq1 = min(256, n)
    tq2 = 512 if n % 512 == 0 else min(256, n)
    h, xn, s1, s2, e2, aux, eidx, gsm, flag = _peer_front(x, oa, ob, wo_b, ffn_norm_w, wq_b, keys_b, tq1, interpret)
    yT = _peer_dense(xn, u_b, vT_b, s1, s2, e2, aux, eidx, gsm, flag, tq2, 512, interpret)
    return h + yT.T


def kernel(x_prompt, x_sample, cache_k, cache_v, cache_idx_k, state_ssm, state_conv, page_table,
           attn_norm_w, w_in, q_norm_w, k_norm_w, idx_k_norm_w, conv_w, a_log, dt_bias, gdn_norm_w,
           w_out, ffn_norm_w, peer_wq, peer_keys, peer_u, peer_v):
    l = 0
    proj_w = (attn_norm_w[l], _permute_w_in(w_in[l]), q_norm_w[l], k_norm_w[l], idx_k_norm_w[l])
    gdn_cols = dict(qkv_blk=_DST["qkv"] // CONV_DIM, gate_blk=_DST["gate"] // (GDN_HEADS * GDN_DV))
    gdn_w = (conv_w[l], a_log[l], dt_bias[l], gdn_norm_w[l])
    bf16 = jnp.bfloat16
    out_w = (w_out[l].astype(bf16), ffn_norm_w[l], peer_wq[l].astype(bf16), peer_keys[l].astype(bf16),
             peer_u[l].astype(bf16), peer_v[l].astype(bf16).T)

    hp, hs = x_prompt, x_sample
    (qa, ka, va, qi, ki, wi), (z, b_raw, a_raw) = _in_projection(hp, *proj_w)
    oa = _dsa_prompt_pallas(qa[0], ka[0], va[0], qi[0], ki[0], wi[0])[None]
    Bp = hp.shape[0]
    buf0 = jnp.zeros((Bp, CONV_W - 1, CONV_DIM), hp.dtype)
    S0 = jnp.zeros((Bp, GDN_HEADS, GDN_DK, GDN_DV), jnp.float32)
    ob, S_p, buf_p = _gdn_pallas(z, b_raw, a_raw, z, buf0, S0, *gdn_w, **gdn_cols)
    half_w = ATTN_HEADS * HEAD_DIM
    hp = _layer_out_pallas(hp[0], oa[0], ob[0], *out_w)[None]
    kp, vp, ip = ka, va, ki

    (qa, ka, va, qi, ki, wi), (z, b_raw, a_raw) = _in_projection(hs, *proj_w)
    oa = _dsa_sample_pallas(qa, ka, va, qi, ki, wi, cache_k[l], cache_v[l], cache_idx_k[l], page_table)
    ob, S_s, buf_s = _gdn_pallas(z, b_raw, a_raw, z, state_conv[l], state_ssm[l].astype(jnp.float32),
                                 *gdn_w, **gdn_cols)
    ns = hs.shape[0] * hs.shape[1]
    hs = _layer_out_pallas(hs.reshape(ns, D_MODEL), oa.reshape(ns, half_w), ob.reshape(ns, half_w),
                           *out_w).reshape(hs.shape)

    return (hp, hs, kp[None], vp[None], ip[None], S_p[None], buf_p[None],
            ka[None], va[None], ki[None], S_s[None], buf_s[None])
```

```python
import functools
import math

import jax
import jax.numpy as jnp
from jax import lax
from jax.experimental import pallas as pl
from jax.experimental.pallas import tpu as pltpu

D_MODEL = 2048
PAGE_SIZE = 128
HEAD_DIM = 128
ATTN_HEADS = 8
KV_HEADS = 2
GROUP = ATTN_HEADS // KV_HEADS
IDX_HEADS = 16
IDX_DIM = 64
INDEX_TOPK = 256
IDX_SCALE = (IDX_HEADS * IDX_DIM) ** -0.5
GDN_DK = 128
GDN_DV = 128
GDN_HEADS = 8
CONV_W = 4
CONV_DIM = GDN_HEADS * (2 * GDN_DK + GDN_DV)
GDN_CHUNK = 64
IN_SIZES = (ATTN_HEADS * HEAD_DIM, KV_HEADS * HEAD_DIM, KV_HEADS * HEAD_DIM,
            IDX_HEADS * IDX_DIM, IDX_DIM, IDX_HEADS,
            CONV_DIM, GDN_HEADS, GDN_HEADS, GDN_HEADS * GDN_DV)
IN_COLS = sum(IN_SIZES)
PEER_HEADS = 8
PEER_NKEYS = 128
PEER_QDIM = 256
PEER_TOPK = 16
NORM_EPS = 1e-6

LANES = 128


def _rmsnorm(x, w):
    xf = x.astype(jnp.float32)
    y = xf * lax.rsqrt(jnp.mean(xf * xf, axis=-1, keepdims=True) + NORM_EPS)
    return (y * w.astype(jnp.float32)).astype(x.dtype)


def _norm_matmul_body(x_ref, g_ref, w_ref, o_ref, xn_ref):
    @pl.when(pl.program_id(1) == 0)
    def _():
        x = x_ref[...]
        r = lax.rsqrt(jnp.mean(x * x, axis=-1, keepdims=True) + NORM_EPS)
        xn_ref[...] = (x * r * g_ref[...]).astype(jnp.bfloat16)

    o_ref[...] = jnp.dot(xn_ref[...], w_ref[...], preferred_element_type=jnp.float32)


def _norm_matmul(x, g, w, tm, tn):
    m, k = x.shape
    n = w.shape[1]
    return pl.pallas_call(
        _norm_matmul_body,
        grid=(m // tm, n // tn),
        in_specs=[pl.BlockSpec((tm, k), lambda i, j: (i, 0)),
                  pl.BlockSpec((1, k), lambda i, j: (0, 0)),
                  pl.BlockSpec((k, tn), lambda i, j: (0, j))],
        out_specs=pl.BlockSpec((tm, tn), lambda i, j: (i, j)),
        out_shape=jax.ShapeDtypeStruct((m, n), jnp.float32),
        scratch_shapes=[pltpu.VMEM((tm, k), jnp.bfloat16)],
        compiler_params=pltpu.CompilerParams(
            dimension_semantics=("arbitrary", "arbitrary"),
            vmem_limit_bytes=48 * 1024 * 1024),
        name="norm_matmul",
    )(x, g.reshape(1, k), w)


PROJ_TILE = 1024
_SRC = dict(zip(("qa", "ka", "va", "qi", "ki", "wi", "qkv", "b", "a", "gate"),
                [(sum(IN_SIZES[:i]), IN_SIZES[i]) for i in range(len(IN_SIZES))]))
_DST_ORDER = (("qkv",), ("qa",), ("qi",), ("gate",), ("ka",), ("va",), ("ki",), ("wi", "b", "a"))


def _proj_layout():
    dst, off = {}, 0
    for group in _DST_ORDER:
        for name in group:
            dst[name] = off
            off += _SRC[name][1]
        off = -(-off // LANES) * LANES
    return dst, -(-off // PROJ_TILE) * PROJ_TILE


_DST, PROJ_COLS = _proj_layout()
assert _DST["qkv"] == 0 and _DST["gate"] % (GDN_HEADS * GDN_DV) == 0


def _permute_w_in(w_in):
    pieces, off = [], 0
    for group in _DST_ORDER:
        for name in group:
            if _DST[name] > off:
                pieces.append(jnp.zeros((D_MODEL, _DST[name] - off), w_in.dtype))
            s0, n = _SRC[name]
            pieces.append(w_in[:, s0:s0 + n])
            off = _DST[name] + n
    pieces.append(jnp.zeros((D_MODEL, PROJ_COLS - off), w_in.dtype))
    return jnp.concatenate(pieces, axis=1).astype(jnp.bfloat16)


def _in_projection(x, attn_norm_w, w_in_p, q_norm_w, k_norm_w, idx_k_norm_w):
    B, T, _ = x.shape
    m = B * T
    tm = 512 if m % 512 == 0 else m
    z = _norm_matmul(x.reshape(m, D_MODEL), attn_norm_w, w_in_p, tm, PROJ_TILE).reshape(B, T, PROJ_COLS)
    col = lambda name: z[:, :, _DST[name]:_DST[name] + _SRC[name][1]]
    qa = _rmsnorm(col("qa").reshape(B, T, ATTN_HEADS, HEAD_DIM), q_norm_w)
    ka = _rmsnorm(col("ka").reshape(B, T, KV_HEADS, HEAD_DIM), k_norm_w)
    va = col("va").reshape(B, T, KV_HEADS, HEAD_DIM)
    qi = col("qi").reshape(B, T, IDX_HEADS, IDX_DIM)
    ki = _rmsnorm(col("ki"), idx_k_norm_w)
    return (qa, ka, va, qi, ki, col("wi")), (z, col("b"), col("a"))


_INT_MIN = -2 ** 31
_NEG_BIG = -1e30
_NT = (((1,), (1,)), ((), ()))


def _dsa_prompt_body(qi_ref, wT_ref, q_ref, ki_ref, k_ref, vT_ref, o_ref,
                     keys_ref, m_ref, l_ref, acc_ref, *, ktop, tq):
    f32 = jnp.float32
    i = pl.program_id(0)
    nkb = i + 1
    col_t = i * tq + lax.broadcasted_iota(jnp.int32, (tq, tq), 1)
    row_s = lax.broadcasted_iota(jnp.int32, (tq, tq), 0)

    def score_blk(kb, carry):
        kib = ki_ref[kb]
        acc = jnp.zeros((tq, tq), f32)
        for h in range(IDX_HEADS):
            s = lax.dot_general(kib, qi_ref[h], _NT, preferred_element_type=f32)
            acc = acc + jnp.maximum(s, 0.0) * wT_ref[h:h + 1, :]
        sc = acc * IDX_SCALE
        bits = lax.bitcast_convert_type(sc, jnp.int32)
        key = bits ^ (lax.shift_right_arithmetic(bits, 31) & 0x7FFFFFFF)
        valid = (kb * tq + row_s) <= col_t
        keys_ref[kb] = jnp.where(valid, key, _INT_MIN)
        return carry

    lax.fori_loop(0, nkb, score_blk, 0)

    def bisect(it, ans_u):
        cand_u = ans_u | lax.shift_left(jnp.int32(1), 31 - it)
        cand_s = cand_u ^ _INT_MIN

        def count_blk(kb, cnt):
            hit = jnp.where(keys_ref[kb] >= cand_s, 1.0, 0.0)
            return cnt + hit.reshape(tq // 8, 8, tq).sum(axis=0)

        cnt = lax.fori_loop(0, nkb, count_blk, jnp.zeros((8, tq), f32))
        cnt = cnt.sum(axis=0, keepdims=True)
        return jnp.where(cnt >= ktop, cand_u, ans_u)

    ans_u = lax.fori_loop(0, 32, bisect, jnp.zeros((1, tq), jnp.int32))
    thr = jnp.maximum(ans_u ^ _INT_MIN, _INT_MIN + 1)

    def tally(kb, c):
        kk = keys_ref[kb]
        ge = jnp.where(kk >= thr, 1.0, 0.0).reshape(tq // 8, 8, tq).sum(axis=0)
        gt = jnp.where(kk > thr, 1.0, 0.0).reshape(tq // 8, 8, tq).sum(axis=0)
        return c[0] + ge, c[1] + gt

    z8 = jnp.zeros((8, tq), f32)
    n_ge, n_gt = lax.fori_loop(0, nkb, tally, (z8, z8))
    n_ge = n_ge.sum(axis=0, keepdims=True)
    n_gt = n_gt.sum(axis=0, keepdims=True)

    @pl.when(jnp.max(n_ge) > ktop)
    def _():
        need = jnp.where(n_ge >= ktop, ktop - n_gt, float(2 ** 30))
        idx_bits = max(1, (keys_ref.shape[0] * tq - 1).bit_length())

        def bisect_idx(it, pos):
            cand = pos | lax.shift_left(jnp.int32(1), idx_bits - 1 - it)

            def count_blk(kb, cnt):
                hit = (keys_ref[kb] == thr) & ((kb * tq + row_s) < cand)
                return cnt + jnp.where(hit, 1.0, 0.0).reshape(tq // 8, 8, tq).sum(axis=0)

            cnt = lax.fori_loop(0, nkb, count_blk, z8).sum(axis=0, keepdims=True)
            return jnp.where(cnt < need, cand, pos)

        last = lax.fori_loop(0, idx_bits, bisect_idx, jnp.zeros((1, tq), jnp.int32))

        def demote(kb, carry):
            kk = keys_ref[kb]
            keys_ref[kb] = jnp.where((kk == thr) & ((kb * tq + row_s) > last), kk - 1, kk)
            return carry

        lax.fori_loop(0, nkb, demote, 0)

    m_ref[...] = jnp.full(m_ref.shape, _NEG_BIG, f32)
    l_ref[...] = jnp.zeros(l_ref.shape, f32)
    acc_ref[...] = jnp.zeros(acc_ref.shape, f32)
    c2 = HEAD_DIM ** -0.5 * math.log2(math.e)

    def attn_blk(kb, carry):
        sel = keys_ref[kb] >= thr
        kblk = k_ref[kb]
        vT = vT_ref[kb]
        s_all = [lax.dot_general(kblk[:, (h // GROUP) * HEAD_DIM:(h // GROUP + 1) * HEAD_DIM], q_ref[h], _NT,
                                 preferred_element_type=f32) for h in range(ATTN_HEADS)]
        for h in range(ATTN_HEADS):
            n = h // GROUP
            s = jnp.where(sel, s_all[h], -jnp.inf)
            m_old = m_ref[h]
            m_new = jnp.maximum(m_old, s.max(axis=0, keepdims=True))
            p = jnp.exp2((s - m_new) * c2)
            alpha = jnp.exp2((m_old - m_new) * c2)
            l_ref[h] = alpha * l_ref[h] + p.sum(axis=0, keepdims=True)
            pv = jnp.dot(vT[n * HEAD_DIM:(n + 1) * HEAD_DIM, :], p.astype(jnp.bfloat16),
                         preferred_element_type=f32)
            acc_ref[h] = alpha * acc_ref[h] + pv
            m_ref[h] = m_new
        return carry

    lax.fori_loop(0, nkb, attn_blk, 0)
    for h in range(ATTN_HEADS):
        o_ref[:, h * HEAD_DIM:(h + 1) * HEAD_DIM] = (acc_ref[h] / l_ref[h]).T


def _dsa_prompt_pallas(q, k, v, qi, ki, wi, interpret=False):
    T = q.shape[0]
    tq = min(256, T)
    nb = T // tq
    ktop = min(INDEX_TOPK, T // 4)
    bf16 = jnp.bfloat16
    qh = jnp.transpose(q.astype(bf16), (1, 0, 2))
    qih = jnp.transpose(qi.astype(bf16), (1, 0, 2))
    wT = wi.astype(jnp.float32).T
    kib = ki.astype(bf16).reshape(nb, tq, IDX_DIM)
    kb = k.astype(bf16).reshape(nb, tq, KV_HEADS * HEAD_DIM)
    vT = jnp.transpose(v.astype(bf16).reshape(nb, tq, KV_HEADS * HEAD_DIM), (0, 2, 1))
    body = functools.partial(_dsa_prompt_body, ktop=ktop, tq=tq)
    return pl.pallas_call(
        body,
        grid=(nb,),
        in_specs=[pl.BlockSpec((IDX_HEADS, tq, IDX_DIM), lambda i: (0, i, 0)),
                  pl.BlockSpec((IDX_HEADS, tq), lambda i: (0, i)),
                  pl.BlockSpec((ATTN_HEADS, tq, HEAD_DIM), lambda i: (0, i, 0)),
                  pl.BlockSpec((nb, tq, IDX_DIM), lambda i: (0, 0, 0)),
                  pl.BlockSpec((nb, tq, KV_HEADS * HEAD_DIM), lambda i: (0, 0, 0)),
                  pl.BlockSpec((nb, KV_HEADS * HEAD_DIM, tq), lambda i: (0, 0, 0))],
        out_specs=pl.BlockSpec((tq, ATTN_HEADS * HEAD_DIM), lambda i: (i, 0)),
        out_shape=jax.ShapeDtypeStruct((T, ATTN_HEADS * HEAD_DIM), jnp.float32),
        scratch_shapes=[pltpu.VMEM((nb, tq, tq), jnp.int32),
                        pltpu.VMEM((ATTN_HEADS, 1, tq), jnp.float32),
                        pltpu.VMEM((ATTN_HEADS, 1, tq), jnp.float32),
                        pltpu.VMEM((ATTN_HEADS, HEAD_DIM, tq), jnp.float32)],
        compiler_params=pltpu.CompilerParams(
            dimension_semantics=("arbitrary",),
            vmem_limit_bytes=56 * 1024 * 1024),
        name="dsa_prompt",
        interpret=interpret,
    )(qih, wT, qh, kib, kb, vT)


PAGES_PER_STEP = 8


def _sortable_key(x):
    bits = lax.bitcast_convert_type(x, jnp.int32)
    return bits ^ (lax.shift_right_arithmetic(bits, 31) & 0x7FFFFFFF)


def _sample_index_body(pt_ref, qi_ref, w_ref, kin_ref, *rest, ktop, nq, n_steps):
    del pt_ref
    pages = rest[:PAGES_PER_STEP]
    keys_out, knew_out, thr_out, keys_scr = rest[PAGES_PER_STEP:]
    f32 = jnp.float32
    j = pl.program_id(1)
    step_w = PAGES_PER_STEP * PAGE_SIZE
    past = n_steps * step_w
    qi = qi_ref[...]
    w = w_ref[...]

    def page_keys(page):
        s = lax.dot_general(qi, page.astype(jnp.bfloat16), _NT, preferred_element_type=f32)
        s = jnp.maximum(s, 0.0) * w
        return s.reshape(IDX_HEADS, nq, PAGE_SIZE).sum(axis=0) * IDX_SCALE

    keys_scr[j] = jnp.concatenate([_sortable_key(page_keys(p[...])) for p in pages], axis=1)

    @pl.when(j == n_steps - 1)
    def _():
        kn = _sortable_key(page_keys(kin_ref[...]))
        col = lax.broadcasted_iota(jnp.int32, (nq, PAGE_SIZE), 1)
        row = lax.broadcasted_iota(jnp.int32, (nq, PAGE_SIZE), 0)
        kn = jnp.where(col <= row, kn, _INT_MIN)
        keys_scr[n_steps] = jnp.concatenate(
            [kn, jnp.full((nq, step_w - PAGE_SIZE), _INT_MIN, jnp.int32)], axis=1)
        chunks = [(st, c) for st in range(n_steps + 1) for c in range(PAGES_PER_STEP)]

        def chunk_of(st, c):
            return (keys_scr[st, :, c * PAGE_SIZE:(c + 1) * PAGE_SIZE],
                    st * step_w + c * PAGE_SIZE + col)

        def bisect(it, ans_u):
            cand_u = ans_u | lax.shift_left(jnp.int32(1), 31 - it)
            cand_s = cand_u ^ _INT_MIN
            cnt = jnp.zeros((nq, PAGE_SIZE), f32)
            for st in range(n_steps + 1):
                for c in range(PAGES_PER_STEP):
                    chunk = keys_scr[st, :, c * PAGE_SIZE:(c + 1) * PAGE_SIZE]
                    cnt = cnt + jnp.where(chunk >= cand_s, 1.0, 0.0)
            cnt = cnt.sum(axis=1, keepdims=True)
            return jnp.where(cnt >= ktop, cand_u, ans_u)

        ans_u = lax.fori_loop(0, 32, bisect, jnp.zeros((nq, 1), jnp.int32))
        thr = jnp.maximum(ans_u ^ _INT_MIN, _INT_MIN + 1)
        thr_out[...] = jnp.broadcast_to(thr, (nq, PAGE_SIZE))

        n_ge = jnp.zeros((nq, PAGE_SIZE), f32)
        n_gt = jnp.zeros((nq, PAGE_SIZE), f32)
        for st, c in chunks:
            kk, _ = chunk_of(st, c)
            n_ge = n_ge + jnp.where(kk >= thr, 1.0, 0.0)
            n_gt = n_gt + jnp.where(kk > thr, 1.0, 0.0)
        n_ge = n_ge.sum(axis=1, keepdims=True)
        n_gt = n_gt.sum(axis=1, keepdims=True)
        need = jnp.where(n_ge >= ktop, ktop - n_gt, float(2 ** 30))
        idx_bits = max(1, (past + PAGE_SIZE - 1).bit_length())

        def bisect_idx(it, pos):
            cand = pos | lax.shift_left(jnp.int32(1), idx_bits - 1 - it)
            cnt = jnp.zeros((nq, PAGE_SIZE), f32)
            for st, c in chunks:
                kk, idx = chunk_of(st, c)
                cnt = cnt + jnp.where(kk == thr, jnp.where(idx < cand, 1.0, 0.0), 0.0)
            cnt = cnt.sum(axis=1, keepdims=True)
            return jnp.where(cnt < need, cand, pos)

        last = lax.fori_loop(0, idx_bits, bisect_idx, jnp.zeros((nq, 1), jnp.int32))

        def demoted(st, c):
            kk, idx = chunk_of(st, c)
            return jnp.where(kk == thr, jnp.where(idx > last, kk - 1, kk), kk)

        for st in range(n_steps):
            for c in range(PAGES_PER_STEP):
                lo = st * step_w + c * PAGE_SIZE
                keys_out[:, lo:lo + PAGE_SIZE] = demoted(st, c)
        knew_out[...] = demoted(n_steps, 0)


def _sample_attn_body(pt_ref, q_ref, keys_ref, knew_ref, thr_ref, kn_ref, vn_ref, *rest, nq, n_steps):
    del pt_ref
    kp = rest[:PAGES_PER_STEP]
    vp = rest[PAGES_PER_STEP:2 * PAGES_PER_STEP]
    o_ref, m_ref, l_ref, acc_ref = rest[2 * PAGES_PER_STEP:]
    f32, bf16 = jnp.float32, jnp.bfloat16
    j = pl.program_id(1)
    scale = HEAD_DIM ** -0.5
    thr = thr_ref[...]

    def update(keys_q, thr_q, k_blocks, v_blocks):
        sel = jnp.concatenate([keys_q] * GROUP, axis=0) >= jnp.concatenate([thr_q] * GROUP, axis=0)
        for n in range(KV_HEADS):
            qn = q_ref[n]
            s = jnp.concatenate(
                [lax.dot_general(qn, kb[pl.ds(n, PAGE_SIZE, stride=KV_HEADS), :].astype(bf16), _NT,
                                 preferred_element_type=f32)
                 for kb in k_blocks], axis=1) * scale
            s = jnp.where(sel, s, _NEG_BIG)
            m_old = m_ref[n]
            m_new = jnp.maximum(m_old, s.max(axis=1, keepdims=True))
            p = jnp.where(sel, jnp.exp(s - m_new), 0.0)
            alpha = jnp.exp(m_old - m_new)
            l_ref[n] = alpha * l_ref[n] + p.sum(axis=1, keepdims=True)
            pb = p.astype(bf16)
            pv = jnp.zeros((GROUP * nq, HEAD_DIM), f32)
            for c, vb in enumerate(v_blocks):
                pv = pv + jnp.dot(pb[:, c * PAGE_SIZE:(c + 1) * PAGE_SIZE],
                                  vb[pl.ds(n, PAGE_SIZE, stride=KV_HEADS), :].astype(bf16),
                                  preferred_element_type=f32)
            acc_ref[n] = alpha * acc_ref[n] + pv
            m_ref[n] = m_new

    @pl.when(j == 0)
    def _():
        m_ref[...] = jnp.full(m_ref.shape, _NEG_BIG, f32)
        l_ref[...] = jnp.zeros(l_ref.shape, f32)
        acc_ref[...] = jnp.zeros(acc_ref.shape, f32)
        update(knew_ref[...], thr, [kn_ref], [vn_ref])

    thr_w = jnp.concatenate([thr] * PAGES_PER_STEP, axis=1)
    update(keys_ref[...], thr_w, kp, vp)

    @pl.when(j == n_steps - 1)
    def _():
        for n in range(KV_HEADS):
            o_ref[n] = acc_ref[n] / l_ref[n]


def _dsa_sample_pallas(q, k, v, qi, ki, wi, cache_k, cache_v, cache_idx_k, page_table, interpret=False):
    B, T = q.shape[:2]
    n_pages = page_table.shape[1]
    past = n_pages * PAGE_SIZE
    ktop = min(INDEX_TOPK, (past + T) // 4)
    n_steps = n_pages // PAGES_PER_STEP
    step_w = PAGES_PER_STEP * PAGE_SIZE
    f32, bf16 = jnp.float32, jnp.bfloat16
    kvw = KV_HEADS * HEAD_DIM
    qi_s = jnp.transpose(qi.astype(bf16), (0, 2, 1, 3)).reshape(B, IDX_HEADS * T, IDX_DIM)
    w_s = jnp.transpose(wi.astype(f32), (0, 2, 1)).reshape(B, IDX_HEADS * T, 1)
    pad_rows = lambda a: jnp.pad(a, ((0, 0), (0, PAGE_SIZE - T)) + ((0, 0),) * (a.ndim - 2))
    ki_new = pad_rows(ki.astype(f32))
    slot_rows = PAGE_SIZE * KV_HEADS
    k_new = pad_rows(k).reshape(B, slot_rows, HEAD_DIM)
    v_new = pad_rows(v).reshape(B, slot_rows, HEAD_DIM)
    q_s = jnp.transpose(q.astype(bf16).reshape(B, T, KV_HEADS, GROUP, HEAD_DIM),
                        (0, 2, 3, 1, 4)).reshape(B, KV_HEADS, GROUP * T, HEAD_DIM)
    ck = cache_k.reshape(cache_k.shape[0], slot_rows, HEAD_DIM)
    cv = cache_v.reshape(cache_v.shape[0], slot_rows, HEAD_DIM)

    def page_map(r):
        return lambda b, j, pt: (pt[b, j * PAGES_PER_STEP + r], 0, 0)

    per_b3 = lambda b, j, pt: (b, 0, 0)
    idx_pages = [pl.BlockSpec((None, PAGE_SIZE, IDX_DIM), page_map(r)) for r in range(PAGES_PER_STEP)]
    keys, knew, thr = pl.pallas_call(
        functools.partial(_sample_index_body, ktop=ktop, nq=T, n_steps=n_steps),
        grid_spec=pltpu.PrefetchScalarGridSpec(
            num_scalar_prefetch=1,
            grid=(B, n_steps),
            in_specs=[pl.BlockSpec((None, IDX_HEADS * T, IDX_DIM), per_b3),
                      pl.BlockSpec((None, IDX_HEADS * T, 1), per_b3),
                      pl.BlockSpec((None, PAGE_SIZE, IDX_DIM), per_b3)] + idx_pages,
            out_specs=[pl.BlockSpec((None, T, past), per_b3),
                       pl.BlockSpec((None, T, PAGE_SIZE), per_b3),
                       pl.BlockSpec((None, T, PAGE_SIZE), per_b3)],
            scratch_shapes=[pltpu.VMEM((n_steps + 1, T, step_w), jnp.int32)]),
        out_shape=[jax.ShapeDtypeStruct((B, T, past), jnp.int32),
                   jax.ShapeDtypeStruct((B, T, PAGE_SIZE), jnp.int32),
                   jax.ShapeDtypeStruct((B, T, PAGE_SIZE), jnp.int32)],
        compiler_params=pltpu.CompilerParams(dimension_semantics=("arbitrary", "arbitrary")),
        name="sample_index",
        interpret=interpret,
    )(page_table, qi_s, w_s, ki_new, *([cache_idx_k] * PAGES_PER_STEP))

    kv_pages = [pl.BlockSpec((None, slot_rows, HEAD_DIM), page_map(r)) for r in range(PAGES_PER_STEP)]
    o = pl.pallas_call(
        functools.partial(_sample_attn_body, nq=T, n_steps=n_steps),
        grid_spec=pltpu.PrefetchScalarGridSpec(
            num_scalar_prefetch=1,
            grid=(B, n_steps),
            in_specs=[pl.BlockSpec((None, KV_HEADS, GROUP * T, HEAD_DIM), lambda b, j, pt: (b, 0, 0, 0)),
                      pl.BlockSpec((None, T, step_w), lambda b, j, pt: (b, 0, j)),
                      pl.BlockSpec((None, T, PAGE_SIZE), per_b3),
                      pl.BlockSpec((None, T, PAGE_SIZE), per_b3),
                      pl.BlockSpec((None, slot_rows, HEAD_DIM), per_b3),
                      pl.BlockSpec((None, slot_rows, HEAD_DIM), per_b3)] + kv_pages + kv_pages,
            out_specs=pl.BlockSpec((None, KV_HEADS, GROUP * T, HEAD_DIM), lambda b, j, pt: (b, 0, 0, 0)),
            scratch_shapes=[pltpu.VMEM((KV_HEADS, GROUP * T, 1), f32),
                            pltpu.VMEM((KV_HEADS, GROUP * T, 1), f32),
                            pltpu.VMEM((KV_HEADS, GROUP * T, HEAD_DIM), f32)]),
        out_shape=jax.ShapeDtypeStruct((B, KV_HEADS, GROUP * T, HEAD_DIM), f32),
        compiler_params=pltpu.CompilerParams(dimension_semantics=("arbitrary", "arbitrary")),
        name="sample_attn",
        interpret=interpret,
    )(page_table, q_s, keys, knew, thr, k_new, v_new, *([ck] * PAGES_PER_STEP), *([cv] * PAGES_PER_STEP))
    o = o.reshape(B, KV_HEADS, GROUP, T, HEAD_DIM)
    return jnp.transpose(o, (0, 3, 1, 2, 4)).reshape(B, T, ATTN_HEADS * HEAD_DIM)


GDN_ROWS = 512
_TN = (((0,), (0,)), ((), ()))


def _mm(a, b):
    return jnp.dot(a.astype(jnp.bfloat16), b.astype(jnp.bfloat16), preferred_element_type=jnp.float32)


def _split2(a):
    hi = a.astype(jnp.bfloat16)
    lo = (a - hi.astype(jnp.float32)).astype(jnp.bfloat16)
    return hi, lo


def _mm3(a, b):
    f32 = jnp.float32
    a1, a2 = _split2(a)
    b1, b2 = _split2(b)
    return (jnp.dot(a1, b1, preferred_element_type=f32) + jnp.dot(a1, b2, preferred_element_type=f32)
            + jnp.dot(a2, b1, preferred_element_type=f32))


def _split3(a):
    f32 = jnp.float32
    p1 = a.astype(jnp.bfloat16)
    r = a - p1.astype(f32)
    p2 = r.astype(jnp.bfloat16)
    p3 = (r - p2.astype(f32)).astype(jnp.bfloat16)
    return p1, p2, p3


def _sigmoid(x):
    return 1.0 / (1.0 + jnp.exp(-x))


def _softplus(x):
    return jnp.maximum(x, 0.0) + jnp.log1p(jnp.exp(-jnp.abs(x)))


def _gdn_body(x_ref, araw_ref, braw_ref, arawT_ref, gate_ref, buf_ref, s0_ref, cw_ref,
              alog_ref, dtb_ref, alogT_ref, dtbT_ref, nw_ref,
              ob_ref, sout_ref, xp_scr, y_scr, s_scr, *, rb, t_valid, t_pad):
    f32, bf16 = jnp.float32, jnp.bfloat16
    C = GDN_CHUNK
    j = pl.program_id(1)

    @pl.when(j == 0)
    def _():
        xp_scr[5:8, :] = buf_ref[...]
        s_scr[...] = s0_ref[...]

    xp_scr[8:8 + rb, :] = x_ref[...]
    y = xp_scr[5:5 + rb, :] * cw_ref[0:1, :]
    for t in range(1, CONV_W):
        y = y + xp_scr[5 + t:5 + t + rb, :] * cw_ref[t:t + 1, :]
    y_scr[...] = y * _sigmoid(y)
    xp_scr[5:8, :] = xp_scr[rb + 5:rb + 8, :]

    ri = lax.broadcasted_iota(jnp.int32, (C, C), 0)
    ci = lax.broadcasted_iota(jnp.int32, (C, C), 1)
    tri_incl = jnp.where(ri >= ci, 1.0, 0.0).astype(bf16)
    tri_inclT = jnp.where(ci >= ri, 1.0, 0.0).astype(bf16)
    causal = ri >= ci
    strict = ri > ci
    eye = jnp.where(ri == ci, 1.0, 0.0)
    neg_a = -jnp.exp(alog_ref[...])
    neg_aT = -jnp.exp(alogT_ref[...])
    q_scale = GDN_DK ** -0.5

    def chunk(c, carry):
        r0 = pl.multiple_of(c * C, C)
        g_c = neg_a * _softplus(araw_ref[pl.ds(r0, C), :] + dtb_ref[...])
        beta_c = _sigmoid(braw_ref[pl.ds(r0, C), :])
        g_r = neg_aT * _softplus(arawT_ref[c] + dtbT_ref[...])
        if t_valid < t_pad:
            base = j * rb + r0
            row_ok = (base + lax.broadcasted_iota(jnp.int32, (C, GDN_HEADS), 0)) < t_valid
            col_ok = (base + lax.broadcasted_iota(jnp.int32, (GDN_HEADS, C), 1)) < t_valid
            g_c = jnp.where(row_ok, g_c, 0.0)
            beta_c = jnp.where(row_ok, beta_c, 0.0)
            g_r = jnp.where(col_ok, g_r, 0.0)
        gc_c = sum(jnp.dot(tri_incl, p, preferred_element_type=f32) for p in _split3(g_c))
        gc_r = sum(jnp.dot(p, tri_inclT, preferred_element_type=f32) for p in _split3(g_r))
        H = range(GDN_HEADS)
        q, k, v, gcol, glast, beta, decay, eg = [], [], [], [], [], [], [], []
        for h in H:
            lo = h * GDN_DK
            qh = y_scr[pl.ds(r0, C), lo:lo + GDN_DK]
            kh = y_scr[pl.ds(r0, C), GDN_HEADS * GDN_DK + lo:GDN_HEADS * GDN_DK + lo + GDN_DK]
            v.append(y_scr[pl.ds(r0, C),
                           2 * GDN_HEADS * GDN_DK + h * GDN_DV:2 * GDN_HEADS * GDN_DK + (h + 1) * GDN_DV])
            q.append(qh * lax.rsqrt(jnp.sum(qh * qh, axis=-1, keepdims=True) + NORM_EPS) * q_scale)
            k.append(kh * lax.rsqrt(jnp.sum(kh * kh, axis=-1, keepdims=True) + NORM_EPS))
            gcol.append(gc_c[:, h:h + 1])
            glast.append(gc_c[C - 1:C, h:h + 1])
            beta.append(beta_c[:, h:h + 1])
            decay.append(jnp.exp(jnp.where(causal, gcol[h] - gc_r[h:h + 1, :], -jnp.inf)))
            eg.append(jnp.exp(gcol[h]))
        kb = [k[h] * beta[h] for h in H]
        kk = [lax.dot_general(kb[h].astype(bf16), k[h].astype(bf16), _NT, preferred_element_type=f32) for h in H]
        qk = [lax.dot_general(q[h].astype(bf16), k[h].astype(bf16), _NT, preferred_element_type=f32) * decay[h]
              for h in H]
        pw = [jnp.where(strict, -(kk[h] * decay[h]), 0.0) for h in H]
        inv = [eye + pw[h] for h in H]
        for _ in range(5):
            pw = [_mm3(pw[h], pw[h]) for h in H]
            inv = [inv[h] + _mm3(inv[h], pw[h]) for h in H]
        sol = [_mm3(inv[h], jnp.concatenate([v[h] * beta[h], kb[h] * eg[h]], axis=1)) for h in H]
        s_old = [s_scr[h] for h in H]
        v_new = [sol[h][:, :GDN_DV] - _mm(sol[h][:, GDN_DV:], s_old[h]) for h in H]
        o = [_mm(q[h] * eg[h], s_old[h]) + _mm(qk[h], v_new[h]) for h in H]
        for h in H:
            kd = k[h] * jnp.exp(glast[h] - gcol[h])
            s_scr[h] = s_old[h] * jnp.exp(glast[h]) + lax.dot_general(
                kd.astype(bf16), v_new[h].astype(bf16), _TN, preferred_element_type=f32)
        for h in H:
            on = o[h] * lax.rsqrt(jnp.mean(o[h] * o[h], axis=-1, keepdims=True) + NORM_EPS) * nw_ref[...]
            gt = gate_ref[pl.ds(r0, C), h * GDN_DV:(h + 1) * GDN_DV]
            ob_ref[pl.ds(r0, C), h * GDN_DV:(h + 1) * GDN_DV] = on * (gt * _sigmoid(gt))
        return carry

    lax.fori_loop(0, rb // C, chunk, 0)

    @pl.when(j == pl.num_programs(1) - 1)
    def _():
        sout_ref[...] = s_scr[...]


def _gdn_pallas(qkv, b_raw, a_raw, gate, conv_buf, S0, conv_w, a_log, dt_bias, gdn_norm_w,
                qkv_blk=0, gate_blk=0, interpret=False):
    B, T, _ = qkv.shape
    assert T >= CONV_W - 1
    f32 = jnp.float32
    C = GDN_CHUNK
    t_pad = -(-T // C) * C
    rb = min(GDN_ROWS, t_pad)
    assert t_pad % rb == 0
    pad = lambda a: jnp.pad(a, ((0, 0), (0, t_pad - T), (0, 0)))
    x, a_p, b_p = pad(qkv), pad(a_raw), pad(b_raw)
    gate_p = x if gate is qkv else pad(gate)
    a_t = jnp.transpose(a_p.reshape(B, t_pad // C, C, GDN_HEADS), (0, 1, 3, 2))
    hd = GDN_HEADS * GDN_DV
    row_blk = lambda b, j: (b, j, 0)
    fix2 = lambda b, j: (0, 0)
    ob, s_out = pl.pallas_call(
        functools.partial(_gdn_body, rb=rb, t_valid=T, t_pad=t_pad),
        grid=(B, t_pad // rb),
        in_specs=[pl.BlockSpec((None, rb, CONV_DIM), lambda b, j: (b, j, qkv_blk)),
                  pl.BlockSpec((None, rb, GDN_HEADS), row_blk),
                  pl.BlockSpec((None, rb, GDN_HEADS), row_blk),
                  pl.BlockSpec((None, rb // C, GDN_HEADS, C), lambda b, j: (b, j, 0, 0)),
                  pl.BlockSpec((None, rb, hd), lambda b, j: (b, j, gate_blk)),
                  pl.BlockSpec((None, CONV_W - 1, CONV_DIM), lambda b, j: (b, 0, 0)),
                  pl.BlockSpec((None, GDN_HEADS, GDN_DK, GDN_DV), lambda b, j: (b, 0, 0, 0)),
                  pl.BlockSpec((CONV_W, CONV_DIM), fix2),
                  pl.BlockSpec((1, GDN_HEADS), fix2),
                  pl.BlockSpec((1, GDN_HEADS), fix2),
                  pl.BlockSpec((GDN_HEADS, 1), fix2),
                  pl.BlockSpec((GDN_HEADS, 1), fix2),
                  pl.BlockSpec((1, GDN_DV), fix2)],
        out_specs=[pl.BlockSpec((None, rb, hd), row_blk),
                   pl.BlockSpec((None, GDN_HEADS, GDN_DK, GDN_DV), lambda b, j: (b, 0, 0, 0))],
        out_shape=[jax.ShapeDtypeStruct((B, t_pad, hd), f32),
                   jax.ShapeDtypeStruct((B, GDN_HEADS, GDN_DK, GDN_DV), f32)],
        scratch_shapes=[pltpu.VMEM((rb + 8, CONV_DIM), f32),
                        pltpu.VMEM((rb, CONV_DIM), f32),
                        pltpu.VMEM((GDN_HEADS, GDN_DK, GDN_DV), f32)],
        compiler_params=pltpu.CompilerParams(
            dimension_semantics=("arbitrary", "arbitrary"),
            vmem_limit_bytes=56 * 1024 * 1024),
        name="gated_deltanet",
        interpret=interpret,
    )(x, a_p, b_p, a_t, gate_p, conv_buf, S0, conv_w,
      a_log.reshape(1, GDN_HEADS), dt_bias.reshape(1, GDN_HEADS),
      a_log.reshape(GDN_HEADS, 1), dt_bias.reshape(GDN_HEADS, 1), gdn_norm_w.reshape(1, GDN_DV))
    return ob[:, :T], s_out, qkv[:, T - (CONV_W - 1):, qkv_blk * CONV_DIM:(qkv_blk + 1) * CONV_DIM]


_SQRT_HALF = 0.7071067811865476


def _top_rows(x, k):
    R, n = x.shape
    ri = lax.broadcasted_iota(jnp.int32, (R, n), 0).astype(jnp.float32)
    ki = lax.broadcasted_iota(jnp.int32, (k, n), 0)

    def body(r, c):
        x, out, idx = c
        m = x.max(axis=0, keepdims=True)
        first = jnp.min(jnp.where(x == m, ri, float(R)), axis=0, keepdims=True)
        x = jnp.where(ri == first, -jnp.inf, x)
        return x, jnp.where(ki == r, m, out), jnp.where(ki == r, first, idx)

    zk = jnp.zeros((k, n), jnp.float32)
    rest, out, idx = lax.fori_loop(0, k, body, (x, zk, zk))
    return out, idx, rest.max(axis=0, keepdims=True)


def _peer_front_body(x_ref, oa_ref, ob_ref, wo_ref, g_ref, wq_ref, keys_ref,
                     h_ref, xn_ref, s1_ref, s2_ref, e2_ref, aux_ref, eidx_ref, gsm_ref, flag_ref):
    f32, bf16 = jnp.float32, jnp.bfloat16
    half_w = ATTN_HEADS * HEAD_DIM
    h = (x_ref[...]
         + jnp.dot(oa_ref[...].astype(bf16), wo_ref[:half_w, :], preferred_element_type=f32)
         + jnp.dot(ob_ref[...].astype(bf16), wo_ref[half_w:, :], preferred_element_type=f32))
    h_ref[...] = h
    xn = (h * lax.rsqrt(jnp.mean(h * h, axis=-1, keepdims=True) + NORM_EPS) * g_ref[...]).astype(bf16)
    xn_ref[...] = xn
    qh = jnp.dot(xn, wq_ref[...], preferred_element_type=f32).astype(bf16)
    tq = qh.shape[0]
    hq = PEER_QDIM // 2
    K = PEER_TOPK
    for hh in range(PEER_HEADS):
        tops = []
        for half in range(2):
            col = (hh * 2 + half) * hq
            sT = lax.dot_general(keys_ref[half, hh], qh[:, col:col + hq], _NT,
                                 preferred_element_type=f32)
            (s1_ref if half == 0 else s2_ref)[hh] = sT
            tops.append(_top_rows(sT, K))
        (a16, ia, a_next), (b16, ib, b_next) = tops
        cand = jnp.concatenate(
            [a16[r:r + 1, :] + b16[0:8, :] for r in range(8)]
            + [a16[0:1, :] + b16[8:16, :], a16[8:16, :] + b16[0:1, :]], axis=0)
        tau = _top_rows(cand, K)[0][K - 1:K, :]
        top_sum = a16[0:1, :] + b16[0:1, :]
        keep = cand >= tau
        z = jnp.sum(jnp.where(keep, jnp.exp(cand - top_sum), 0.0), axis=0, keepdims=True)
        e2_ref[hh] = jnp.exp(s2_ref[hh] - b16[0:1, :]) / z
        aux_ref[hh] = jnp.concatenate([tau, a16[0:1, :], jnp.zeros((6, tq), f32)], axis=0)

        tied = (jnp.sum(jnp.where(keep, 1.0, 0.0), axis=0, keepdims=True) > K)
        tied = tied | ((a_next == a16[K - 1:K, :]) & (a16[K - 1:K, :] + b16[0:1, :] >= tau))
        tied = tied | ((b_next == b16[K - 1:K, :]) & (a16[0:1, :] + b16[K - 1:K, :] >= tau))
        any_tied = jnp.max(jnp.where(tied, 1.0, 0.0)) > 0.0
        flag_ref[0, hh:hh + 1, :] = jnp.full((1, LANES), jnp.where(any_tied, 1.0, 0.0), f32)

        @pl.when(jnp.logical_not(any_tied))
        def _():
            eidx_ref[hh] = jnp.zeros((K, tq), jnp.int32)
            gsm_ref[hh] = jnp.zeros((K, tq), f32)

        @pl.when(any_tied)
        def _():
            full = jnp.concatenate([a16[r:r + 1, :] + b16 for r in range(K)], axis=0)
            sv, flat, _ = _top_rows(full, K)
            ra = jnp.floor(flat * (1.0 / K))
            rb = flat - ra * K
            i1 = jnp.zeros((K, tq), f32)
            i2 = jnp.zeros((K, tq), f32)
            for r in range(K):
                i1 = jnp.where(ra == r, ia[r:r + 1, :], i1)
                i2 = jnp.where(rb == r, ib[r:r + 1, :], i2)
            eidx_ref[hh] = (i1 * PEER_NKEYS + i2).astype(jnp.int32)
            ex = jnp.exp(sv - sv[0:1, :])
            gsm_ref[hh] = ex / jnp.sum(ex, axis=0, keepdims=True)


def _peer_front(x, oa, ob, wo, g, wq, keys, tq, interpret=False):
    n = x.shape[0]
    half_w = ATTN_HEADS * HEAD_DIM
    tok = lambda i: (i, 0)
    fix2 = lambda i: (0, 0)
    colT = lambda i: (0, 0, i)
    f32 = jnp.float32
    plane = jax.ShapeDtypeStruct((PEER_HEADS, PEER_NKEYS, n), f32)
    plane_spec = pl.BlockSpec((PEER_HEADS, PEER_NKEYS, tq), colT)
    return pl.pallas_call(
        _peer_front_body,
        grid=(n // tq,),
        in_specs=[pl.BlockSpec((tq, D_MODEL), tok),
                  pl.BlockSpec((tq, half_w), tok),
                  pl.BlockSpec((tq, half_w), tok),
                  pl.BlockSpec((D_MODEL, D_MODEL), fix2),
                  pl.BlockSpec((1, D_MODEL), fix2),
                  pl.BlockSpec((D_MODEL, PEER_HEADS * PEER_QDIM), fix2),
                  pl.BlockSpec((2, PEER_HEADS, PEER_NKEYS, PEER_QDIM // 2), lambda i: (0, 0, 0, 0))],
        out_specs=[pl.BlockSpec((tq, D_MODEL), tok),
                   pl.BlockSpec((tq, D_MODEL), tok),
                   plane_spec, plane_spec, plane_spec,
                   pl.BlockSpec((PEER_HEADS, 8, tq), colT),
                   pl.BlockSpec((PEER_HEADS, PEER_TOPK, tq), colT),
                   pl.BlockSpec((PEER_HEADS, PEER_TOPK, tq), colT),
                   pl.BlockSpec((1, 8, LANES), lambda i: (i, 0, 0))],
        out_shape=[jax.ShapeDtypeStruct((n, D_MODEL), f32),
                   jax.ShapeDtypeStruct((n, D_MODEL), jnp.bfloat16),
                   plane, plane, plane,
                   jax.ShapeDtypeStruct((PEER_HEADS, 8, n), f32),
                   jax.ShapeDtypeStruct((PEER_HEADS, PEER_TOPK, n), jnp.int32),
                   jax.ShapeDtypeStruct((PEER_HEADS, PEER_TOPK, n), f32),
                   jax.ShapeDtypeStruct((n // tq, 8, LANES), f32)],
        compiler_params=pltpu.CompilerParams(
            dimension_semantics=("arbitrary",),
            vmem_limit_bytes=56 * 1024 * 1024),
        name="peer_front",
        interpret=interpret,
    )(x, oa, ob, wo, g.reshape(1, D_MODEL), wq, keys)


def _peer_dense_body(xn_ref, u_ref, vT_ref, s1_ref, s2_ref, e2_ref, aux_ref, eidx_ref, gsm_ref, flag_ref,
                     yT_ref, g_scr, *, eblk):
    f32 = jnp.float32
    eb = pl.program_id(1)
    tq = xn_ref.shape[0]
    sub = eblk // PEER_NKEYS

    @pl.when(eb == 0)
    def _():
        yT_ref[...] = jnp.zeros(yT_ref.shape, f32)

    def finish(gate_of):
        a = lax.dot_general(u_ref[...], xn_ref[...], _NT, preferred_element_type=f32)
        act = 0.5 * a * (1.0 + lax.erf(a * _SQRT_HALF))
        hT = jnp.concatenate(
            [(gate_of(r) * act[r * PEER_NKEYS:(r + 1) * PEER_NKEYS, :]).astype(jnp.bfloat16) for r in range(sub)],
            axis=0)
        yT_ref[...] += jnp.dot(vT_ref[...], hT, preferred_element_type=f32)

    exact_lists = jnp.max(flag_ref[...]) > 0.0

    def head_gate(r, hh):
        s1row = s1_ref[hh, pl.ds(eb * sub + r, 1), :]
        tau = aux_ref[hh, 0:1, :]
        e1row = jnp.exp(s1row - aux_ref[hh, 1:2, :])
        return jnp.where(s1row + s2_ref[hh] >= tau, e1row * e2_ref[hh], 0.0)

    @pl.when(jnp.logical_not(exact_lists))
    def _():
        finish(lambda r: sum(head_gate(r, hh) for hh in range(PEER_HEADS)))

    @pl.when(exact_lists)
    def _():
        n_flags = flag_ref.shape[0]
        row = lax.broadcasted_iota(jnp.int32, (PEER_NKEYS, tq), 0)
        for r in range(sub):
            base = (eb * sub + r) * PEER_NKEYS
            gate = jnp.zeros((PEER_NKEYS, tq), f32)
            for hh in range(PEER_HEADS):
                def with_list(g, hh=hh):
                    use_list = jnp.concatenate(
                        [jnp.broadcast_to(flag_ref[f, hh:hh + 1, 0:1], (1, tq // n_flags)) for f in range(n_flags)],
                        axis=1) > 0.0
                    ids = eidx_ref[hh] - base
                    wts = gsm_ref[hh]
                    listed = jnp.zeros((PEER_NKEYS, tq), f32)
                    for k in range(PEER_TOPK):
                        listed = listed + jnp.where(row == ids[k:k + 1, :], wts[k:k + 1, :], 0.0)
                    return g + jnp.where(use_list, listed, head_gate(r, hh))

                gate = lax.cond(jnp.max(flag_ref[:, hh:hh + 1, :]) > 0.0, with_list,
                                lambda g, hh=hh: g + head_gate(r, hh), gate)
            g_scr[r * PEER_NKEYS:(r + 1) * PEER_NKEYS, :] = gate
        finish(lambda r: g_scr[r * PEER_NKEYS:(r + 1) * PEER_NKEYS, :])


def _peer_dense(xn, u, vT, s1, s2, e2, aux, eidx, gsm, flag, tq, eblk, interpret=False):
    n = xn.shape[0]
    ne = u.shape[0]
    flags_per_tile = flag.shape[0] * tq // n
    colT = lambda i, e: (0, 0, i)
    plane_spec = pl.BlockSpec((PEER_HEADS, PEER_NKEYS, tq), colT)
    list_spec = pl.BlockSpec((PEER_HEADS, PEER_TOPK, tq), colT)
    return pl.pallas_call(
        functools.partial(_peer_dense_body, eblk=eblk),
        grid=(n // tq, ne // eblk),
        in_specs=[pl.BlockSpec((tq, D_MODEL), lambda i, e: (i, 0)),
                  pl.BlockSpec((eblk, D_MODEL), lambda i, e: (e, 0)),
                  pl.BlockSpec((D_MODEL, eblk), lambda i, e: (0, e)),
                  plane_spec, plane_spec, plane_spec,
                  pl.BlockSpec((PEER_HEADS, 8, tq), colT),
                  list_spec, list_spec,
                  pl.BlockSpec((flags_per_tile, 8, LANES), lambda i, e: (i, 0, 0))],
        out_specs=pl.BlockSpec((D_MODEL, tq), lambda i, e: (0, i)),
        out_shape=jax.ShapeDtypeStruct((D_MODEL, n), jnp.float32),
        scratch_shapes=[pltpu.VMEM((eblk, tq), jnp.float32)],
        compiler_params=pltpu.CompilerParams(
            dimension_semantics=("arbitrary", "arbitrary"),
            vmem_limit_bytes=56 * 1024 * 1024),
        name="peer_dense",
        interpret=interpret,
    )(xn, u, vT, s1, s2, e2, aux, eidx, gsm, flag)


def _layer_out_pallas(x, oa, ob, wo_b, ffn_norm_w, wq_b, keys_b, u_b, vT_b, interpret=False):
    n = x.shape[0]
    tq1 = min(256, n)
    tq2 = 512 if n % 512 == 0 else min(256, n)
    h, xn, s1, s2, e2, aux, eidx, gsm, flag = _peer_front(x, oa, ob, wo_b, ffn_norm_w, wq_b, keys_b, tq1, interpret)
    yT = _peer_dense(xn, u_b, vT_b, s1, s2, e2, aux, eidx, gsm, flag, tq2, 512, interpret)
    return h + yT.T


def kernel(x_prompt, x_sample, cache_k, cache_v, cache_idx_k, state_ssm, state_conv, page_table,
           attn_norm_w, w_in, q_norm_w, k_norm_w, idx_k_norm_w, conv_w, a_log, dt_bias, gdn_norm_w,
           w_out, ffn_norm_w, peer_wq, peer_keys, peer_u, peer_v):
    l = 0
    proj_w = (attn_norm_w[l], _permute_w_in(w_in[l]), q_norm_w[l], k_norm_w[l], idx_k_norm_w[l])
    gdn_cols = dict(qkv_blk=_DST["qkv"] // CONV_DIM, gate_blk=_DST["gate"] // (GDN_HEADS * GDN_DV))
    gdn_w = (conv_w[l], a_log[l], dt_bias[l], gdn_norm_w[l])
    bf16 = jnp.bfloat16
    out_w = (w_out[l].astype(bf16), ffn_norm_w[l], peer_wq[l].astype(bf16), peer_keys[l].astype(bf16),
             peer_u[l].astype(bf16), peer_v[l].astype(bf16).T)

    hp, hs = x_prompt, x_sample
    (qa, ka, va, qi, ki, wi), (z, b_raw, a_raw) = _in_projection(hp, *proj_w)
    oa = _dsa_prompt_pallas(qa[0], ka[0], va[0], qi[0], ki[0], wi[0])[None]
    Bp = hp.shape[0]
    buf0 = jnp.zeros((Bp, CONV_W - 1, CONV_DIM), hp.dtype)
    S0 = jnp.zeros((Bp, GDN_HEADS, GDN_DK, GDN_DV), jnp.float32)
    ob, S_p, buf_p = _gdn_pallas(z, b_raw, a_raw, z, buf0, S0, *gdn_w, **gdn_cols)
    half_w = ATTN_HEADS * HEAD_DIM
    hp = _layer_out_pallas(hp[0], oa[0], ob[0], *out_w)[None]
    kp, vp, ip = ka, va, ki

    (qa, ka, va, qi, ki, wi), (z, b_raw, a_raw) = _in_projection(hs, *proj_w)
    oa = _dsa_sample_pallas(qa, ka, va, qi, ki, wi, cache_k[l], cache_v[l], cache_idx_k[l], page_table)
    ob, S_s, buf_s = _gdn_pallas(z, b_raw, a_raw, z, state_conv[l], state_ssm[l].astype(jnp.float32),
                                 *gdn_w, **gdn_cols)
    ns = hs.shape[0] * hs.shape[1]
    hs = _layer_out_pallas(hs.reshape(ns, D_MODEL), oa.reshape(ns, half_w), ob.reshape(ns, half_w),
                           *out_w).reshape(hs.shape)

    return (hp, hs, kp[None], vp[None], ip[None], S_p[None], buf_p[None],
            ka[None], va[None], ki[None], S_s[None], buf_s[None])
```

```python
import functools
import math

import jax
import jax.numpy as jnp
from jax import lax
from jax.experimental import pallas as pl
from jax.experimental.pallas import tpu as pltpu

D_MODEL = 2048
PAGE_SIZE = 128
HEAD_DIM = 128
ATTN_HEADS = 8
KV_HEADS = 2
GROUP = ATTN_HEADS // KV_HEADS
IDX_HEADS = 16
IDX_DIM = 64
INDEX_TOPK = 256
IDX_SCALE = (IDX_HEADS * IDX_DIM) ** -0.5
GDN_DK = 128
GDN_DV = 128
GDN_HEADS = 8
CONV_W = 4
CONV_DIM = GDN_HEADS * (2 * GDN_DK + GDN_DV)
GDN_CHUNK = 64
IN_SIZES = (ATTN_HEADS * HEAD_DIM, KV_HEADS * HEAD_DIM, KV_HEADS * HEAD_DIM,
            IDX_HEADS * IDX_DIM, IDX_DIM, IDX_HEADS,
            CONV_DIM, GDN_HEADS, GDN_HEADS, GDN_HEADS * GDN_DV)
IN_COLS = sum(IN_SIZES)
PEER_HEADS = 8
PEER_NKEYS = 128
PEER_QDIM = 256
PEER_TOPK = 16
NORM_EPS = 1e-6

LANES = 128


def _rmsnorm(x, w):
    xf = x.astype(jnp.float32)
    y = xf * lax.rsqrt(jnp.mean(xf * xf, axis=-1, keepdims=True) + NORM_EPS)
    return (y * w.astype(jnp.float32)).astype(x.dtype)


def _norm_matmul_body(x_ref, g_ref, w_ref, o_ref, xn_ref):
    @pl.when(pl.program_id(1) == 0)
    def _():
        x = x_ref[...]
        r = lax.rsqrt(jnp.mean(x * x, axis=-1, keepdims=True) + NORM_EPS)
        xn_ref[...] = (x * r * g_ref[...]).astype(jnp.bfloat16)

    o_ref[...] = jnp.dot(xn_ref[...], w_ref[...], preferred_element_type=jnp.float32)


def _norm_matmul(x, g, w, tm, tn):
    m, k = x.shape
    n = w.shape[1]
    return pl.pallas_call(
        _norm_matmul_body,
        grid=(m // tm, n // tn),
        in_specs=[pl.BlockSpec((tm, k), lambda i, j: (i, 0)),
                  pl.BlockSpec((1, k), lambda i, j: (0, 0)),
                  pl.BlockSpec((k, tn), lambda i, j: (0, j))],
        out_specs=pl.BlockSpec((tm, tn), lambda i, j: (i, j)),
        out_shape=jax.ShapeDtypeStruct((m, n), jnp.float32),
        scratch_shapes=[pltpu.VMEM((tm, k), jnp.bfloat16)],
        compiler_params=pltpu.CompilerParams(
            dimension_semantics=("arbitrary", "arbitrary"),
            vmem_limit_bytes=48 * 1024 * 1024),
        name="norm_matmul",
    )(x, g.reshape(1, k), w)


PROJ_TILE = 1024
_SRC = dict(zip(("qa", "ka", "va", "qi", "ki", "wi", "qkv", "b", "a", "gate"),
                [(sum(IN_SIZES[:i]), IN_SIZES[i]) for i in range(len(IN_SIZES))]))
_DST_ORDER = (("qkv",), ("qa",), ("qi",), ("gate",), ("ka",), ("va",), ("ki",), ("wi", "b", "a"))


def _proj_layout():
    dst, off = {}, 0
    for group in _DST_ORDER:
        for name in group:
            dst[name] = off
            off += _SRC[name][1]
        off = -(-off // LANES) * LANES
    return dst, -(-off // PROJ_TILE) * PROJ_TILE


_DST, PROJ_COLS = _proj_layout()
assert _DST["qkv"] == 0 and _DST["gate"] % (GDN_HEADS * GDN_DV) == 0


def _permute_w_in(w_in):
    pieces, off = [], 0
    for group in _DST_ORDER:
        for name in group:
            if _DST[name] > off:
                pieces.append(jnp.zeros((D_MODEL, _DST[name] - off), w_in.dtype))
            s0, n = _SRC[name]
            pieces.append(w_in[:, s0:s0 + n])
            off = _DST[name] + n
    pieces.append(jnp.zeros((D_MODEL, PROJ_COLS - off), w_in.dtype))
    return jnp.concatenate(pieces, axis=1).astype(jnp.bfloat16)


def _in_projection(x, attn_norm_w, w_in_p, q_norm_w, k_norm_w, idx_k_norm_w):
    B, T, _ = x.shape
    m = B * T
    tm = 512 if m % 512 == 0 else m
    z = _norm_matmul(x.reshape(m, D_MODEL), attn_norm_w, w_in_p, tm, PROJ_TILE).reshape(B, T, PROJ_COLS)
    col = lambda name: z[:, :, _DST[name]:_DST[name] + _SRC[name][1]]
    qa = _rmsnorm(col("qa").reshape(B, T, ATTN_HEADS, HEAD_DIM), q_norm_w)
    ka = _rmsnorm(col("ka").reshape(B, T, KV_HEADS, HEAD_DIM), k_norm_w)
    va = col("va").reshape(B, T, KV_HEADS, HEAD_DIM)
    qi = col("qi").reshape(B, T, IDX_HEADS, IDX_DIM)
    ki = _rmsnorm(col("ki"), idx_k_norm_w)
    return (qa, ka, va, qi, ki, col("wi")), (z, col("b"), col("a"))


_INT_MIN = -2 ** 31
_NEG_BIG = -1e30
_NT = (((1,), (1,)), ((), ()))


def _dsa_prompt_body(qi_ref, wT_ref, q_ref, ki_ref, k_ref, vT_ref, o_ref,
                     keys_ref, m_ref, l_ref, acc_ref, *, ktop, tq):
    f32 = jnp.float32
    i = pl.program_id(0)
    nkb = i + 1
    col_t = i * tq + lax.broadcasted_iota(jnp.int32, (tq, tq), 1)
    row_s = lax.broadcasted_iota(jnp.int32, (tq, tq), 0)

    def score_blk(kb, carry):
        kib = ki_ref[kb]
        acc = jnp.zeros((tq, tq), f32)
        for h in range(IDX_HEADS):
            s = lax.dot_general(kib, qi_ref[h], _NT, preferred_element_type=f32)
            acc = acc + jnp.maximum(s, 0.0) * wT_ref[h:h + 1, :]
        sc = acc * IDX_SCALE
        bits = lax.bitcast_convert_type(sc, jnp.int32)
        key = bits ^ (lax.shift_right_arithmetic(bits, 31) & 0x7FFFFFFF)
        valid = (kb * tq + row_s) <= col_t
        keys_ref[kb] = jnp.where(valid, key, _INT_MIN)
        return carry

    lax.fori_loop(0, nkb, score_blk, 0)

    def bisect(it, ans_u):
        cand_u = ans_u | lax.shift_left(jnp.int32(1), 31 - it)
        cand_s = cand_u ^ _INT_MIN

        def count_blk(kb, cnt):
            hit = jnp.where(keys_ref[kb] >= cand_s, 1.0, 0.0)
            return cnt + hit.reshape(tq // 8, 8, tq).sum(axis=0)

        cnt = lax.fori_loop(0, nkb, count_blk, jnp.zeros((8, tq), f32))
        cnt = cnt.sum(axis=0, keepdims=True)
        return jnp.where(cnt >= ktop, cand_u, ans_u)

    ans_u = lax.fori_loop(0, 32, bisect, jnp.zeros((1, tq), jnp.int32))
    thr = jnp.maximum(ans_u ^ _INT_MIN, _INT_MIN + 1)

    def tally(kb, c):
        kk = keys_ref[kb]
        ge = jnp.where(kk >= thr, 1.0, 0.0).reshape(tq // 8, 8, tq).sum(axis=0)
        gt = jnp.where(kk > thr, 1.0, 0.0).reshape(tq // 8, 8, tq).sum(axis=0)
        return c[0] + ge, c[1] + gt

    z8 = jnp.zeros((8, tq), f32)
    n_ge, n_gt = lax.fori_loop(0, nkb, tally, (z8, z8))
    n_ge = n_ge.sum(axis=0, keepdims=True)
    n_gt = n_gt.sum(axis=0, keepdims=True)

    @pl.when(jnp.max(n_ge) > ktop)
    def _():
        need = jnp.where(n_ge >= ktop, ktop - n_gt, float(2 ** 30))
        idx_bits = max(1, (keys_ref.shape[0] * tq - 1).bit_length())

        def bisect_idx(it, pos):
            cand = pos | lax.shift_left(jnp.int32(1), idx_bits - 1 - it)

            def count_blk(kb, cnt):
                hit = (keys_ref[kb] == thr) & ((kb * tq + row_s) < cand)
                return cnt + jnp.where(hit, 1.0, 0.0).reshape(tq // 8, 8, tq).sum(axis=0)

            cnt = lax.fori_loop(0, nkb, count_blk, z8).sum(axis=0, keepdims=True)
            return jnp.where(cnt < need, cand, pos)

        last = lax.fori_loop(0, idx_bits, bisect_idx, jnp.zeros((1, tq), jnp.int32))

        def demote(kb, carry):
            kk = keys_ref[kb]
            keys_ref[kb] = jnp.where((kk == thr) & ((kb * tq + row_s) > last), kk - 1, kk)
            return carry

        lax.fori_loop(0, nkb, demote, 0)

    m_ref[...] = jnp.full(m_ref.shape, _NEG_BIG, f32)
    l_ref[...] = jnp.zeros(l_ref.shape, f32)
    acc_ref[...] = jnp.zeros(acc_ref.shape, f32)
    c2 = HEAD_DIM ** -0.5 * math.log2(math.e)

    def attn_blk(kb, carry):
        sel = keys_ref[kb] >= thr
        kblk = k_ref[kb]
        vT = vT_ref[kb]
        s_all = [lax.dot_general(kblk[:, (h // GROUP) * HEAD_DIM:(h // GROUP + 1) * HEAD_DIM],
                                 q_ref[:, h * HEAD_DIM:(h + 1) * HEAD_DIM], _NT,
                                 preferred_element_type=f32) for h in range(ATTN_HEADS)]
        for h in range(ATTN_HEADS):
            n = h // GROUP
            s = jnp.where(sel, s_all[h], -jnp.inf)
            m_old = m_ref[h]
            m_new = jnp.maximum(m_old, s.max(axis=0, keepdims=True))
            p = jnp.exp2((s - m_new) * c2)
            alpha = jnp.exp2((m_old - m_new) * c2)
            l_ref[h] = alpha * l_ref[h] + p.sum(axis=0, keepdims=True)
            pv = jnp.dot(vT[n * HEAD_DIM:(n + 1) * HEAD_DIM, :], p.astype(jnp.bfloat16),
                         preferred_element_type=f32)
            acc_ref[h] = alpha * acc_ref[h] + pv
            m_ref[h] = m_new
        return carry

    lax.fori_loop(0, nkb, attn_blk, 0)
    for h in range(ATTN_HEADS):
        o_ref[:, h * HEAD_DIM:(h + 1) * HEAD_DIM] = (acc_ref[h] / l_ref[h]).T


def _dsa_prompt_pallas(q, k, v, qi, ki, wi, interpret=False):
    T = q.shape[0]
    tq = min(256, T)
    nb = T // tq
    ktop = min(INDEX_TOPK, T // 4)
    bf16 = jnp.bfloat16
    qh = q.astype(bf16).reshape(T, ATTN_HEADS * HEAD_DIM)
    qih = jnp.transpose(qi.astype(bf16), (1, 0, 2))
    wT = wi.astype(jnp.float32).T
    kib = ki.astype(bf16).reshape(nb, tq, IDX_DIM)
    kb = k.astype(bf16).reshape(nb, tq, KV_HEADS * HEAD_DIM)
    vT = jnp.transpose(v.astype(bf16).reshape(nb, tq, KV_HEADS * HEAD_DIM), (0, 2, 1))
    body = functools.partial(_dsa_prompt_body, ktop=ktop, tq=tq)
    return pl.pallas_call(
        body,
        grid=(nb,),
        in_specs=[pl.BlockSpec((IDX_HEADS, tq, IDX_DIM), lambda i: (0, i, 0)),
                  pl.BlockSpec((IDX_HEADS, tq), lambda i: (0, i)),
                  pl.BlockSpec((tq, ATTN_HEADS * HEAD_DIM), lambda i: (i, 0)),
                  pl.BlockSpec((nb, tq, IDX_DIM), lambda i: (0, 0, 0)),
                  pl.BlockSpec((nb, tq, KV_HEADS * HEAD_DIM), lambda i: (0, 0, 0)),
                  pl.BlockSpec((nb, KV_HEADS * HEAD_DIM, tq), lambda i: (0, 0, 0))],
        out_specs=pl.BlockSpec((tq, ATTN_HEADS * HEAD_DIM), lambda i: (i, 0)),
        out_shape=jax.ShapeDtypeStruct((T, ATTN_HEADS * HEAD_DIM), jnp.float32),
        scratch_shapes=[pltpu.VMEM((nb, tq, tq), jnp.int32),
                        pltpu.VMEM((ATTN_HEADS, 1, tq), jnp.float32),
                        pltpu.VMEM((ATTN_HEADS, 1, tq), jnp.float32),
                        pltpu.VMEM((ATTN_HEADS, HEAD_DIM, tq), jnp.float32)],
        compiler_params=pltpu.CompilerParams(
            dimension_semantics=("arbitrary",),
            vmem_limit_bytes=56 * 1024 * 1024),
        name="dsa_prompt",
        interpret=interpret,
    )(qih, wT, qh, kib, kb, vT)


PAGES_PER_STEP = 8


def _sortable_key(x):
    bits = lax.bitcast_convert_type(x, jnp.int32)
    return bits ^ (lax.shift_right_arithmetic(bits, 31) & 0x7FFFFFFF)


def _sample_index_body(pt_ref, qi_ref, w_ref, kin_ref, *rest, ktop, nq, n_steps):
    del pt_ref
    pages = rest[:PAGES_PER_STEP]
    keys_out, knew_out, thr_out, keys_scr = rest[PAGES_PER_STEP:]
    f32 = jnp.float32
    j = pl.program_id(1)
    step_w = PAGES_PER_STEP * PAGE_SIZE
    past = n_steps * step_w
    qi = qi_ref[...]
    w = w_ref[...]

    def page_keys(page):
        s = lax.dot_general(qi, page.astype(jnp.bfloat16), _NT, preferred_element_type=f32)
        s = jnp.maximum(s, 0.0) * w
        return s.reshape(IDX_HEADS, nq, PAGE_SIZE).sum(axis=0) * IDX_SCALE

    keys_scr[j] = jnp.concatenate([_sortable_key(page_keys(p[...])) for p in pages], axis=1)

    @pl.when(j == n_steps - 1)
    def _():
        kn = _sortable_key(page_keys(kin_ref[...]))
        col = lax.broadcasted_iota(jnp.int32, (nq, PAGE_SIZE), 1)
        row = lax.broadcasted_iota(jnp.int32, (nq, PAGE_SIZE), 0)
        kn = jnp.where(col <= row, kn, _INT_MIN)
        keys_scr[n_steps] = jnp.concatenate(
            [kn, jnp.full((nq, step_w - PAGE_SIZE), _INT_MIN, jnp.int32)], axis=1)
        chunks = [(st, c) for st in range(n_steps + 1) for c in range(PAGES_PER_STEP)]

        def chunk_of(st, c):
            return (keys_scr[st, :, c * PAGE_SIZE:(c + 1) * PAGE_SIZE],
                    st * step_w + c * PAGE_SIZE + col)

        def bisect(it, ans_u):
            cand_u = ans_u | lax.shift_left(jnp.int32(1), 31 - it)
            cand_s = cand_u ^ _INT_MIN
            cnt = jnp.zeros((nq, PAGE_SIZE), f32)
            for st in range(n_steps + 1):
                for c in range(PAGES_PER_STEP):
                    chunk = keys_scr[st, :, c * PAGE_SIZE:(c + 1) * PAGE_SIZE]
                    cnt = cnt + jnp.where(chunk >= cand_s, 1.0, 0.0)
            cnt = cnt.sum(axis=1, keepdims=True)
            return jnp.where(cnt >= ktop, cand_u, ans_u)

        ans_u = lax.fori_loop(0, 32, bisect, jnp.zeros((nq, 1), jnp.int32))
        thr = jnp.maximum(ans_u ^ _INT_MIN, _INT_MIN + 1)
        thr_out[...] = jnp.broadcast_to(thr, (nq, PAGE_SIZE))

        n_ge = jnp.zeros((nq, PAGE_SIZE), f32)
        n_gt = jnp.zeros((nq, PAGE_SIZE), f32)
        for st, c in chunks:
            kk, _ = chunk_of(st, c)
            n_ge = n_ge + jnp.where(kk >= thr, 1.0, 0.0)
            n_gt = n_gt + jnp.where(kk > thr, 1.0, 0.0)
        n_ge = n_ge.sum(axis=1, keepdims=True)
        n_gt = n_gt.sum(axis=1, keepdims=True)
        need = jnp.where(n_ge >= ktop, ktop - n_gt, float(2 ** 30))
        idx_bits = max(1, (past + PAGE_SIZE - 1).bit_length())

        def bisect_idx(it, pos):
            cand = pos | lax.shift_left(jnp.int32(1), idx_bits - 1 - it)
            cnt = jnp.zeros((nq, PAGE_SIZE), f32)
            for st, c in chunks:
                kk, idx = chunk_of(st, c)
                cnt = cnt + jnp.where(kk == thr, jnp.where(idx < cand, 1.0, 0.0), 0.0)
            cnt = cnt.sum(axis=1, keepdims=True)
            return jnp.where(cnt < need, cand, pos)

        last = lax.fori_loop(0, idx_bits, bisect_idx, jnp.zeros((nq, 1), jnp.int32))

        def demoted(st, c):
            kk, idx = chunk_of(st, c)
            return jnp.where(kk == thr, jnp.where(idx > last, kk - 1, kk), kk)

        for st in range(n_steps):
            for c in range(PAGES_PER_STEP):
                lo = st * step_w + c * PAGE_SIZE
                keys_out[:, lo:lo + PAGE_SIZE] = demoted(st, c)
        knew_out[...] = demoted(n_steps, 0)


def _sample_attn_body(pt_ref, q_ref, keys_ref, knew_ref, thr_ref, kn_ref, vn_ref, *rest, nq, n_steps):
    del pt_ref
    kp = rest[:PAGES_PER_STEP]
    vp = rest[PAGES_PER_STEP:2 * PAGES_PER_STEP]
    o_ref, m_ref, l_ref, acc_ref = rest[2 * PAGES_PER_STEP:]
    f32, bf16 = jnp.float32, jnp.bfloat16
    j = pl.program_id(1)
    scale = HEAD_DIM ** -0.5
    thr = thr_ref[...]

    def update(keys_q, thr_q, k_blocks, v_blocks):
        sel = jnp.concatenate([keys_q] * GROUP, axis=0) >= jnp.concatenate([thr_q] * GROUP, axis=0)
        for n in range(KV_HEADS):
            qn = q_ref[n]
            s = jnp.concatenate(
                [lax.dot_general(qn, kb[pl.ds(n, PAGE_SIZE, stride=KV_HEADS), :].astype(bf16), _NT,
                                 preferred_element_type=f32)
                 for kb in k_blocks], axis=1) * scale
            s = jnp.where(sel, s, _NEG_BIG)
            m_old = m_ref[n]
            m_new = jnp.maximum(m_old, s.max(axis=1, keepdims=True))
            p = jnp.where(sel, jnp.exp(s - m_new), 0.0)
            alpha = jnp.exp(m_old - m_new)
            l_ref[n] = alpha * l_ref[n] + p.sum(axis=1, keepdims=True)
            pb = p.astype(bf16)
            pv = jnp.zeros((GROUP * nq, HEAD_DIM), f32)
            for c, vb in enumerate(v_blocks):
                pv = pv + jnp.dot(pb[:, c * PAGE_SIZE:(c + 1) * PAGE_SIZE],
                                  vb[pl.ds(n, PAGE_SIZE, stride=KV_HEADS), :].astype(bf16),
                                  preferred_element_type=f32)
            acc_ref[n] = alpha * acc_ref[n] + pv
            m_ref[n] = m_new

    @pl.when(j == 0)
    def _():
        m_ref[...] = jnp.full(m_ref.shape, _NEG_BIG, f32)
        l_ref[...] = jnp.zeros(l_ref.shape, f32)
        acc_ref[...] = jnp.zeros(acc_ref.shape, f32)
        update(knew_ref[...], thr, [kn_ref], [vn_ref])

    thr_w = jnp.concatenate([thr] * PAGES_PER_STEP, axis=1)
    update(keys_ref[...], thr_w, kp, vp)

    @pl.when(j == n_steps - 1)
    def _():
        for n in range(KV_HEADS):
            o_ref[n] = acc_ref[n] / l_ref[n]


def _dsa_sample_pallas(q, k, v, qi, ki, wi, cache_k, cache_v, cache_idx_k, page_table, interpret=False):
    B, T = q.shape[:2]
    n_pages = page_table.shape[1]
    past = n_pages * PAGE_SIZE
    ktop = min(INDEX_TOPK, (past + T) // 4)
    n_steps = n_pages // PAGES_PER_STEP
    step_w = PAGES_PER_STEP * PAGE_SIZE
    f32, bf16 = jnp.float32, jnp.bfloat16
    kvw = KV_HEADS * HEAD_DIM
    qi_s = jnp.transpose(qi.astype(bf16), (0, 2, 1, 3)).reshape(B, IDX_HEADS * T, IDX_DIM)
    w_s = jnp.transpose(wi.astype(f32), (0, 2, 1)).reshape(B, IDX_HEADS * T, 1)
    pad_rows = lambda a: jnp.pad(a, ((0, 0), (0, PAGE_SIZE - T)) + ((0, 0),) * (a.ndim - 2))
    ki_new = pad_rows(ki.astype(f32))
    slot_rows = PAGE_SIZE * KV_HEADS
    k_new = pad_rows(k).reshape(B, slot_rows, HEAD_DIM)
    v_new = pad_rows(v).reshape(B, slot_rows, HEAD_DIM)
    q_s = jnp.transpose(q.astype(bf16).reshape(B, T, KV_HEADS, GROUP, HEAD_DIM),
                        (0, 2, 3, 1, 4)).reshape(B, KV_HEADS, GROUP * T, HEAD_DIM)
    ck = cache_k.reshape(cache_k.shape[0], slot_rows, HEAD_DIM)
    cv = cache_v.reshape(cache_v.shape[0], slot_rows, HEAD_DIM)

    def page_map(r):
        return lambda b, j, pt: (pt[b, j * PAGES_PER_STEP + r], 0, 0)

    per_b3 = lambda b, j, pt: (b, 0, 0)
    idx_pages = [pl.BlockSpec((None, PAGE_SIZE, IDX_DIM), page_map(r)) for r in range(PAGES_PER_STEP)]
    keys, knew, thr = pl.pallas_call(
        functools.partial(_sample_index_body, ktop=ktop, nq=T, n_steps=n_steps),
        grid_spec=pltpu.PrefetchScalarGridSpec(
            num_scalar_prefetch=1,
            grid=(B, n_steps),
            in_specs=[pl.BlockSpec((None, IDX_HEADS * T, IDX_DIM), per_b3),
                      pl.BlockSpec((None, IDX_HEADS * T, 1), per_b3),
                      pl.BlockSpec((None, PAGE_SIZE, IDX_DIM), per_b3)] + idx_pages,
            out_specs=[pl.BlockSpec((None, T, past), per_b3),
                       pl.BlockSpec((None, T, PAGE_SIZE), per_b3),
                       pl.BlockSpec((None, T, PAGE_SIZE), per_b3)],
            scratch_shapes=[pltpu.VMEM((n_steps + 1, T, step_w), jnp.int32)]),
        out_shape=[jax.ShapeDtypeStruct((B, T, past), jnp.int32),
                   jax.ShapeDtypeStruct((B, T, PAGE_SIZE), jnp.int32),
                   jax.ShapeDtypeStruct((B, T, PAGE_SIZE), jnp.int32)],
        compiler_params=pltpu.CompilerParams(dimension_semantics=("arbitrary", "arbitrary")),
        name="sample_index",
        interpret=interpret,
    )(page_table, qi_s, w_s, ki_new, *([cache_idx_k] * PAGES_PER_STEP))

    kv_pages = [pl.BlockSpec((None, slot_rows, HEAD_DIM), page_map(r)) for r in range(PAGES_PER_STEP)]
    o = pl.pallas_call(
        functools.partial(_sample_attn_body, nq=T, n_steps=n_steps),
        grid_spec=pltpu.PrefetchScalarGridSpec(
            num_scalar_prefetch=1,
            grid=(B, n_steps),
            in_specs=[pl.BlockSpec((None, KV_HEADS, GROUP * T, HEAD_DIM), lambda b, j, pt: (b, 0, 0, 0)),
                      pl.BlockSpec((None, T, step_w), lambda b, j, pt: (b, 0, j)),
                      pl.BlockSpec((None, T, PAGE_SIZE), per_b3),
                      pl.BlockSpec((None, T, PAGE_SIZE), per_b3),
                      pl.BlockSpec((None, slot_rows, HEAD_DIM), per_b3),
                      pl.BlockSpec((None, slot_rows, HEAD_DIM), per_b3)] + kv_pages + kv_pages,
            out_specs=pl.BlockSpec((None, KV_HEADS, GROUP * T, HEAD_DIM), lambda b, j, pt: (b, 0, 0, 0)),
            scratch_shapes=[pltpu.VMEM((KV_HEADS, GROUP * T, 1), f32),
                            pltpu.VMEM((KV_HEADS, GROUP * T, 1), f32),
                            pltpu.VMEM((KV_HEADS, GROUP * T, HEAD_DIM), f32)]),
        out_shape=jax.ShapeDtypeStruct((B, KV_HEADS, GROUP * T, HEAD_DIM), f32),
        compiler_params=pltpu.CompilerParams(dimension_semantics=("arbitrary", "arbitrary")),
        name="sample_attn",
        interpret=interpret,
    )(page_table, q_s, keys, knew, thr, k_new, v_new, *([ck] * PAGES_PER_STEP), *([cv] * PAGES_PER_STEP))
    o = o.reshape(B, KV_HEADS, GROUP, T, HEAD_DIM)
    return jnp.transpose(o, (0, 3, 1, 2, 4)).reshape(B, T, ATTN_HEADS * HEAD_DIM)


GDN_ROWS = 512
_TN = (((0,), (0,)), ((), ()))


def _mm(a, b):
    return jnp.dot(a.astype(jnp.bfloat16), b.astype(jnp.bfloat16), preferred_element_type=jnp.float32)


def _split2(a):
    hi = a.astype(jnp.bfloat16)
    lo = (a - hi.astype(jnp.float32)).astype(jnp.bfloat16)
    return hi, lo


def _mm3(a, b):
    f32 = jnp.float32
    a1, a2 = _split2(a)
    b1, b2 = _split2(b)
    return (jnp.dot(a1, b1, preferred_element_type=f32) + jnp.dot(a1, b2, preferred_element_type=f32)
            + jnp.dot(a2, b1, preferred_element_type=f32))


def _split3(a):
    f32 = jnp.float32
    p1 = a.astype(jnp.bfloat16)
    r = a - p1.astype(f32)
    p2 = r.astype(jnp.bfloat16)
    p3 = (r - p2.astype(f32)).astype(jnp.bfloat16)
    return p1, p2, p3


def _sigmoid(x):
    return 1.0 / (1.0 + jnp.exp(-x))


def _softplus(x):
    return jnp.maximum(x, 0.0) + jnp.log1p(jnp.exp(-jnp.abs(x)))


def _gdn_body(x_ref, araw_ref, braw_ref, arawT_ref, gate_ref, buf_ref, s0_ref, cw_ref,
              alog_ref, dtb_ref, alogT_ref, dtbT_ref, nw_ref,
              ob_ref, sout_ref, xp_scr, y_scr, s_scr, *, rb, t_valid, t_pad):
    f32, bf16 = jnp.float32, jnp.bfloat16
    C = GDN_CHUNK
    j = pl.program_id(1)

    @pl.when(j == 0)
    def _():
        xp_scr[5:8, :] = buf_ref[...]
        s_scr[...] = s0_ref[...]

    xp_scr[8:8 + rb, :] = x_ref[...]
    y = xp_scr[5:5 + rb, :] * cw_ref[0:1, :]
    for t in range(1, CONV_W):
        y = y + xp_scr[5 + t:5 + t + rb, :] * cw_ref[t:t + 1, :]
    y_scr[...] = y * _sigmoid(y)
    xp_scr[5:8, :] = xp_scr[rb + 5:rb + 8, :]

    ri = lax.broadcasted_iota(jnp.int32, (C, C), 0)
    ci = lax.broadcasted_iota(jnp.int32, (C, C), 1)
    tri_incl = jnp.where(ri >= ci, 1.0, 0.0).astype(bf16)
    tri_inclT = jnp.where(ci >= ri, 1.0, 0.0).astype(bf16)
    causal = ri >= ci
    strict = ri > ci
    eye = jnp.where(ri == ci, 1.0, 0.0)
    neg_a = -jnp.exp(alog_ref[...])
    neg_aT = -jnp.exp(alogT_ref[...])
    q_scale = GDN_DK ** -0.5

    def chunk(c, carry):
        r0 = pl.multiple_of(c * C, C)
        g_c = neg_a * _softplus(araw_ref[pl.ds(r0, C), :] + dtb_ref[...])
        beta_c = _sigmoid(braw_ref[pl.ds(r0, C), :])
        g_r = neg_aT * _softplus(arawT_ref[c] + dtbT_ref[...])
        if t_valid < t_pad:
            base = j * rb + r0
            row_ok = (base + lax.broadcasted_iota(jnp.int32, (C, GDN_HEADS), 0)) < t_valid
            col_ok = (base + lax.broadcasted_iota(jnp.int32, (GDN_HEADS, C), 1)) < t_valid
            g_c = jnp.where(row_ok, g_c, 0.0)
            beta_c = jnp.where(row_ok, beta_c, 0.0)
            g_r = jnp.where(col_ok, g_r, 0.0)
        gc_c = sum(jnp.dot(tri_incl, p, preferred_element_type=f32) for p in _split3(g_c))
        gc_r = sum(jnp.dot(p, tri_inclT, preferred_element_type=f32) for p in _split3(g_r))
        H = range(GDN_HEADS)
        q, k, v, gcol, glast, beta, decay, eg = [], [], [], [], [], [], [], []
        for h in H:
            lo = h * GDN_DK
            qh = y_scr[pl.ds(r0, C), lo:lo + GDN_DK]
            kh = y_scr[pl.ds(r0, C), GDN_HEADS * GDN_DK + lo:GDN_HEADS * GDN_DK + lo + GDN_DK]
            v.append(y_scr[pl.ds(r0, C),
                           2 * GDN_HEADS * GDN_DK + h * GDN_DV:2 * GDN_HEADS * GDN_DK + (h + 1) * GDN_DV])
            q.append(qh * lax.rsqrt(jnp.sum(qh * qh, axis=-1, keepdims=True) + NORM_EPS) * q_scale)
            k.append(kh * lax.rsqrt(jnp.sum(kh * kh, axis=-1, keepdims=True) + NORM_EPS))
            gcol.append(gc_c[:, h:h + 1])
            glast.append(gc_c[C - 1:C, h:h + 1])
            beta.append(beta_c[:, h:h + 1])
            decay.append(jnp.exp(jnp.where(causal, gcol[h] - gc_r[h:h + 1, :], -jnp.inf)))
            eg.append(jnp.exp(gcol[h]))
        kb = [k[h] * beta[h] for h in H]
        kk = [lax.dot_general(kb[h].astype(bf16), k[h].astype(bf16), _NT, preferred_element_type=f32) for h in H]
        qk = [lax.dot_general(q[h].astype(bf16), k[h].astype(bf16), _NT, preferred_element_type=f32) * decay[h]
              for h in H]
        pw = [jnp.where(strict, -(kk[h] * decay[h]), 0.0) for h in H]
        inv = [eye + pw[h] for h in H]
        for _ in range(5):
            pw = [_mm3(pw[h], pw[h]) for h in H]
            inv = [inv[h] + _mm3(inv[h], pw[h]) for h in H]
        sol = [_mm3(inv[h], jnp.concatenate([v[h] * beta[h], kb[h] * eg[h]], axis=1)) for h in H]
        s_old = [s_scr[h] for h in H]
        v_new = [sol[h][:, :GDN_DV] - _mm(sol[h][:, GDN_DV:], s_old[h]) for h in H]
        o = [_mm(q[h] * eg[h], s_old[h]) + _mm(qk[h], v_new[h]) for h in H]
        for h in H:
            kd = k[h] * jnp.exp(glast[h] - gcol[h])
            s_scr[h] = s_old[h] * jnp.exp(glast[h]) + lax.dot_general(
                kd.astype(bf16), v_new[h].astype(bf16), _TN, preferred_element_type=f32)
        for h in H:
            on = o[h] * lax.rsqrt(jnp.mean(o[h] * o[h], axis=-1, keepdims=True) + NORM_EPS) * nw_ref[...]
            gt = gate_ref[pl.ds(r0, C), h * GDN_DV:(h + 1) * GDN_DV]
            ob_ref[pl.ds(r0, C), h * GDN_DV:(h + 1) * GDN_DV] = on * (gt * _sigmoid(gt))
        return carry

    lax.fori_loop(0, rb // C, chunk, 0)

    @pl.when(j == pl.num_programs(1) - 1)
    def _():
        sout_ref[...] = s_scr[...]


def _gdn_pallas(qkv, b_raw, a_raw, gate, conv_buf, S0, conv_w, a_log, dt_bias, gdn_norm_w,
                qkv_blk=0, gate_blk=0, interpret=False):
    B, T, _ = qkv.shape
    assert T >= CONV_W - 1
    f32 = jnp.float32
    C = GDN_CHUNK
    t_pad = -(-T // C) * C
    rb = min(GDN_ROWS, t_pad)
    assert t_pad % rb == 0
    pad = lambda a: jnp.pad(a, ((0, 0), (0, t_pad - T), (0, 0)))
    x, a_p, b_p = pad(qkv), pad(a_raw), pad(b_raw)
    gate_p = x if gate is qkv else pad(gate)
    a_t = jnp.transpose(a_p.reshape(B, t_pad // C, C, GDN_HEADS), (0, 1, 3, 2))
    hd = GDN_HEADS * GDN_DV
    row_blk = lambda b, j: (b, j, 0)
    fix2 = lambda b, j: (0, 0)
    ob, s_out = pl.pallas_call(
        functools.partial(_gdn_body, rb=rb, t_valid=T, t_pad=t_pad),
        grid=(B, t_pad // rb),
        in_specs=[pl.BlockSpec((None, rb, CONV_DIM), lambda b, j: (b, j, qkv_blk)),
                  pl.BlockSpec((None, rb, GDN_HEADS), row_blk),
                  pl.BlockSpec((None, rb, GDN_HEADS), row_blk),
                  pl.BlockSpec((None, rb // C, GDN_HEADS, C), lambda b, j: (b, j, 0, 0)),
                  pl.BlockSpec((None, rb, hd), lambda b, j: (b, j, gate_blk)),
                  pl.BlockSpec((None, CONV_W - 1, CONV_DIM), lambda b, j: (b, 0, 0)),
                  pl.BlockSpec((None, GDN_HEADS, GDN_DK, GDN_DV), lambda b, j: (b, 0, 0, 0)),
                  pl.BlockSpec((CONV_W, CONV_DIM), fix2),
                  pl.BlockSpec((1, GDN_HEADS), fix2),
                  pl.BlockSpec((1, GDN_HEADS), fix2),
                  pl.BlockSpec((GDN_HEADS, 1), fix2),
                  pl.BlockSpec((GDN_HEADS, 1), fix2),
                  pl.BlockSpec((1, GDN_DV), fix2)],
        out_specs=[pl.BlockSpec((None, rb, hd), row_blk),
                   pl.BlockSpec((None, GDN_HEADS, GDN_DK, GDN_DV), lambda b, j: (b, 0, 0, 0))],
        out_shape=[jax.ShapeDtypeStruct((B, t_pad, hd), f32),
                   jax.ShapeDtypeStruct((B, GDN_HEADS, GDN_DK, GDN_DV), f32)],
        scratch_shapes=[pltpu.VMEM((rb + 8, CONV_DIM), f32),
                        pltpu.VMEM((rb, CONV_DIM), f32),
                        pltpu.VMEM((GDN_HEADS, GDN_DK, GDN_DV), f32)],
        compiler_params=pltpu.CompilerParams(
            dimension_semantics=("arbitrary", "arbitrary"),
            vmem_limit_bytes=56 * 1024 * 1024),
        name="gated_deltanet",
        interpret=interpret,
    )(x, a_p, b_p, a_t, gate_p, conv_buf, S0, conv_w,
      a_log.reshape(1, GDN_HEADS), dt_bias.reshape(1, GDN_HEADS),
      a_log.reshape(GDN_HEADS, 1), dt_bias.reshape(GDN_HEADS, 1), gdn_norm_w.reshape(1, GDN_DV))
    return ob[:, :T], s_out, qkv[:, T - (CONV_W - 1):, qkv_blk * CONV_DIM:(qkv_blk + 1) * CONV_DIM]


_SQRT_HALF = 0.7071067811865476


def _top_rows(x, k):
    R, n = x.shape
    ri = lax.broadcasted_iota(jnp.int32, (R, n), 0).astype(jnp.float32)
    ki = lax.broadcasted_iota(jnp.int32, (k, n), 0)

    def body(r, c):
        x, out, idx = c
        m = x.max(axis=0, keepdims=True)
        first = jnp.min(jnp.where(x == m, ri, float(R)), axis=0, keepdims=True)
        x = jnp.where(ri == first, -jnp.inf, x)
        return x, jnp.where(ki == r, m, out), jnp.where(ki == r, first, idx)

    zk = jnp.zeros((k, n), jnp.float32)
    rest, out, idx = lax.fori_loop(0, k, body, (x, zk, zk))
    return out, idx, rest.max(axis=0, keepdims=True)


def _peer_front_body(x_ref, oa_ref, ob_ref, wo_ref, g_ref, wq_ref, keys_ref,
                     h_ref, xn_ref, s1_ref, s2_ref, e2_ref, aux_ref, eidx_ref, gsm_ref, flag_ref, top_scr):
    f32, bf16 = jnp.float32, jnp.bfloat16
    half_w = ATTN_HEADS * HEAD_DIM
    h = (x_ref[...]
         + jnp.dot(oa_ref[...].astype(bf16), wo_ref[:half_w, :], preferred_element_type=f32)
         + jnp.dot(ob_ref[...].astype(bf16), wo_ref[half_w:, :], preferred_element_type=f32))
    h_ref[...] = h
    xn = (h * lax.rsqrt(jnp.mean(h * h, axis=-1, keepdims=True) + NORM_EPS) * g_ref[...]).astype(bf16)
    xn_ref[...] = xn
    qh = jnp.dot(xn, wq_ref[...], preferred_element_type=f32).astype(bf16)
    tq = qh.shape[0]
    hq = PEER_QDIM // 2
    K = PEER_TOPK
    for hh in range(PEER_HEADS):
        tops = []
        for half in range(2):
            col = (hh * 2 + half) * hq
            sT = lax.dot_general(keys_ref[half, hh], qh[:, col:col + hq], _NT,
                                 preferred_element_type=f32)
            (s1_ref if half == 0 else s2_ref)[hh] = sT
            tops.append(_top_rows(sT, K))
        (a16, ia, a_next), (b16, ib, b_next) = tops
        cand = jnp.concatenate(
            [a16[r:r + 1, :] + b16[0:8, :] for r in range(8)]
            + [a16[0:1, :] + b16[8:16, :], a16[8:16, :] + b16[0:1, :]], axis=0)
        tau = _top_rows(cand, K)[0][K - 1:K, :]
        top_sum = a16[0:1, :] + b16[0:1, :]
        keep = cand >= tau
        z = jnp.sum(jnp.where(keep, jnp.exp(cand - top_sum), 0.0), axis=0, keepdims=True)
        e2_ref[hh] = jnp.exp(s2_ref[hh] - b16[0:1, :]) / z
        aux_ref[hh] = jnp.concatenate([tau, a16[0:1, :], jnp.zeros((6, tq), f32)], axis=0)

        tied = (jnp.sum(jnp.where(keep, 1.0, 0.0), axis=0, keepdims=True) > K)
        tied = tied | ((a_next == a16[K - 1:K, :]) & (a16[K - 1:K, :] + b16[0:1, :] >= tau))
        tied = tied | ((b_next == b16[K - 1:K, :]) & (a16[0:1, :] + b16[K - 1:K, :] >= tau))
        head_flag = jnp.max(jnp.where(tied, 1.0, 0.0), axis=1, keepdims=True)
        flag_ref[0, hh:hh + 1, :] = jnp.broadcast_to(head_flag, (1, LANES))
        top_scr[hh] = jnp.concatenate([a16, ia, b16, ib], axis=0)

    any_tied = jnp.max(flag_ref[0]) > 0.0

    @pl.when(jnp.logical_not(any_tied))
    def _():
        eidx_ref[...] = jnp.zeros(eidx_ref.shape, jnp.int32)
        gsm_ref[...] = jnp.zeros(gsm_ref.shape, f32)

    @pl.when(any_tied)
    def _():
        for hh in range(PEER_HEADS):
            flagged = jnp.max(flag_ref[0, hh:hh + 1, :]) > 0.0

            @pl.when(jnp.logical_not(flagged))
            def _():
                eidx_ref[hh] = jnp.zeros((K, tq), jnp.int32)
                gsm_ref[hh] = jnp.zeros((K, tq), f32)

            @pl.when(flagged)
            def _():
                a16, ia = top_scr[hh, 0:K, :], top_scr[hh, K:2 * K, :]
                b16, ib = top_scr[hh, 2 * K:3 * K, :], top_scr[hh, 3 * K:4 * K, :]
                full = jnp.concatenate([a16[r:r + 1, :] + b16 for r in range(K)], axis=0)
                sv, flat, _ = _top_rows(full, K)
                ra = jnp.floor(flat * (1.0 / K))
                rb = flat - ra * K
                i1 = jnp.zeros((K, tq), f32)
                i2 = jnp.zeros((K, tq), f32)
                for r in range(K):
                    i1 = jnp.where(ra == r, ia[r:r + 1, :], i1)
                    i2 = jnp.where(rb == r, ib[r:r + 1, :], i2)
                eidx_ref[hh] = (i1 * PEER_NKEYS + i2).astype(jnp.int32)
                ex = jnp.exp(sv - sv[0:1, :])
                gsm_ref[hh] = ex / jnp.sum(ex, axis=0, keepdims=True)


def _peer_front(x, oa, ob, wo, g, wq, keys, tq, interpret=False):
    n = x.shape[0]
    half_w = ATTN_HEADS * HEAD_DIM
    tok = lambda i: (i, 0)
    fix2 = lambda i: (0, 0)
    colT = lambda i: (0, 0, i)
    f32 = jnp.float32
    plane = jax.ShapeDtypeStruct((PEER_HEADS, PEER_NKEYS, n), f32)
    plane_spec = pl.BlockSpec((PEER_HEADS, PEER_NKEYS, tq), colT)
    return pl.pallas_call(
        _peer_front_body,
        grid=(n // tq,),
        in_specs=[pl.BlockSpec((tq, D_MODEL), tok),
                  pl.BlockSpec((tq, half_w), tok),
                  pl.BlockSpec((tq, half_w), tok),
                  pl.BlockSpec((D_MODEL, D_MODEL), fix2),
                  pl.BlockSpec((1, D_MODEL), fix2),
                  pl.BlockSpec((D_MODEL, PEER_HEADS * PEER_QDIM), fix2),
                  pl.BlockSpec((2, PEER_HEADS, PEER_NKEYS, PEER_QDIM // 2), lambda i: (0, 0, 0, 0))],
        out_specs=[pl.BlockSpec((tq, D_MODEL), tok),
                   pl.BlockSpec((tq, D_MODEL), tok),
                   plane_spec, plane_spec, plane_spec,
                   pl.BlockSpec((PEER_HEADS, 8, tq), colT),
                   pl.BlockSpec((PEER_HEADS, PEER_TOPK, tq), colT),
                   pl.BlockSpec((PEER_HEADS, PEER_TOPK, tq), colT),
                   pl.BlockSpec((1, 8, LANES), lambda i: (i, 0, 0))],
        out_shape=[jax.ShapeDtypeStruct((n, D_MODEL), f32),
                   jax.ShapeDtypeStruct((n, D_MODEL), jnp.bfloat16),
                   plane, plane, plane,
                   jax.ShapeDtypeStruct((PEER_HEADS, 8, n), f32),
                   jax.ShapeDtypeStruct((PEER_HEADS, PEER_TOPK, n), jnp.int32),
                   jax.ShapeDtypeStruct((PEER_HEADS, PEER_TOPK, n), f32),
                   jax.ShapeDtypeStruct((n // tq, 8, LANES), f32)],
        scratch_shapes=[pltpu.VMEM((PEER_HEADS, 4 * PEER_TOPK, tq), f32)],
        compiler_params=pltpu.CompilerParams(
            dimension_semantics=("arbitrary",),
            vmem_limit_bytes=56 * 1024 * 1024),
        name="peer_front",
        interpret=interpret,
    )(x, oa, ob, wo, g.reshape(1, D_MODEL), wq, keys)


def _peer_dense_body(xn_ref, u_ref, vT_ref, s1_ref, s2_ref, e2_ref, aux_ref, eidx_ref, gsm_ref, flag_ref,
                     yT_ref, g_scr, *, eblk):
    f32 = jnp.float32
    eb = pl.program_id(1)
    tq = xn_ref.shape[0]
    sub = eblk // PEER_NKEYS

    @pl.when(eb == 0)
    def _():
        yT_ref[...] = jnp.zeros(yT_ref.shape, f32)

    def finish(gate_of):
        a = lax.dot_general(u_ref[...], xn_ref[...], _NT, preferred_element_type=f32)
        act = 0.5 * a * (1.0 + lax.erf(a * _SQRT_HALF))
        hT = jnp.concatenate(
            [(gate_of(r) * act[r * PEER_NKEYS:(r + 1) * PEER_NKEYS, :]).astype(jnp.bfloat16) for r in range(sub)],
            axis=0)
        yT_ref[...] += jnp.dot(vT_ref[...], hT, preferred_element_type=f32)

    exact_lists = jnp.max(flag_ref[...]) > 0.0

    def head_gate(r, hh):
        s1row = s1_ref[hh, pl.ds(eb * sub + r, 1), :]
        tau = aux_ref[hh, 0:1, :]
        e1row = jnp.exp(s1row - aux_ref[hh, 1:2, :])
        return jnp.where(s1row + s2_ref[hh] >= tau, e1row * e2_ref[hh], 0.0)

    @pl.when(jnp.logical_not(exact_lists))
    def _():
        finish(lambda r: sum(head_gate(r, hh) for hh in range(PEER_HEADS)))

    @pl.when(exact_lists)
    def _():
        n_flags = flag_ref.shape[0]
        row = lax.broadcasted_iota(jnp.int32, (PEER_NKEYS, tq), 0)
        for r in range(sub):
            base = (eb * sub + r) * PEER_NKEYS
            gate = jnp.zeros((PEER_NKEYS, tq), f32)
            for hh in range(PEER_HEADS):
                def with_list(g, hh=hh):
                    use_list = jnp.concatenate(
                        [jnp.broadcast_to(flag_ref[f, hh:hh + 1, 0:1], (1, tq // n_flags)) for f in range(n_flags)],
                        axis=1) > 0.0
                    ids = eidx_ref[hh] - base
                    wts = gsm_ref[hh]
                    listed = jnp.zeros((PEER_NKEYS, tq), f32)
                    for k in range(PEER_TOPK):
                        listed = listed + jnp.where(row == ids[k:k + 1, :], wts[k:k + 1, :], 0.0)
                    return g + jnp.where(use_list, listed, head_gate(r, hh))

                gate = lax.cond(jnp.max(flag_ref[:, hh:hh + 1, :]) > 0.0, with_list,
                                lambda g, hh=hh: g + head_gate(r, hh), gate)
            g_scr[r * PEER_NKEYS:(r + 1) * PEER_NKEYS, :] = gate
        finish(lambda r: g_scr[r * PEER_NKEYS:(r + 1) * PEER_NKEYS, :])


def _peer_dense(xn, u, vT, s1, s2, e2, aux, eidx, gsm, flag, tq, eblk, interpret=False):
    n = xn.shape[0]
    ne = u.shape[0]
    flags_per_tile = flag.shape[0] * tq // n
    colT = lambda i, e: (0, 0, i)
    plane_spec = pl.BlockSpec((PEER_HEADS, PEER_NKEYS, tq), colT)
    list_spec = pl.BlockSpec((PEER_HEADS, PEER_TOPK, tq), colT)
    return pl.pallas_call(
        functools.partial(_peer_dense_body, eblk=eblk),
        grid=(n // tq, ne // eblk),
        in_specs=[pl.BlockSpec((tq, D_MODEL), lambda i, e: (i, 0)),
                  pl.BlockSpec((eblk, D_MODEL), lambda i, e: (e, 0)),
                  pl.BlockSpec((D_MODEL, eblk), lambda i, e: (0, e)),
                  plane_spec, plane_spec, plane_spec,
                  pl.BlockSpec((PEER_HEADS, 8, tq), colT),
                  list_spec, list_spec,
                  pl.BlockSpec((flags_per_tile, 8, LANES), lambda i, e: (i, 0, 0))],
        out_specs=pl.BlockSpec((D_MODEL, tq), lambda i, e: (0, i)),
        out_shape=jax.ShapeDtypeStruct((D_MODEL, n), jnp.float32),
        scratch_shapes=[pltpu.VMEM((eblk, tq), jnp.float32)],
        compiler_params=pltpu.CompilerParams(
            dimension_semantics=("arbitrary", "arbitrary"),
            vmem_limit_bytes=56 * 1024 * 1024),
        name="peer_dense",
        interpret=interpret,
    )(xn, u, vT, s1, s2, e2, aux, eidx, gsm, flag)


def _layer_out_pallas(x, oa, ob, wo_b, ffn_norm_w, wq_b, keys_b, u_b, vT_b, interpret=False):
    n = x.shape[0]
    tq1 = min(256, n)
    tq2 = 512 if n % 512 == 0 else min(256, n)
    h, xn, s1, s2, e2, aux, eidx, gsm, flag = _peer_front(x, oa, ob, wo_b, ffn_norm_w, wq_b, keys_b, tq1, interpret)
    yT = _peer_dense(xn, u_b, vT_b, s1, s2, e2, aux, eidx, gsm, flag, tq2, 512, interpret)
    return h + yT.T


def kernel(x_prompt, x_sample, cache_k, cache_v, cache_idx_k, state_ssm, state_conv, page_table,
           attn_norm_w, w_in, q_norm_w, k_norm_w, idx_k_norm_w, conv_w, a_log, dt_bias, gdn_norm_w,
           w_out, ffn_norm_w, peer_wq, peer_keys, peer_u, peer_v):
    l = 0
    proj_w = (attn_norm_w[l], _permute_w_in(w_in[l]), q_norm_w[l], k_norm_w[l], idx_k_norm_w[l])
    gdn_cols = dict(qkv_blk=_DST["qkv"] // CONV_DIM, gate_blk=_DST["gate"] // (GDN_HEADS * GDN_DV))
    gdn_w = (conv_w[l], a_log[l], dt_bias[l], gdn_norm_w[l])
    bf16 = jnp.bfloat16
    out_w = (w_out[l].astype(bf16), ffn_norm_w[l], peer_wq[l].astype(bf16), peer_keys[l].astype(bf16),
             peer_u[l].astype(bf16), peer_v[l].astype(bf16).T)

    hp, hs = x_prompt, x_sample
    (qa, ka, va, qi, ki, wi), (z, b_raw, a_raw) = _in_projection(hp, *proj_w)
    oa = _dsa_prompt_pallas(qa[0], ka[0], va[0], qi[0], ki[0], wi[0])[None]
    Bp = hp.shape[0]
    buf0 = jnp.zeros((Bp, CONV_W - 1, CONV_DIM), hp.dtype)
    S0 = jnp.zeros((Bp, GDN_HEADS, GDN_DK, GDN_DV), jnp.float32)
    ob, S_p, buf_p = _gdn_pallas(z, b_raw, a_raw, z, buf0, S0, *gdn_w, **gdn_cols)
    half_w = ATTN_HEADS * HEAD_DIM
    hp = _layer_out_pallas(hp[0], oa[0], ob[0], *out_w)[None]
    kp, vp, ip = ka, va, ki

    (qa, ka, va, qi, ki, wi), (z, b_raw, a_raw) = _in_projection(hs, *proj_w)
    oa = _dsa_sample_pallas(qa, ka, va, qi, ki, wi, cache_k[l], cache_v[l], cache_idx_k[l], page_table)
    ob, S_s, buf_s = _gdn_pallas(z, b_raw, a_raw, z, state_conv[l], state_ssm[l].astype(jnp.float32),
                                 *gdn_w, **gdn_cols)
    ns = hs.shape[0] * hs.shape[1]
    hs = _layer_out_pallas(hs.reshape(ns, D_MODEL), oa.reshape(ns, half_w), ob.reshape(ns, half_w),
                           *out_w).reshape(hs.shape)

    return (hp, hs, kp[None], vp[None], ip[None], S_p[None], buf_p[None],
            ka[None], va[None], ki[None], S_s[None], buf_s[None])
```

```python
import functools
import math

import jax
import jax.numpy as jnp
from jax import lax
from jax.experimental import pallas as pl
from jax.experimental.pallas import tpu as pltpu

D_MODEL = 2048
PAGE_SIZE = 128
HEAD_DIM = 128
ATTN_HEADS = 8
KV_HEADS = 2
GROUP = ATTN_HEADS // KV_HEADS
IDX_HEADS = 16
IDX_DIM = 64
INDEX_TOPK = 256
IDX_SCALE = (IDX_HEADS * IDX_DIM) ** -0.5
GDN_DK = 128
GDN_DV = 128
GDN_HEADS = 8
CONV_W = 4
CONV_DIM = GDN_HEADS * (2 * GDN_DK + GDN_DV)
GDN_CHUNK = 64
IN_SIZES = (ATTN_HEADS * HEAD_DIM, KV_HEADS * HEAD_DIM, KV_HEADS * HEAD_DIM,
            IDX_HEADS * IDX_DIM, IDX_DIM, IDX_HEADS,
            CONV_DIM, GDN_HEADS, GDN_HEADS, GDN_HEADS * GDN_DV)
IN_COLS = sum(IN_SIZES)
PEER_HEADS = 8
PEER_NKEYS = 128
PEER_QDIM = 256
PEER_TOPK = 16
NORM_EPS = 1e-6

LANES = 128


def _rmsnorm(x, w):
    xf = x.astype(jnp.float32)
    y = xf * lax.rsqrt(jnp.mean(xf * xf, axis=-1, keepdims=True) + NORM_EPS)
    return (y * w.astype(jnp.float32)).astype(x.dtype)


def _norm_matmul_body(x_ref, g_ref, w_ref, o_ref, xn_ref):
    @pl.when(pl.program_id(1) == 0)
    def _():
        x = x_ref[...]
        r = lax.rsqrt(jnp.mean(x * x, axis=-1, keepdims=True) + NORM_EPS)
        xn_ref[...] = (x * r * g_ref[...]).astype(jnp.bfloat16)

    o_ref[...] = jnp.dot(xn_ref[...], w_ref[...], preferred_element_type=jnp.float32)


def _norm_matmul(x, g, w, tm, tn):
    m, k = x.shape
    n = w.shape[1]
    return pl.pallas_call(
        _norm_matmul_body,
        grid=(m // tm, n // tn),
        in_specs=[pl.BlockSpec((tm, k), lambda i, j: (i, 0)),
                  pl.BlockSpec((1, k), lambda i, j: (0, 0)),
                  pl.BlockSpec((k, tn), lambda i, j: (0, j))],
        out_specs=pl.BlockSpec((tm, tn), lambda i, j: (i, j)),
        out_shape=jax.ShapeDtypeStruct((m, n), jnp.float32),
        scratch_shapes=[pltpu.VMEM((tm, k), jnp.bfloat16)],
        compiler_params=pltpu.CompilerParams(
            dimension_semantics=("arbitrary", "arbitrary"),
            vmem_limit_bytes=48 * 1024 * 1024),
        name="norm_matmul",
    )(x, g.reshape(1, k), w)


PROJ_TILE = 1024
_SRC = dict(zip(("qa", "ka", "va", "qi", "ki", "wi", "qkv", "b", "a", "gate"),
                [(sum(IN_SIZES[:i]), IN_SIZES[i]) for i in range(len(IN_SIZES))]))
_DST_ORDER = (("qkv",), ("qa",), ("qi",), ("gate",), ("ka",), ("va",), ("ki",), ("wi", "b", "a"))


def _proj_layout():
    dst, off = {}, 0
    for group in _DST_ORDER:
        for name in group:
            dst[name] = off
            off += _SRC[name][1]
        off = -(-off // LANES) * LANES
    return dst, -(-off // PROJ_TILE) * PROJ_TILE


_DST, PROJ_COLS = _proj_layout()
assert _DST["qkv"] == 0 and _DST["gate"] % (GDN_HEADS * GDN_DV) == 0


def _permute_w_in(w_in):
    pieces, off = [], 0
    for group in _DST_ORDER:
        for name in group:
            if _DST[name] > off:
                pieces.append(jnp.zeros((D_MODEL, _DST[name] - off), w_in.dtype))
            s0, n = _SRC[name]
            pieces.append(w_in[:, s0:s0 + n])
            off = _DST[name] + n
    pieces.append(jnp.zeros((D_MODEL, PROJ_COLS - off), w_in.dtype))
    return jnp.concatenate(pieces, axis=1).astype(jnp.bfloat16)


def _in_projection(x, attn_norm_w, w_in_p, q_norm_w, k_norm_w, idx_k_norm_w):
    B, T, _ = x.shape
    m = B * T
    tm = 512 if m % 512 == 0 else m
    z = _norm_matmul(x.reshape(m, D_MODEL), attn_norm_w, w_in_p, tm, PROJ_TILE).reshape(B, T, PROJ_COLS)
    col = lambda name: z[:, :, _DST[name]:_DST[name] + _SRC[name][1]]
    qa = _rmsnorm(col("qa").reshape(B, T, ATTN_HEADS, HEAD_DIM), q_norm_w)
    ka = _rmsnorm(col("ka").reshape(B, T, KV_HEADS, HEAD_DIM), k_norm_w)
    va = col("va").reshape(B, T, KV_HEADS, HEAD_DIM)
    qi = col("qi").reshape(B, T, IDX_HEADS, IDX_DIM)
    ki = _rmsnorm(col("ki"), idx_k_norm_w)
    return (qa, ka, va, qi, ki, col("wi")), (z, col("b"), col("a"))


_INT_MIN = -2 ** 31
_NEG_BIG = -1e30
_NT = (((1,), (1,)), ((), ()))


def _dsa_prompt_body(qi_ref, wT_ref, q_ref, ki_ref, k_ref, vT_ref, o_ref,
                     keys_ref, m_ref, l_ref, acc_ref, *, ktop, tq):
    f32 = jnp.float32
    i = pl.program_id(0)
    nkb = i + 1
    col_t = i * tq + lax.broadcasted_iota(jnp.int32, (tq, tq), 1)
    row_s = lax.broadcasted_iota(jnp.int32, (tq, tq), 0)

    def score_blk(kb, carry):
        kib = ki_ref[kb]
        acc = jnp.zeros((tq, tq), f32)
        for h in range(IDX_HEADS):
            s = lax.dot_general(kib, qi_ref[h], _NT, preferred_element_type=f32)
            acc = acc + jnp.maximum(s, 0.0) * wT_ref[h:h + 1, :]
        sc = acc * IDX_SCALE
        bits = lax.bitcast_convert_type(sc, jnp.int32)
        key = bits ^ (lax.shift_right_arithmetic(bits, 31) & 0x7FFFFFFF)
        valid = (kb * tq + row_s) <= col_t
        keys_ref[kb] = jnp.where(valid, key, _INT_MIN)
        return carry

    lax.fori_loop(0, nkb, score_blk, 0)

    def bisect(it, ans_u):
        cand_u = ans_u | lax.shift_left(jnp.int32(1), 31 - it)
        cand_s = cand_u ^ _INT_MIN

        def count_blk(kb, cnt):
            hit = jnp.where(keys_ref[kb] >= cand_s, 1.0, 0.0)
            return cnt + hit.reshape(tq // 8, 8, tq).sum(axis=0)

        cnt = lax.fori_loop(0, nkb, count_blk, jnp.zeros((8, tq), f32))
        cnt = cnt.sum(axis=0, keepdims=True)
        return jnp.where(cnt >= ktop, cand_u, ans_u)

    ans_u = lax.fori_loop(0, 32, bisect, jnp.zeros((1, tq), jnp.int32))
    thr = jnp.maximum(ans_u ^ _INT_MIN, _INT_MIN + 1)

    def tally(kb, c):
        kk = keys_ref[kb]
        ge = jnp.where(kk >= thr, 1.0, 0.0).reshape(tq // 8, 8, tq).sum(axis=0)
        gt = jnp.where(kk > thr, 1.0, 0.0).reshape(tq // 8, 8, tq).sum(axis=0)
        return c[0] + ge, c[1] + gt

    z8 = jnp.zeros((8, tq), f32)
    n_ge, n_gt = lax.fori_loop(0, nkb, tally, (z8, z8))
    n_ge = n_ge.sum(axis=0, keepdims=True)
    n_gt = n_gt.sum(axis=0, keepdims=True)

    @pl.when(jnp.max(n_ge) > ktop)
    def _():
        need = jnp.where(n_ge >= ktop, ktop - n_gt, float(2 ** 30))
        idx_bits = max(1, (keys_ref.shape[0] * tq - 1).bit_length())

        def bisect_idx(it, pos):
            cand = pos | lax.shift_left(jnp.int32(1), idx_bits - 1 - it)

            def count_blk(kb, cnt):
                hit = (keys_ref[kb] == thr) & ((kb * tq + row_s) < cand)
                return cnt + jnp.where(hit, 1.0, 0.0).reshape(tq // 8, 8, tq).sum(axis=0)

            cnt = lax.fori_loop(0, nkb, count_blk, z8).sum(axis=0, keepdims=True)
            return jnp.where(cnt < need, cand, pos)

        last = lax.fori_loop(0, idx_bits, bisect_idx, jnp.zeros((1, tq), jnp.int32))

        def demote(kb, carry):
            kk = keys_ref[kb]
            keys_ref[kb] = jnp.where((kk == thr) & ((kb * tq + row_s) > last), kk - 1, kk)
            return carry

        lax.fori_loop(0, nkb, demote, 0)

    m_ref[...] = jnp.full(m_ref.shape, _NEG_BIG, f32)
    l_ref[...] = jnp.zeros(l_ref.shape, f32)
    acc_ref[...] = jnp.zeros(acc_ref.shape, f32)
    c2 = HEAD_DIM ** -0.5 * math.log2(math.e)

    def attn_blk(kb, carry):
        sel = keys_ref[kb] >= thr
        kblk = k_ref[kb]
        vT = vT_ref[kb]
        s_all = [lax.dot_general(kblk[:, (h // GROUP) * HEAD_DIM:(h // GROUP + 1) * HEAD_DIM],
                                 q_ref[:, h * HEAD_DIM:(h + 1) * HEAD_DIM], _NT,
                                 preferred_element_type=f32) for h in range(ATTN_HEADS)]
        for h in range(ATTN_HEADS):
            n = h // GROUP
            s = jnp.where(sel, s_all[h], -jnp.inf)
            m_old = m_ref[h]
            m_new = jnp.maximum(m_old, s.max(axis=0, keepdims=True))
            p = jnp.exp2((s - m_new) * c2)
            alpha = jnp.exp2((m_old - m_new) * c2)
            l_ref[h] = alpha * l_ref[h] + p.sum(axis=0, keepdims=True)
            pv = jnp.dot(vT[n * HEAD_DIM:(n + 1) * HEAD_DIM, :], p.astype(jnp.bfloat16),
                         preferred_element_type=f32)
            acc_ref[h] = alpha * acc_ref[h] + pv
            m_ref[h] = m_new
        return carry

    lax.fori_loop(0, nkb, attn_blk, 0)
    for h in range(ATTN_HEADS):
        o_ref[:, h * HEAD_DIM:(h + 1) * HEAD_DIM] = (acc_ref[h] / l_ref[h]).T


def _dsa_prompt_pallas(q, k, v, qi, ki, wi, interpret=False):
    T = q.shape[0]
    tq = min(256, T)
    nb = T // tq
    ktop = min(INDEX_TOPK, T // 4)
    bf16 = jnp.bfloat16
    qh = q.astype(bf16).reshape(T, ATTN_HEADS * HEAD_DIM)
    qih = jnp.transpose(qi.astype(bf16), (1, 0, 2))
    wT = wi.astype(jnp.float32).T
    kib = ki.astype(bf16).reshape(nb, tq, IDX_DIM)
    kb = k.astype(bf16).reshape(nb, tq, KV_HEADS * HEAD_DIM)
    vT = jnp.transpose(v.astype(bf16).reshape(nb, tq, KV_HEADS * HEAD_DIM), (0, 2, 1))
    body = functools.partial(_dsa_prompt_body, ktop=ktop, tq=tq)
    return pl.pallas_call(
        body,
        grid=(nb,),
        in_specs=[pl.BlockSpec((IDX_HEADS, tq, IDX_DIM), lambda i: (0, i, 0)),
                  pl.BlockSpec((IDX_HEADS, tq), lambda i: (0, i)),
                  pl.BlockSpec((tq, ATTN_HEADS * HEAD_DIM), lambda i: (i, 0)),
                  pl.BlockSpec((nb, tq, IDX_DIM), lambda i: (0, 0, 0)),
                  pl.BlockSpec((nb, tq, KV_HEADS * HEAD_DIM), lambda i: (0, 0, 0)),
                  pl.BlockSpec((nb, KV_HEADS * HEAD_DIM, tq), lambda i: (0, 0, 0))],
        out_specs=pl.BlockSpec((tq, ATTN_HEADS * HEAD_DIM), lambda i: (i, 0)),
        out_shape=jax.ShapeDtypeStruct((T, ATTN_HEADS * HEAD_DIM), jnp.float32),
        scratch_shapes=[pltpu.VMEM((nb, tq, tq), jnp.int32),
                        pltpu.VMEM((ATTN_HEADS, 1, tq), jnp.float32),
                        pltpu.VMEM((ATTN_HEADS, 1, tq), jnp.float32),
                        pltpu.VMEM((ATTN_HEADS, HEAD_DIM, tq), jnp.float32)],
        compiler_params=pltpu.CompilerParams(
            dimension_semantics=("arbitrary",),
            vmem_limit_bytes=56 * 1024 * 1024),
        name="dsa_prompt",
        interpret=interpret,
    )(qih, wT, qh, kib, kb, vT)


PAGES_PER_STEP = 16


def _sortable_key(x):
    bits = lax.bitcast_convert_type(x, jnp.int32)
    return bits ^ (lax.shift_right_arithmetic(bits, 31) & 0x7FFFFFFF)


def _sample_index_body(pt_ref, qi_ref, w_ref, kin_ref, *rest, ktop, nq, n_steps):
    del pt_ref
    pages = rest[:PAGES_PER_STEP]
    keys_out, knew_out, thr_out, keys_scr = rest[PAGES_PER_STEP:]
    f32 = jnp.float32
    j = pl.program_id(1)
    step_w = PAGES_PER_STEP * PAGE_SIZE
    past = n_steps * step_w
    qi = qi_ref[...]
    w = w_ref[...]

    def page_keys(page):
        s = lax.dot_general(qi, page.astype(jnp.bfloat16), _NT, preferred_element_type=f32)
        s = jnp.maximum(s, 0.0) * w
        return s.reshape(IDX_HEADS, nq, PAGE_SIZE).sum(axis=0) * IDX_SCALE

    keys_scr[j] = jnp.concatenate([_sortable_key(page_keys(p[...])) for p in pages], axis=1)

    @pl.when(j == n_steps - 1)
    def _():
        kn = _sortable_key(page_keys(kin_ref[...]))
        col = lax.broadcasted_iota(jnp.int32, (nq, PAGE_SIZE), 1)
        row = lax.broadcasted_iota(jnp.int32, (nq, PAGE_SIZE), 0)
        kn = jnp.where(col <= row, kn, _INT_MIN)
        keys_scr[n_steps] = jnp.concatenate(
            [kn, jnp.full((nq, step_w - PAGE_SIZE), _INT_MIN, jnp.int32)], axis=1)
        chunks = [(st, c) for st in range(n_steps + 1) for c in range(PAGES_PER_STEP)]

        def chunk_of(st, c):
            return (keys_scr[st, :, c * PAGE_SIZE:(c + 1) * PAGE_SIZE],
                    st * step_w + c * PAGE_SIZE + col)

        def bisect(it, ans_u):
            cand_u = ans_u | lax.shift_left(jnp.int32(1), 31 - it)
            cand_s = cand_u ^ _INT_MIN
            cnt = jnp.zeros((nq, PAGE_SIZE), f32)
            for st in range(n_steps + 1):
                for c in range(PAGES_PER_STEP):
                    chunk = keys_scr[st, :, c * PAGE_SIZE:(c + 1) * PAGE_SIZE]
                    cnt = cnt + jnp.where(chunk >= cand_s, 1.0, 0.0)
            cnt = cnt.sum(axis=1, keepdims=True)
            return jnp.where(cnt >= ktop, cand_u, ans_u)

        ans_u = lax.fori_loop(0, 32, bisect, jnp.zeros((nq, 1), jnp.int32))
        thr = jnp.maximum(ans_u ^ _INT_MIN, _INT_MIN + 1)
        thr_out[...] = jnp.broadcast_to(thr, (nq, PAGE_SIZE))

        n_ge = jnp.zeros((nq, PAGE_SIZE), f32)
        n_gt = jnp.zeros((nq, PAGE_SIZE), f32)
        for st, c in chunks:
            kk, _ = chunk_of(st, c)
            n_ge = n_ge + jnp.where(kk >= thr, 1.0, 0.0)
            n_gt = n_gt + jnp.where(kk > thr, 1.0, 0.0)
        n_ge = n_ge.sum(axis=1, keepdims=True)
        n_gt = n_gt.sum(axis=1, keepdims=True)
        need = jnp.where(n_ge >= ktop, ktop - n_gt, float(2 ** 30))
        idx_bits = max(1, (past + PAGE_SIZE - 1).bit_length())

        def bisect_idx(it, pos):
            cand = pos | lax.shift_left(jnp.int32(1), idx_bits - 1 - it)
            cnt = jnp.zeros((nq, PAGE_SIZE), f32)
            for st, c in chunks:
                kk, idx = chunk_of(st, c)
                cnt = cnt + jnp.where(kk == thr, jnp.where(idx < cand, 1.0, 0.0), 0.0)
            cnt = cnt.sum(axis=1, keepdims=True)
            return jnp.where(cnt < need, cand, pos)

        last = lax.fori_loop(0, idx_bits, bisect_idx, jnp.zeros((nq, 1), jnp.int32))

        def demoted(st, c):
            kk, idx = chunk_of(st, c)
            return jnp.where(kk == thr, jnp.where(idx > last, kk - 1, kk), kk)

        for st in range(n_steps):
            for c in range(PAGES_PER_STEP):
                lo = st * step_w + c * PAGE_SIZE
                keys_out[:, lo:lo + PAGE_SIZE] = demoted(st, c)
        knew_out[...] = demoted(n_steps, 0)


def _sample_attn_body(pt_ref, q_ref, keys_ref, knew_ref, thr_ref, kn_ref, vn_ref, *rest, nq, n_steps):
    del pt_ref
    kp = rest[:PAGES_PER_STEP]
    vp = rest[PAGES_PER_STEP:2 * PAGES_PER_STEP]
    o_ref, m_ref, l_ref, acc_ref = rest[2 * PAGES_PER_STEP:]
    f32, bf16 = jnp.float32, jnp.bfloat16
    j = pl.program_id(1)
    scale = HEAD_DIM ** -0.5
    thr = thr_ref[...]

    def update(keys_q, thr_q, k_blocks, v_blocks):
        sel = jnp.concatenate([keys_q] * GROUP, axis=0) >= jnp.concatenate([thr_q] * GROUP, axis=0)
        for n in range(KV_HEADS):
            qn = q_ref[n]
            s = jnp.concatenate(
                [lax.dot_general(qn, kb[pl.ds(n, PAGE_SIZE, stride=KV_HEADS), :].astype(bf16), _NT,
                                 preferred_element_type=f32)
                 for kb in k_blocks], axis=1) * scale
            s = jnp.where(sel, s, _NEG_BIG)
            m_old = m_ref[n]
            m_new = jnp.maximum(m_old, s.max(axis=1, keepdims=True))
            p = jnp.where(sel, jnp.exp(s - m_new), 0.0)
            alpha = jnp.exp(m_old - m_new)
            l_ref[n] = alpha * l_ref[n] + p.sum(axis=1, keepdims=True)
            pb = p.astype(bf16)
            pv = jnp.zeros((GROUP * nq, HEAD_DIM), f32)
            for c, vb in enumerate(v_blocks):
                pv = pv + jnp.dot(pb[:, c * PAGE_SIZE:(c + 1) * PAGE_SIZE],
                                  vb[pl.ds(n, PAGE_SIZE, stride=KV_HEADS), :].astype(bf16),
                                  preferred_element_type=f32)
            acc_ref[n] = alpha * acc_ref[n] + pv
            m_ref[n] = m_new

    @pl.when(j == 0)
    def _():
        m_ref[...] = jnp.full(m_ref.shape, _NEG_BIG, f32)
        l_ref[...] = jnp.zeros(l_ref.shape, f32)
        acc_ref[...] = jnp.zeros(acc_ref.shape, f32)
        update(knew_ref[...], thr, [kn_ref], [vn_ref])

    thr_w = jnp.concatenate([thr] * PAGES_PER_STEP, axis=1)
    update(keys_ref[...], thr_w, kp, vp)

    @pl.when(j == n_steps - 1)
    def _():
        for n in range(KV_HEADS):
            o_ref[n] = acc_ref[n] / l_ref[n]


def _dsa_sample_pallas(q, k, v, qi, ki, wi, cache_k, cache_v, cache_idx_k, page_table, interpret=False):
    B, T = q.shape[:2]
    n_pages = page_table.shape[1]
    past = n_pages * PAGE_SIZE
    ktop = min(INDEX_TOPK, (past + T) // 4)
    n_steps = n_pages // PAGES_PER_STEP
    step_w = PAGES_PER_STEP * PAGE_SIZE
    f32, bf16 = jnp.float32, jnp.bfloat16
    kvw = KV_HEADS * HEAD_DIM
    qi_s = jnp.transpose(qi.astype(bf16), (0, 2, 1, 3)).reshape(B, IDX_HEADS * T, IDX_DIM)
    w_s = jnp.transpose(wi.astype(f32), (0, 2, 1)).reshape(B, IDX_HEADS * T, 1)
    pad_rows = lambda a: jnp.pad(a, ((0, 0), (0, PAGE_SIZE - T)) + ((0, 0),) * (a.ndim - 2))
    ki_new = pad_rows(ki.astype(f32))
    slot_rows = PAGE_SIZE * KV_HEADS
    k_new = pad_rows(k).reshape(B, slot_rows, HEAD_DIM)
    v_new = pad_rows(v).reshape(B, slot_rows, HEAD_DIM)
    q_s = jnp.transpose(q.astype(bf16).reshape(B, T, KV_HEADS, GROUP, HEAD_DIM),
                        (0, 2, 3, 1, 4)).reshape(B, KV_HEADS, GROUP * T, HEAD_DIM)
    ck = cache_k.reshape(cache_k.shape[0], slot_rows, HEAD_DIM)
    cv = cache_v.reshape(cache_v.shape[0], slot_rows, HEAD_DIM)

    def page_map(r):
        return lambda b, j, pt: (pt[b, j * PAGES_PER_STEP + r], 0, 0)

    per_b3 = lambda b, j, pt: (b, 0, 0)
    idx_pages = [pl.BlockSpec((None, PAGE_SIZE, IDX_DIM), page_map(r)) for r in range(PAGES_PER_STEP)]
    keys, knew, thr = pl.pallas_call(
        functools.partial(_sample_index_body, ktop=ktop, nq=T, n_steps=n_steps),
        grid_spec=pltpu.PrefetchScalarGridSpec(
            num_scalar_prefetch=1,
            grid=(B, n_steps),
            in_specs=[pl.BlockSpec((None, IDX_HEADS * T, IDX_DIM), per_b3),
                      pl.BlockSpec((None, IDX_HEADS * T, 1), per_b3),
                      pl.BlockSpec((None, PAGE_SIZE, IDX_DIM), per_b3)] + idx_pages,
            out_specs=[pl.BlockSpec((None, T, past), per_b3),
                       pl.BlockSpec((None, T, PAGE_SIZE), per_b3),
                       pl.BlockSpec((None, T, PAGE_SIZE), per_b3)],
            scratch_shapes=[pltpu.VMEM((n_steps + 1, T, step_w), jnp.int32)]),
        out_shape=[jax.ShapeDtypeStruct((B, T, past), jnp.int32),
                   jax.ShapeDtypeStruct((B, T, PAGE_SIZE), jnp.int32),
                   jax.ShapeDtypeStruct((B, T, PAGE_SIZE), jnp.int32)],
        compiler_params=pltpu.CompilerParams(dimension_semantics=("arbitrary", "arbitrary")),
        name="sample_index",
        interpret=interpret,
    )(page_table, qi_s, w_s, ki_new, *([cache_idx_k] * PAGES_PER_STEP))

    kv_pages = [pl.BlockSpec((None, slot_rows, HEAD_DIM), page_map(r)) for r in range(PAGES_PER_STEP)]
    o = pl.pallas_call(
        functools.partial(_sample_attn_body, nq=T, n_steps=n_steps),
        grid_spec=pltpu.PrefetchScalarGridSpec(
            num_scalar_prefetch=1,
            grid=(B, n_steps),
            in_specs=[pl.BlockSpec((None, KV_HEADS, GROUP * T, HEAD_DIM), lambda b, j, pt: (b, 0, 0, 0)),
                      pl.BlockSpec((None, T, step_w), lambda b, j, pt: (b, 0, j)),
                      pl.BlockSpec((None, T, PAGE_SIZE), per_b3),
                      pl.BlockSpec((None, T, PAGE_SIZE), per_b3),
                      pl.BlockSpec((None, slot_rows, HEAD_DIM), per_b3),
                      pl.BlockSpec((None, slot_rows, HEAD_DIM), per_b3)] + kv_pages + kv_pages,
            out_specs=pl.BlockSpec((None, KV_HEADS, GROUP * T, HEAD_DIM), lambda b, j, pt: (b, 0, 0, 0)),
            scratch_shapes=[pltpu.VMEM((KV_HEADS, GROUP * T, 1), f32),
                            pltpu.VMEM((KV_HEADS, GROUP * T, 1), f32),
                            pltpu.VMEM((KV_HEADS, GROUP * T, HEAD_DIM), f32)]),
        out_shape=jax.ShapeDtypeStruct((B, KV_HEADS, GROUP * T, HEAD_DIM), f32),
        compiler_params=pltpu.CompilerParams(dimension_semantics=("arbitrary", "arbitrary")),
        name="sample_attn",
        interpret=interpret,
    )(page_table, q_s, keys, knew, thr, k_new, v_new, *([ck] * PAGES_PER_STEP), *([cv] * PAGES_PER_STEP))
    o = o.reshape(B, KV_HEADS, GROUP, T, HEAD_DIM)
    return jnp.transpose(o, (0, 3, 1, 2, 4)).reshape(B, T, ATTN_HEADS * HEAD_DIM)


GDN_ROWS = 512
_TN = (((0,), (0,)), ((), ()))


def _mm(a, b):
    return jnp.dot(a.astype(jnp.bfloat16), b.astype(jnp.bfloat16), preferred_element_type=jnp.float32)


def _split2(a):
    hi = a.astype(jnp.bfloat16)
    lo = (a - hi.astype(jnp.float32)).astype(jnp.bfloat16)
    return hi, lo


def _mm3(a, b):
    f32 = jnp.float32
    a1, a2 = _split2(a)
    b1, b2 = _split2(b)
    return (jnp.dot(a1, b1, preferred_element_type=f32) + jnp.dot(a1, b2, preferred_element_type=f32)
            + jnp.dot(a2, b1, preferred_element_type=f32))


def _split3(a):
    f32 = jnp.float32
    p1 = a.astype(jnp.bfloat16)
    r = a - p1.astype(f32)
    p2 = r.astype(jnp.bfloat16)
    p3 = (r - p2.astype(f32)).astype(jnp.bfloat16)
    return p1, p2, p3


def _sigmoid(x):
    return 1.0 / (1.0 + jnp.exp(-x))


def _softplus(x):
    return jnp.maximum(x, 0.0) + jnp.log1p(jnp.exp(-jnp.abs(x)))


def _gdn_body(x_ref, araw_ref, braw_ref, arawT_ref, gate_ref, buf_ref, s0_ref, cw_ref,
              alog_ref, dtb_ref, alogT_ref, dtbT_ref, nw_ref,
              ob_ref, sout_ref, xp_scr, y_scr, s_scr, *, rb, t_valid, t_pad):
    f32, bf16 = jnp.float32, jnp.bfloat16
    C = GDN_CHUNK
    j = pl.program_id(1)

    @pl.when(j == 0)
    def _():
        xp_scr[5:8, :] = buf_ref[...]
        s_scr[...] = s0_ref[...]

    xp_scr[8:8 + rb, :] = x_ref[...]
    y = xp_scr[5:5 + rb, :] * cw_ref[0:1, :]
    for t in range(1, CONV_W):
        y = y + xp_scr[5 + t:5 + t + rb, :] * cw_ref[t:t + 1, :]
    y_scr[...] = y * _sigmoid(y)
    xp_scr[5:8, :] = xp_scr[rb + 5:rb + 8, :]

    ri = lax.broadcasted_iota(jnp.int32, (C, C), 0)
    ci = lax.broadcasted_iota(jnp.int32, (C, C), 1)
    tri_incl = jnp.where(ri >= ci, 1.0, 0.0).astype(bf16)
    tri_inclT = jnp.where(ci >= ri, 1.0, 0.0).astype(bf16)
    causal = ri >= ci
    strict = ri > ci
    eye = jnp.where(ri == ci, 1.0, 0.0)
    neg_a = -jnp.exp(alog_ref[...])
    neg_aT = -jnp.exp(alogT_ref[...])
    q_scale = GDN_DK ** -0.5

    def chunk(c, carry):
        r0 = pl.multiple_of(c * C, C)
        g_c = neg_a * _softplus(araw_ref[pl.ds(r0, C), :] + dtb_ref[...])
        beta_c = _sigmoid(braw_ref[pl.ds(r0, C), :])
        g_r = neg_aT * _softplus(arawT_ref[c] + dtbT_ref[...])
        if t_valid < t_pad:
            base = j * rb + r0
            row_ok = (base + lax.broadcasted_iota(jnp.int32, (C, GDN_HEADS), 0)) < t_valid
            col_ok = (base + lax.broadcasted_iota(jnp.int32, (GDN_HEADS, C), 1)) < t_valid
            g_c = jnp.where(row_ok, g_c, 0.0)
            beta_c = jnp.where(row_ok, beta_c, 0.0)
            g_r = jnp.where(col_ok, g_r, 0.0)
        gc_c = sum(jnp.dot(tri_incl, p, preferred_element_type=f32) for p in _split3(g_c))
        gc_r = sum(jnp.dot(p, tri_inclT, preferred_element_type=f32) for p in _split3(g_r))
        H = range(GDN_HEADS)
        q, k, v, gcol, glast, beta, decay, eg = [], [], [], [], [], [], [], []
        for h in H:
            lo = h * GDN_DK
            qh = y_scr[pl.ds(r0, C), lo:lo + GDN_DK]
            kh = y_scr[pl.ds(r0, C), GDN_HEADS * GDN_DK + lo:GDN_HEADS * GDN_DK + lo + GDN_DK]
            v.append(y_scr[pl.ds(r0, C),
                           2 * GDN_HEADS * GDN_DK + h * GDN_DV:2 * GDN_HEADS * GDN_DK + (h + 1) * GDN_DV])
            q.append(qh * lax.rsqrt(jnp.sum(qh * qh, axis=-1, keepdims=True) + NORM_EPS) * q_scale)
            k.append(kh * lax.rsqrt(jnp.sum(kh * kh, axis=-1, keepdims=True) + NORM_EPS))
            gcol.append(gc_c[:, h:h + 1])
            glast.append(gc_c[C - 1:C, h:h + 1])
            beta.append(beta_c[:, h:h + 1])
            decay.append(jnp.exp(jnp.where(causal, gcol[h] - gc_r[h:h + 1, :], -jnp.inf)))
            eg.append(jnp.exp(gcol[h]))
        kb = [k[h] * beta[h] for h in H]
        kk = [lax.dot_general(kb[h].astype(bf16), k[h].astype(bf16), _NT, preferred_element_type=f32) for h in H]
        qk = [lax.dot_general(q[h].astype(bf16), k[h].astype(bf16), _NT, preferred_element_type=f32) * decay[h]
              for h in H]
        pw = [jnp.where(strict, -(kk[h] * decay[h]), 0.0) for h in H]
        inv = [eye + pw[h] for h in H]
        for _ in range(5):
            pw = [_mm3(pw[h], pw[h]) for h in H]
            inv = [inv[h] + _mm3(inv[h], pw[h]) for h in H]
        sol = [_mm3(inv[h], jnp.concatenate([v[h] * beta[h], kb[h] * eg[h]], axis=1)) for h in H]
        s_old = [s_scr[h] for h in H]
        v_new = [sol[h][:, :GDN_DV] - _mm(sol[h][:, GDN_DV:], s_old[h]) for h in H]
        o = [_mm(q[h] * eg[h], s_old[h]) + _mm(qk[h], v_new[h]) for h in H]
        for h in H:
            kd = k[h] * jnp.exp(glast[h] - gcol[h])
            s_scr[h] = s_old[h] * jnp.exp(glast[h]) + lax.dot_general(
                kd.astype(bf16), v_new[h].astype(bf16), _TN, preferred_element_type=f32)
        for h in H:
            on = o[h] * lax.rsqrt(jnp.mean(o[h] * o[h], axis=-1, keepdims=True) + NORM_EPS) * nw_ref[...]
            gt = gate_ref[pl.ds(r0, C), h * GDN_DV:(h + 1) * GDN_DV]
            ob_ref[pl.ds(r0, C), h * GDN_DV:(h + 1) * GDN_DV] = on * (gt * _sigmoid(gt))
        return carry

    lax.fori_loop(0, rb // C, chunk, 0)

    @pl.when(j == pl.num_programs(1) - 1)
    def _():
        sout_ref[...] = s_scr[...]


def _gdn_pallas(qkv, b_raw, a_raw, gate, conv_buf, S0, conv_w, a_log, dt_bias, gdn_norm_w,
                qkv_blk=0, gate_blk=0, interpret=False):
    B, T, _ = qkv.shape
    assert T >= CONV_W - 1
    f32 = jnp.float32
    C = GDN_CHUNK
    t_pad = -(-T // C) * C
    rb = min(GDN_ROWS, t_pad)
    assert t_pad % rb == 0
    pad = lambda a: jnp.pad(a, ((0, 0), (0, t_pad - T), (0, 0)))
    x, a_p, b_p = pad(qkv), pad(a_raw), pad(b_raw)
    gate_p = x if gate is qkv else pad(gate)
    a_t = jnp.transpose(a_p.reshape(B, t_pad // C, C, GDN_HEADS), (0, 1, 3, 2))
    hd = GDN_HEADS * GDN_DV
    row_blk = lambda b, j: (b, j, 0)
    fix2 = lambda b, j: (0, 0)
    ob, s_out = pl.pallas_call(
        functools.partial(_gdn_body, rb=rb, t_valid=T, t_pad=t_pad),
        grid=(B, t_pad // rb),
        in_specs=[pl.BlockSpec((None, rb, CONV_DIM), lambda b, j: (b, j, qkv_blk)),
                  pl.BlockSpec((None, rb, GDN_HEADS), row_blk),
                  pl.BlockSpec((None, rb, GDN_HEADS), row_blk),
                  pl.BlockSpec((None, rb // C, GDN_HEADS, C), lambda b, j: (b, j, 0, 0)),
                  pl.BlockSpec((None, rb, hd), lambda b, j: (b, j, gate_blk)),
                  pl.BlockSpec((None, CONV_W - 1, CONV_DIM), lambda b, j: (b, 0, 0)),
                  pl.BlockSpec((None, GDN_HEADS, GDN_DK, GDN_DV), lambda b, j: (b, 0, 0, 0)),
                  pl.BlockSpec((CONV_W, CONV_DIM), fix2),
                  pl.BlockSpec((1, GDN_HEADS), fix2),
                  pl.BlockSpec((1, GDN_HEADS), fix2),
                  pl.BlockSpec((GDN_HEADS, 1), fix2),
                  pl.BlockSpec((GDN_HEADS, 1), fix2),
                  pl.BlockSpec((1, GDN_DV), fix2)],
        out_specs=[pl.BlockSpec((None, rb, hd), row_blk),
                   pl.BlockSpec((None, GDN_HEADS, GDN_DK, GDN_DV), lambda b, j: (b, 0, 0, 0))],
        out_shape=[jax.ShapeDtypeStruct((B, t_pad, hd), f32),
                   jax.ShapeDtypeStruct((B, GDN_HEADS, GDN_DK, GDN_DV), f32)],
        scratch_shapes=[pltpu.VMEM((rb + 8, CONV_DIM), f32),
                        pltpu.VMEM((rb, CONV_DIM), f32),
                        pltpu.VMEM((GDN_HEADS, GDN_DK, GDN_DV), f32)],
        compiler_params=pltpu.CompilerParams(
            dimension_semantics=("arbitrary", "arbitrary"),
            vmem_limit_bytes=56 * 1024 * 1024),
        name="gated_deltanet",
        interpret=interpret,
    )(x, a_p, b_p, a_t, gate_p, conv_buf, S0, conv_w,
      a_log.reshape(1, GDN_HEADS), dt_bias.reshape(1, GDN_HEADS),
      a_log.reshape(GDN_HEADS, 1), dt_bias.reshape(GDN_HEADS, 1), gdn_norm_w.reshape(1, GDN_DV))
    return ob[:, :T], s_out, qkv[:, T - (CONV_W - 1):, qkv_blk * CONV_DIM:(qkv_blk + 1) * CONV_DIM]


_SQRT_HALF = 0.7071067811865476


def _top_rows(x, k):
    R, n = x.shape
    ri = lax.broadcasted_iota(jnp.int32, (R, n), 0).astype(jnp.float32)
    ki = lax.broadcasted_iota(jnp.int32, (k, n), 0)

    def body(r, c):
        x, out, idx = c
        m = x.max(axis=0, keepdims=True)
        first = jnp.min(jnp.where(x == m, ri, float(R)), axis=0, keepdims=True)
        x = jnp.where(ri == first, -jnp.inf, x)
        return x, jnp.where(ki == r, m, out), jnp.where(ki == r, first, idx)

    zk = jnp.zeros((k, n), jnp.float32)
    rest, out, idx = lax.fori_loop(0, k, body, (x, zk, zk))
    return out, idx, rest.max(axis=0, keepdims=True)


def _peer_front_body(x_ref, oa_ref, ob_ref, wo_ref, g_ref, wq_ref, keys_ref,
                     h_ref, xn_ref, s1_ref, s2_ref, e2_ref, aux_ref, eidx_ref, gsm_ref, flag_ref, top_scr):
    f32, bf16 = jnp.float32, jnp.bfloat16
    half_w = ATTN_HEADS * HEAD_DIM
    h = (x_ref[...]
         + jnp.dot(oa_ref[...].astype(bf16), wo_ref[:half_w, :], preferred_element_type=f32)
         + jnp.dot(ob_ref[...].astype(bf16), wo_ref[half_w:, :], preferred_element_type=f32))
    h_ref[...] = h
    xn = (h * lax.rsqrt(jnp.mean(h * h, axis=-1, keepdims=True) + NORM_EPS) * g_ref[...]).astype(bf16)
    xn_ref[...] = xn
    qh = jnp.dot(xn, wq_ref[...], preferred_element_type=f32).astype(bf16)
    tq = qh.shape[0]
    hq = PEER_QDIM // 2
    K = PEER_TOPK
    for hh in range(PEER_HEADS):
        tops = []
        for half in range(2):
            col = (hh * 2 + half) * hq
            sT = lax.dot_general(keys_ref[half, hh], qh[:, col:col + hq], _NT,
                                 preferred_element_type=f32)
            (s1_ref if half == 0 else s2_ref)[hh] = sT
            tops.append(_top_rows(sT, K))
        (a16, ia, a_next), (b16, ib, b_next) = tops
        cand = jnp.concatenate(
            [a16[r:r + 1, :] + b16[0:8, :] for r in range(8)]
            + [a16[0:1, :] + b16[8:16, :], a16[8:16, :] + b16[0:1, :]], axis=0)
        tau = _top_rows(cand, K)[0][K - 1:K, :]
        top_sum = a16[0:1, :] + b16[0:1, :]
        keep = cand >= tau
        z = jnp.sum(jnp.where(keep, jnp.exp(cand - top_sum), 0.0), axis=0, keepdims=True)
        e2_ref[hh] = jnp.exp(s2_ref[hh] - b16[0:1, :]) / z
        aux_ref[hh] = jnp.concatenate([tau, a16[0:1, :], jnp.zeros((6, tq), f32)], axis=0)

        tied = (jnp.sum(jnp.where(keep, 1.0, 0.0), axis=0, keepdims=True) > K)
        tied = tied | ((a_next == a16[K - 1:K, :]) & (a16[K - 1:K, :] + b16[0:1, :] >= tau))
        tied = tied | ((b_next == b16[K - 1:K, :]) & (a16[0:1, :] + b16[K - 1:K, :] >= tau))
        head_flag = jnp.max(jnp.where(tied, 1.0, 0.0), axis=1, keepdims=True)
        flag_ref[0, hh:hh + 1, :] = jnp.broadcast_to(head_flag, (1, LANES))
        top_scr[hh] = jnp.concatenate([a16, ia, b16, ib], axis=0)

    any_tied = jnp.max(flag_ref[0]) > 0.0

    @pl.when(jnp.logical_not(any_tied))
    def _():
        eidx_ref[...] = jnp.zeros(eidx_ref.shape, jnp.int32)
        gsm_ref[...] = jnp.zeros(gsm_ref.shape, f32)

    @pl.when(any_tied)
    def _():
        for hh in range(PEER_HEADS):
            flagged = jnp.max(flag_ref[0, hh:hh + 1, :]) > 0.0

            @pl.when(jnp.logical_not(flagged))
            def _():
                eidx_ref[hh] = jnp.zeros((K, tq), jnp.int32)
                gsm_ref[hh] = jnp.zeros((K, tq), f32)

            @pl.when(flagged)
            def _():
                a16, ia = top_scr[hh, 0:K, :], top_scr[hh, K:2 * K, :]
                b16, ib = top_scr[hh, 2 * K:3 * K, :], top_scr[hh, 3 * K:4 * K, :]
                full = jnp.concatenate([a16[r:r + 1, :] + b16 for r in range(K)], axis=0)
                sv, flat, _ = _top_rows(full, K)
                ra = jnp.floor(flat * (1.0 / K))
                rb = flat - ra * K
                i1 = jnp.zeros((K, tq), f32)
                i2 = jnp.zeros((K, tq), f32)
                for r in range(K):
                    i1 = jnp.where(ra == r, ia[r:r + 1, :], i1)
                    i2 = jnp.where(rb == r, ib[r:r + 1, :], i2)
                eidx_ref[hh] = (i1 * PEER_NKEYS + i2).astype(jnp.int32)
                ex = jnp.exp(sv - sv[0:1, :])
                gsm_ref[hh] = ex / jnp.sum(ex, axis=0, keepdims=True)


def _peer_front(x, oa, ob, wo, g, wq, keys, tq, interpret=False):
    n = x.shape[0]
    half_w = ATTN_HEADS * HEAD_DIM
    tok = lambda i: (i, 0)
    fix2 = lambda i: (0, 0)
    colT = lambda i: (0, 0, i)
    f32 = jnp.float32
    plane = jax.ShapeDtypeStruct((PEER_HEADS, PEER_NKEYS, n), f32)
    plane_spec = pl.BlockSpec((PEER_HEADS, PEER_NKEYS, tq), colT)
    return pl.pallas_call(
        _peer_front_body,
        grid=(n // tq,),
        in_specs=[pl.BlockSpec((tq, D_MODEL), tok),
                  pl.BlockSpec((tq, half_w), tok),
                  pl.BlockSpec((tq, half_w), tok),
                  pl.BlockSpec((D_MODEL, D_MODEL), fix2),
                  pl.BlockSpec((1, D_MODEL), fix2),
                  pl.BlockSpec((D_MODEL, PEER_HEADS * PEER_QDIM), fix2),
                  pl.BlockSpec((2, PEER_HEADS, PEER_NKEYS, PEER_QDIM // 2), lambda i: (0, 0, 0, 0))],
        out_specs=[pl.BlockSpec((tq, D_MODEL), tok),
                   pl.BlockSpec((tq, D_MODEL), tok),
                   plane_spec, plane_spec, plane_spec,
                   pl.BlockSpec((PEER_HEADS, 8, tq), colT),
                   pl.BlockSpec((PEER_HEADS, PEER_TOPK, tq), colT),
                   pl.BlockSpec((PEER_HEADS, PEER_TOPK, tq), colT),
                   pl.BlockSpec((1, 8, LANES), lambda i: (i, 0, 0))],
        out_shape=[jax.ShapeDtypeStruct((n, D_MODEL), f32),
                   jax.ShapeDtypeStruct((n, D_MODEL), jnp.bfloat16),
                   plane, plane, plane,
                   jax.ShapeDtypeStruct((PEER_HEADS, 8, n), f32),
                   jax.ShapeDtypeStruct((PEER_HEADS, PEER_TOPK, n), jnp.int32),
                   jax.ShapeDtypeStruct((PEER_HEADS, PEER_TOPK, n), f32),
                   jax.ShapeDtypeStruct((n // tq, 8, LANES), f32)],
        scratch_shapes=[pltpu.VMEM((PEER_HEADS, 4 * PEER_TOPK, tq), f32)],
        compiler_params=pltpu.CompilerParams(
            dimension_semantics=("arbitrary",),
            vmem_limit_bytes=56 * 1024 * 1024),
        name="peer_front",
        interpret=interpret,
    )(x, oa, ob, wo, g.reshape(1, D_MODEL), wq, keys)


def _peer_dense_body(xn_ref, u_ref, vT_ref, s1_ref, s2_ref, e2_ref, aux_ref, eidx_ref, gsm_ref, flag_ref,
                     yT_ref, g_scr, *, eblk):
    f32 = jnp.float32
    eb = pl.program_id(1)
    tq = xn_ref.shape[0]
    sub = eblk // PEER_NKEYS

    @pl.when(eb == 0)
    def _():
        yT_ref[...] = jnp.zeros(yT_ref.shape, f32)

    def finish(gate_of):
        a = lax.dot_general(u_ref[...], xn_ref[...], _NT, preferred_element_type=f32)
        act = 0.5 * a * (1.0 + lax.erf(a * _SQRT_HALF))
        hT = jnp.concatenate(
            [(gate_of(r) * act[r * PEER_NKEYS:(r + 1) * PEER_NKEYS, :]).astype(jnp.bfloat16) for r in range(sub)],
            axis=0)
        yT_ref[...] += jnp.dot(vT_ref[...], hT, preferred_element_type=f32)

    exact_lists = jnp.max(flag_ref[...]) > 0.0

    def head_gate(r, hh):
        s1row = s1_ref[hh, pl.ds(eb * sub + r, 1), :]
        tau = aux_ref[hh, 0:1, :]
        e1row = jnp.exp(s1row - aux_ref[hh, 1:2, :])
        return jnp.where(s1row + s2_ref[hh] >= tau, e1row * e2_ref[hh], 0.0)

    @pl.when(jnp.logical_not(exact_lists))
    def _():
        finish(lambda r: sum(head_gate(r, hh) for hh in range(PEER_HEADS)))

    @pl.when(exact_lists)
    def _():
        n_flags = flag_ref.shape[0]
        row = lax.broadcasted_iota(jnp.int32, (PEER_NKEYS, tq), 0)
        for r in range(sub):
            base = (eb * sub + r) * PEER_NKEYS
            gate = jnp.zeros((PEER_NKEYS, tq), f32)
            for hh in range(PEER_HEADS):
                def with_list(g, hh=hh):
                    use_list = jnp.concatenate(
                        [jnp.broadcast_to(flag_ref[f, hh:hh + 1, 0:1], (1, tq // n_flags)) for f in range(n_flags)],
                        axis=1) > 0.0
                    ids = eidx_ref[hh] - base
                    wts = gsm_ref[hh]
                    listed = jnp.zeros((PEER_NKEYS, tq), f32)
                    for k in range(PEER_TOPK):
                        listed = listed + jnp.where(row == ids[k:k + 1, :], wts[k:k + 1, :], 0.0)
                    return g + jnp.where(use_list, listed, head_gate(r, hh))

                gate = lax.cond(jnp.max(flag_ref[:, hh:hh + 1, :]) > 0.0, with_list,
                                lambda g, hh=hh: g + head_gate(r, hh), gate)
            g_scr[r * PEER_NKEYS:(r + 1) * PEER_NKEYS, :] = gate
        finish(lambda r: g_scr[r * PEER_NKEYS:(r + 1) * PEER_NKEYS, :])


def _peer_dense(xn, u, vT, s1, s2, e2, aux, eidx, gsm, flag, tq, eblk, interpret=False):
    n = xn.shape[0]
    ne = u.shape[0]
    flags_per_tile = flag.shape[0] * tq // n
    colT = lambda i, e: (0, 0, i)
    plane_spec = pl.BlockSpec((PEER_HEADS, PEER_NKEYS, tq), colT)
    list_spec = pl.BlockSpec((PEER_HEADS, PEER_TOPK, tq), colT)
    return pl.pallas_call(
        functools.partial(_peer_dense_body, eblk=eblk),
        grid=(n // tq, ne // eblk),
        in_specs=[pl.BlockSpec((tq, D_MODEL), lambda i, e: (i, 0)),
                  pl.BlockSpec((eblk, D_MODEL), lambda i, e: (e, 0)),
                  pl.BlockSpec((D_MODEL, eblk), lambda i, e: (0, e)),
                  plane_spec, plane_spec, plane_spec,
                  pl.BlockSpec((PEER_HEADS, 8, tq), colT),
                  list_spec, list_spec,
                  pl.BlockSpec((flags_per_tile, 8, LANES), lambda i, e: (i, 0, 0))],
        out_specs=pl.BlockSpec((D_MODEL, tq), lambda i, e: (0, i)),
        out_shape=jax.ShapeDtypeStruct((D_MODEL, n), jnp.float32),
        scratch_shapes=[pltpu.VMEM((eblk, tq), jnp.float32)],
        compiler_params=pltpu.CompilerParams(
            dimension_semantics=("arbitrary", "arbitrary"),
            vmem_limit_bytes=56 * 1024 * 1024),
        name="peer_dense",
        interpret=interpret,
    )(xn, u, vT, s1, s2, e2, aux, eidx, gsm, flag)


def _layer_out_pallas(x, oa, ob, wo_b, ffn_norm_w, wq_b, keys_b, u_b, vT_b, interpret=False):
    n = x.shape[0]
    tq1 = min(256, n)
    tq2 = 512 if n % 512 == 0 else min(256, n)
    h, xn, s1, s2, e2, aux, eidx, gsm, flag = _peer_front(x, oa, ob, wo_b, ffn_norm_w, wq_b, keys_b, tq1, interpret)
    yT = _peer_dense(xn, u_b, vT_b, s1, s2, e2, aux, eidx, gsm, flag, tq2, 512, interpret)
    return h + yT.T


def kernel(x_prompt, x_sample, cache_k, cache_v, cache_idx_k, state_ssm, state_conv, page_table,
           attn_norm_w, w_in, q_norm_w, k_norm_w, idx_k_norm_w, conv_w, a_log, dt_bias, gdn_norm_w,
           w_out, ffn_norm_w, peer_wq, peer_keys, peer_u, peer_v):
    l = 0
    proj_w = (attn_norm_w[l], _permute_w_in(w_in[l]), q_norm_w[l], k_norm_w[l], idx_k_norm_w[l])
    gdn_cols = dict(qkv_blk=_DST["qkv"] // CONV_DIM, gate_blk=_DST["gate"] // (GDN_HEADS * GDN_DV))
    gdn_w = (conv_w[l], a_log[l], dt_bias[l], gdn_norm_w[l])
    bf16 = jnp.bfloat16
    out_w = (w_out[l].astype(bf16), ffn_norm_w[l], peer_wq[l].astype(bf16), peer_keys[l].astype(bf16),
             peer_u[l].astype(bf16), peer_v[l].astype(bf16).T)

    hp, hs = x_prompt, x_sample
    (qa, ka, va, qi, ki, wi), (z, b_raw, a_raw) = _in_projection(hp, *proj_w)
    oa = _dsa_prompt_pallas(qa[0], ka[0], va[0], qi[0], ki[0], wi[0])[None]
    Bp = hp.shape[0]
    buf0 = jnp.zeros((Bp, CONV_W - 1, CONV_DIM), hp.dtype)
    S0 = jnp.zeros((Bp, GDN_HEADS, GDN_DK, GDN_DV), jnp.float32)
    ob, S_p, buf_p = _gdn_pallas(z, b_raw, a_raw, z, buf0, S0, *gdn_w, **gdn_cols)
    half_w = ATTN_HEADS * HEAD_DIM
    hp = _layer_out_pallas(hp[0], oa[0], ob[0], *out_w)[None]
    kp, vp, ip = ka, va, ki

    (qa, ka, va, qi, ki, wi), (z, b_raw, a_raw) = _in_projection(hs, *proj_w)
    oa = _dsa_sample_pallas(qa, ka, va, qi, ki, wi, cache_k[l], cache_v[l], cache_idx_k[l], page_table)
    ob, S_s, buf_s = _gdn_pallas(z, b_raw, a_raw, z, state_conv[l], state_ssm[l].astype(jnp.float32),
                                 *gdn_w, **gdn_cols)
    ns = hs.shape[0] * hs.shape[1]
    hs = _layer_out_pallas(hs.reshape(ns, D_MODEL), oa.reshape(ns, half_w), ob.reshape(ns, half_w),
                           *out_w).reshape(hs.shape)

    return (hp, hs, kp[None], vp[None], ip[None], S_p[None], buf_p[None],
            ka[None], va[None], ki[None], S_s[None], buf_s[None])
```
